```python
import jax, jax.numpy as jnp
from jax import lax
import numpy as np


D_MODEL = 1024
BATCH = 8
SEQ = 4096
DEPTH = 4

CHUNK = 64
Q_BLOCK = 128
NORM_EPS = 1e-6
D_FF = 256 * ((8 * D_MODEL // 3 + 255) // 256)

D_RNN = D_MODEL
RG_BLOCKS = 16
RG_BLOCK_W = D_RNN // RG_BLOCKS
CONV_W = 4
RG_C = 8.0

SB_HEAD_DIM = 128
SB_HEADS = D_MODEL // 128
SB_W = SB_HEADS * SB_HEAD_DIM

MLA_HEADS = 8
MLA_Q_LORA = D_MODEL // 4
MLA_KV_LORA = D_MODEL // 4
MLA_NOPE = 128
MLA_ROPE = 64
MLA_V = 128
MLA_V_W = MLA_HEADS * MLA_V
ROPE_THETA = 10000.0

N_BRANCHES = 3
IN_SPLITS = (D_RNN, D_RNN, SB_W, SB_W, SB_W, MLA_Q_LORA, MLA_KV_LORA, MLA_ROPE, N_BRANCHES * D_MODEL)
N_IN = D_RNN * 2 + SB_W * 3 + MLA_Q_LORA + MLA_KV_LORA + MLA_ROPE + N_BRANCHES * D_MODEL

kernel_name = 'hybrid_rglru_stickbreak_mla_macaron_trunk'


def rms_norm(x, g):
    xf = x.astype(jnp.float32)
    y = xf * lax.rsqrt(jnp.mean(xf * xf, axis=-1, keepdims=True) + NORM_EPS)
    return (y * g.astype(jnp.float32)).astype(x.dtype)


def swiglu_ffn(h, w_gate_up, w_down):
    gate, up = jnp.split(h @ w_gate_up, 2, axis=-1)
    return (jax.nn.silu(gate) * up) @ w_down


def causal_depthwise_conv(u, w, b):
    S = u.shape[1]
    up = jnp.pad(u, ((0, 0), (CONV_W - 1, 0), (0, 0)))
    out = b
    for tap in range(CONV_W):
        out = out + up[:, tap:tap + S] * w[tap]
    return out


def _lin_rec_combine(earlier, later):
    a1, b1 = earlier
    a2, b2 = later
    return a1 * a2, a2 * b1 + b2


def rg_lru(u, w_a, b_a, w_x, b_x, lam):
    B, S, _ = u.shape
    ub = u.reshape(B, S, RG_BLOCKS, RG_BLOCK_W)
    r = jax.nn.sigmoid(jnp.einsum('bsnj,njk->bsnk', ub, w_a).reshape(B, S, D_RNN) + b_a)
    i = jax.nn.sigmoid(jnp.einsum('bsnj,njk->bsnk', ub, w_x).reshape(B, S, D_RNN) + b_x)
    log_a = (-RG_C * jax.nn.softplus(-lam.astype(jnp.float32))) * r.astype(jnp.float32)
    a = jnp.exp(log_a)
    b = jnp.sqrt(-jnp.expm1(2.0 * log_a)) * (i * u).astype(jnp.float32)
    _, h = lax.associative_scan(_lin_rec_combine, (a, b), axis=1)
    return h.astype(u.dtype)


def stick_breaking_attention(q, k, v):
    S = q.shape[1]
    scale = SB_HEAD_DIM ** -0.5
    outs = []
    for blk in range(S // Q_BLOCK):
        q0 = blk * Q_BLOCK
        k_end = q0 + Q_BLOCK
        z = jnp.einsum('bqhd,bkhd->bhqk', q[:, q0:k_end], k[:, :k_end]).astype(jnp.float32) * scale
        q_pos = q0 + jnp.arange(Q_BLOCK)
        k_pos = jnp.arange(k_end)
        earlier = k_pos[None, :] < q_pos[:, None]
        log_keep = jnp.where(earlier, jax.nn.log_sigmoid(-z), 0.0)
        between = lax.cumsum(log_keep, axis=3, reverse=True) - log_keep
        w = jnp.where(earlier, jnp.exp(jax.nn.log_sigmoid(z) + between), 0.0)
        outs.append(jnp.einsum('bhqk,bkhd->bqhd', w.astype(v.dtype), v[:, :k_end]))
    return jnp.concatenate(outs, axis=1)


def chunk_causal_softmax_attention(q, k, v, scale):
    S = q.shape[1]
    outs = []
    for blk in range(S // Q_BLOCK):
        q0 = blk * Q_BLOCK
        k_end = q0 + Q_BLOCK
        s = jnp.einsum('bqhd,bkhd->bhqk', q[:, q0:k_end], k[:, :k_end]).astype(jnp.float32) * scale
        q_chunk = (q0 + jnp.arange(Q_BLOCK)) // CHUNK
        k_chunk = jnp.arange(k_end) // CHUNK
        s = jnp.where(k_chunk[None, :] <= q_chunk[:, None], s, -jnp.inf)
        p = jax.nn.softmax(s, axis=-1).astype(v.dtype)
        outs.append(jnp.einsum('bhqk,bkhd->bqhd', p, v[:, :k_end]))
    return jnp.concatenate(outs, axis=1)


def rope_tables(positions, dtype):
    inv = ROPE_THETA ** (-jnp.arange(0, MLA_ROPE, 2, dtype=jnp.float32) / MLA_ROPE)
    ang = positions.astype(jnp.float32)[..., None] * inv
    return jnp.cos(ang).astype(dtype), jnp.sin(ang).astype(dtype)


def apply_rope(x, cos, sin):
    x1, x2 = jnp.split(x, 2, axis=-1)
    return jnp.concatenate([x1 * cos - x2 * sin, x2 * cos + x1 * sin], axis=-1)


def mla_branch(c_q_raw, c_kv_raw, k_rope_raw, q_norm, w_uq, kv_norm, w_ukv, cos, sin):
    B, S, _ = c_q_raw.shape
    q = (rms_norm(c_q_raw, q_norm) @ w_uq).reshape(B, S, MLA_HEADS, MLA_NOPE + MLA_ROPE)
    q_nope, q_rope = jnp.split(q, [MLA_NOPE], axis=-1)
    q_rope = apply_rope(q_rope, cos[:, :, None, :], sin[:, :, None, :])
    kv = (rms_norm(c_kv_raw, kv_norm) @ w_ukv).reshape(B, S, MLA_HEADS, MLA_NOPE + MLA_V)
    k_nope, v = jnp.split(kv, [MLA_NOPE], axis=-1)
    k_rope = apply_rope(k_rope_raw, cos, sin)
    k = jnp.concatenate([k_nope, jnp.broadcast_to(k_rope[:, :, None, :], (B, S, MLA_HEADS, MLA_ROPE))], axis=-1)
    q = jnp.concatenate([q_nope, q_rope], axis=-1)
    o = chunk_causal_softmax_attention(q, k, v, (MLA_NOPE + MLA_ROPE) ** -0.5)
    return o.reshape(B, S, MLA_V_W)


def _fwd_setup_inputs(seed: int = 0) -> dict:
    key = jax.random.key(seed)
    ks = iter(jax.random.split(key, 48))
    L, D = DEPTH, D_MODEL

    def normal(shape, fan_in):
        return jax.random.normal(next(ks), shape, jnp.float32) * (fan_in ** -0.5)

    def gain(shape):
        return 1.0 + 0.02 * jax.random.normal(next(ks), shape, jnp.float32)

    def bias(shape):
        return 0.02 * jax.random.normal(next(ks), shape, jnp.float32)

    x = jax.random.normal(next(ks), (BATCH, SEQ, D), jnp.float32)
    offsets = jax.random.randint(next(ks), (BATCH, 1), 0, 16384, dtype=jnp.int32)
    positions = (offsets + jnp.arange(SEQ, dtype=jnp.int32)[None, :]).astype(jnp.int32)
    a0 = jax.random.uniform(next(ks), (L, D_RNN), jnp.float32, minval=0.9, maxval=0.999)
    base = a0 ** (1.0 / RG_C)
    rg_lambda = jnp.log(base) - jnp.log1p(-base)
    return {
        'x': x,
        'positions': positions,
        'ffn1_norm': gain((L, D)),
        'ffn1_w_gate_up': normal((L, D, 2 * D_FF), D),
        'ffn1_w_down': normal((L, D_FF, D), D_FF),
        'mix_norm': gain((L, D)),
        'w_in': normal((L, D, N_IN), D),
        'conv_w': normal((L, CONV_W, D_RNN), CONV_W),
        'conv_b': bias((L, D_RNN)),
        'rg_w_a': normal((L, RG_BLOCKS, RG_BLOCK_W, RG_BLOCK_W), RG_BLOCK_W),
        'rg_b_a': bias((L, D_RNN)),
        'rg_w_x': normal((L, RG_BLOCKS, RG_BLOCK_W, RG_BLOCK_W), RG_BLOCK_W),
        'rg_b_x': bias((L, D_RNN)),
        'rg_lambda': rg_lambda,
        'mla_q_norm': gain((L, MLA_Q_LORA)),
        'mla_w_uq': normal((L, MLA_Q_LORA, MLA_HEADS * (MLA_NOPE + MLA_ROPE)), MLA_Q_LORA),
        'mla_kv_norm': gain((L, MLA_KV_LORA)),
        'mla_w_ukv': normal((L, MLA_KV_LORA, MLA_HEADS * (MLA_NOPE + MLA_V)), MLA_KV_LORA),
        'w_branch_a': normal((L, D_RNN, D), D_RNN),
        'w_branch_b': normal((L, SB_W, D), SB_W),
        'w_branch_c': normal((L, MLA_V_W, D), MLA_V_W),
        'w_out': normal((L, D, D), D),
        'ffn2_norm': gain((L, D)),
        'ffn2_w_gate_up': normal((L, D, 2 * D_FF), D),
        'ffn2_w_down': normal((L, D_FF, D), D_FF),
        'final_norm': gain((D,)),
    }


def _fwd_reference(x, positions, ffn1_norm, ffn1_w_gate_up, ffn1_w_down, mix_norm, w_in,
              conv_w, conv_b, rg_w_a, rg_b_a, rg_w_x, rg_b_x, rg_lambda,
              mla_q_norm, mla_w_uq, mla_kv_norm, mla_w_ukv,
              w_branch_a, w_branch_b, w_branch_c, w_out,
              ffn2_norm, ffn2_w_gate_up, ffn2_w_down, final_norm):
    B, S, _ = x.shape
    split_points = []
    acc = 0
    for width in IN_SPLITS[:-1]:
        acc += width
        split_points.append(acc)
    cos, sin = rope_tables(positions, x.dtype)

    for l in range(DEPTH):
        x = x + 0.5 * swiglu_ffn(rms_norm(x, ffn1_norm[l]), ffn1_w_gate_up[l], ffn1_w_down[l])

        h = rms_norm(x, mix_norm[l])
        rg_x, rg_g, sb_q, sb_k, sb_v, c_q, c_kv, k_r, gate_logits = jnp.split(h @ w_in[l], split_points, axis=-1)

        u = causal_depthwise_conv(rg_x, conv_w[l], conv_b[l])
        y_a = rg_lru(u, rg_w_a[l], rg_b_a[l], rg_w_x[l], rg_b_x[l], rg_lambda[l]) * jax.nn.gelu(rg_g)

        hd = (B, S, SB_HEADS, SB_HEAD_DIM)
        y_b = stick_breaking_attention(sb_q.reshape(hd), sb_k.reshape(hd), sb_v.reshape(hd)).reshape(B, S, SB_W)

        y_c = mla_branch(c_q, c_kv, k_r, mla_q_norm[l], mla_w_uq[l], mla_kv_norm[l], mla_w_ukv[l], cos, sin)

        g_a, g_b, g_c = jnp.split(jax.nn.sigmoid(gate_logits), N_BRANCHES, axis=-1)
        merged = g_a * (y_a @ w_branch_a[l]) + g_b * (y_b @ w_branch_b[l]) + g_c * (y_c @ w_branch_c[l])
        x = x + merged @ w_out[l]

        x = x + 0.5 * swiglu_ffn(rms_norm(x, ffn2_norm[l]), ffn2_w_gate_up[l], ffn2_w_down[l])

    return rms_norm(x, final_norm)


import jax as _jax
import jax.numpy as _jnp

TWIN_FORMAT = 'train_step'
FWD_PARAMS = ['x', 'positions', 'ffn1_norm', 'ffn1_w_gate_up', 'ffn1_w_down', 'mix_norm', 'w_in', 'conv_w', 'conv_b', 'rg_w_a', 'rg_b_a', 'rg_w_x', 'rg_b_x', 'rg_lambda', 'mla_q_norm', 'mla_w_uq', 'mla_kv_norm', 'mla_w_ukv', 'w_branch_a', 'w_branch_b', 'w_branch_c', 'w_out', 'ffn2_norm', 'ffn2_w_gate_up', 'ffn2_w_down', 'final_norm']
TWIN_WEIGHTS = ['ffn1_norm', 'ffn1_w_gate_up', 'ffn1_w_down', 'mix_norm', 'w_in', 'conv_w', 'conv_b', 'rg_w_a', 'rg_b_a', 'rg_w_x', 'rg_b_x', 'rg_lambda', 'mla_q_norm', 'mla_w_uq', 'mla_kv_norm', 'mla_w_ukv', 'w_branch_a', 'w_branch_b', 'w_branch_c', 'w_out', 'ffn2_norm', 'ffn2_w_gate_up', 'ffn2_w_down', 'final_norm']
TWIN_DIFF_INPUT = 'x'
TWIN_INPUTS = ['x', 'positions', 'ffn1_norm', 'ffn1_w_gate_up', 'ffn1_w_down', 'mix_norm', 'w_in', 'conv_w', 'conv_b', 'rg_w_a', 'rg_b_a', 'rg_w_x', 'rg_b_x', 'rg_lambda', 'mla_q_norm', 'mla_w_uq', 'mla_kv_norm', 'mla_w_ukv', 'w_branch_a', 'w_branch_b', 'w_branch_c', 'w_out', 'ffn2_norm', 'ffn2_w_gate_up', 'ffn2_w_down', 'final_norm', 'loss_target', 'm_ffn1_norm', 'm_ffn1_w_gate_up', 'm_ffn1_w_down', 'm_mix_norm', 'm_w_in', 'm_conv_w', 'm_conv_b', 'm_rg_w_a', 'm_rg_b_a', 'm_rg_w_x', 'm_rg_b_x', 'm_rg_lambda', 'm_mla_q_norm', 'm_mla_w_uq', 'm_mla_kv_norm', 'm_mla_w_ukv', 'm_w_branch_a', 'm_w_branch_b', 'm_w_branch_c', 'm_w_out', 'm_ffn2_norm', 'm_ffn2_w_gate_up', 'm_ffn2_w_down', 'm_final_norm', 'v_ffn1_norm', 'v_ffn1_w_gate_up', 'v_ffn1_w_down', 'v_mix_norm', 'v_w_in', 'v_conv_w', 'v_conv_b', 'v_rg_w_a', 'v_rg_b_a', 'v_rg_w_x', 'v_rg_b_x', 'v_rg_lambda', 'v_mla_q_norm', 'v_mla_w_uq', 'v_mla_kv_norm', 'v_mla_w_ukv', 'v_w_branch_a', 'v_w_branch_b', 'v_w_branch_c', 'v_w_out', 'v_ffn2_norm', 'v_ffn2_w_gate_up', 'v_ffn2_w_down', 'v_final_norm']
TWIN_OUTPUTS = ['loss', 'grad_x', 'grad_ffn1_norm', 'grad_ffn1_w_gate_up', 'grad_ffn1_w_down', 'grad_mix_norm', 'grad_w_in', 'grad_conv_w', 'grad_conv_b', 'grad_rg_w_a', 'grad_rg_b_a', 'grad_rg_w_x', 'grad_rg_b_x', 'grad_rg_lambda', 'grad_mla_q_norm', 'grad_mla_w_uq', 'grad_mla_kv_norm', 'grad_mla_w_ukv', 'grad_w_branch_a', 'grad_w_branch_b', 'grad_w_branch_c', 'grad_w_out', 'grad_ffn2_norm', 'grad_ffn2_w_gate_up', 'grad_ffn2_w_down', 'grad_final_norm', 'delta_ffn1_norm', 'delta_ffn1_w_gate_up', 'delta_ffn1_w_down', 'delta_mix_norm', 'delta_w_in', 'delta_conv_w', 'delta_conv_b', 'delta_rg_w_a', 'delta_rg_b_a', 'delta_rg_w_x', 'delta_rg_b_x', 'delta_rg_lambda', 'delta_mla_q_norm', 'delta_mla_w_uq', 'delta_mla_kv_norm', 'delta_mla_w_ukv', 'delta_w_branch_a', 'delta_w_branch_b', 'delta_w_branch_c', 'delta_w_out', 'delta_ffn2_norm', 'delta_ffn2_w_gate_up', 'delta_ffn2_w_down', 'delta_final_norm', 'new_m_ffn1_norm', 'new_m_ffn1_w_gate_up', 'new_m_ffn1_w_down', 'new_m_mix_norm', 'new_m_w_in', 'new_m_conv_w', 'new_m_conv_b', 'new_m_rg_w_a', 'new_m_rg_b_a', 'new_m_rg_w_x', 'new_m_rg_b_x', 'new_m_rg_lambda', 'new_m_mla_q_norm', 'new_m_mla_w_uq', 'new_m_mla_kv_norm', 'new_m_mla_w_ukv', 'new_m_w_branch_a', 'new_m_w_branch_b', 'new_m_w_branch_c', 'new_m_w_out', 'new_m_ffn2_norm', 'new_m_ffn2_w_gate_up', 'new_m_ffn2_w_down', 'new_m_final_norm', 'new_v_ffn1_norm', 'new_v_ffn1_w_gate_up', 'new_v_ffn1_w_down', 'new_v_mix_norm', 'new_v_w_in', 'new_v_conv_w', 'new_v_conv_b', 'new_v_rg_w_a', 'new_v_rg_b_a', 'new_v_rg_w_x', 'new_v_rg_b_x', 'new_v_rg_lambda', 'new_v_mla_q_norm', 'new_v_mla_w_uq', 'new_v_mla_kv_norm', 'new_v_mla_w_ukv', 'new_v_w_branch_a', 'new_v_w_branch_b', 'new_v_w_branch_c', 'new_v_w_out', 'new_v_ffn2_norm', 'new_v_ffn2_w_gate_up', 'new_v_ffn2_w_down', 'new_v_final_norm']
TWIN_LEAF_KINDS = {'loss': 'loss', 'grad_x': 'grad_x', 'grad_ffn1_norm': 'grad_w', 'grad_ffn1_w_gate_up': 'grad_w', 'grad_ffn1_w_down': 'grad_w', 'grad_mix_norm': 'grad_w', 'grad_w_in': 'grad_w', 'grad_conv_w': 'grad_w', 'grad_conv_b': 'grad_w', 'grad_rg_w_a': 'grad_w', 'grad_rg_b_a': 'grad_w', 'grad_rg_w_x': 'grad_w', 'grad_rg_b_x': 'grad_w', 'grad_rg_lambda': 'grad_w', 'grad_mla_q_norm': 'grad_w', 'grad_mla_w_uq': 'grad_w', 'grad_mla_kv_norm': 'grad_w', 'grad_mla_w_ukv': 'grad_w', 'grad_w_branch_a': 'grad_w', 'grad_w_branch_b': 'grad_w', 'grad_w_branch_c': 'grad_w', 'grad_w_out': 'grad_w', 'grad_ffn2_norm': 'grad_w', 'grad_ffn2_w_gate_up': 'grad_w', 'grad_ffn2_w_down': 'grad_w', 'grad_final_norm': 'grad_w', 'delta_ffn1_norm': 'delta_w', 'delta_ffn1_w_gate_up': 'delta_w', 'delta_ffn1_w_down': 'delta_w', 'delta_mix_norm': 'delta_w', 'delta_w_in': 'delta_w', 'delta_conv_w': 'delta_w', 'delta_conv_b': 'delta_w', 'delta_rg_w_a': 'delta_w', 'delta_rg_b_a': 'delta_w', 'delta_rg_w_x': 'delta_w', 'delta_rg_b_x': 'delta_w', 'delta_rg_lambda': 'delta_w', 'delta_mla_q_norm': 'delta_w', 'delta_mla_w_uq': 'delta_w', 'delta_mla_kv_norm': 'delta_w', 'delta_mla_w_ukv': 'delta_w', 'delta_w_branch_a': 'delta_w', 'delta_w_branch_b': 'delta_w', 'delta_w_branch_c': 'delta_w', 'delta_w_out': 'delta_w', 'delta_ffn2_norm': 'delta_w', 'delta_ffn2_w_gate_up': 'delta_w', 'delta_ffn2_w_down': 'delta_w', 'delta_final_norm': 'delta_w', 'new_m_ffn1_norm': 'new_m', 'new_m_ffn1_w_gate_up': 'new_m', 'new_m_ffn1_w_down': 'new_m', 'new_m_mix_norm': 'new_m', 'new_m_w_in': 'new_m', 'new_m_conv_w': 'new_m', 'new_m_conv_b': 'new_m', 'new_m_rg_w_a': 'new_m', 'new_m_rg_b_a': 'new_m', 'new_m_rg_w_x': 'new_m', 'new_m_rg_b_x': 'new_m', 'new_m_rg_lambda': 'new_m', 'new_m_mla_q_norm': 'new_m', 'new_m_mla_w_uq': 'new_m', 'new_m_mla_kv_norm': 'new_m', 'new_m_mla_w_ukv': 'new_m', 'new_m_w_branch_a': 'new_m', 'new_m_w_branch_b': 'new_m', 'new_m_w_branch_c': 'new_m', 'new_m_w_out': 'new_m', 'new_m_ffn2_norm': 'new_m', 'new_m_ffn2_w_gate_up': 'new_m', 'new_m_ffn2_w_down': 'new_m', 'new_m_final_norm': 'new_m', 'new_v_ffn1_norm': 'new_v', 'new_v_ffn1_w_gate_up': 'new_v', 'new_v_ffn1_w_down': 'new_v', 'new_v_mix_norm': 'new_v', 'new_v_w_in': 'new_v', 'new_v_conv_w': 'new_v', 'new_v_conv_b': 'new_v', 'new_v_rg_w_a': 'new_v', 'new_v_rg_b_a': 'new_v', 'new_v_rg_w_x': 'new_v', 'new_v_rg_b_x': 'new_v', 'new_v_rg_lambda': 'new_v', 'new_v_mla_q_norm': 'new_v', 'new_v_mla_w_uq': 'new_v', 'new_v_mla_kv_norm': 'new_v', 'new_v_mla_w_ukv': 'new_v', 'new_v_w_branch_a': 'new_v', 'new_v_w_branch_b': 'new_v', 'new_v_w_branch_c': 'new_v', 'new_v_w_out': 'new_v', 'new_v_ffn2_norm': 'new_v', 'new_v_ffn2_w_gate_up': 'new_v', 'new_v_ffn2_w_down': 'new_v', 'new_v_final_norm': 'new_v'}


def _forward(args):
    return _fwd_reference(*[args[k] for k in FWD_PARAMS])


def _output_shape():
    def fwd():
        inp = _fwd_setup_inputs(0)
        return _fwd_reference(*[inp[k] for k in FWD_PARAMS])
    out = _jax.eval_shape(fwd)
    return out.shape, out.dtype

N_MICROBATCH = 1
ADAM_LR = 0.001
ADAM_B1 = 0.9
ADAM_B2 = 0.999
ADAM_EPS = 1e-08
ADAM_WD = 0.01
ADAM_STEP = 10
PER_EXAMPLE_BATCH_AXIS = {'x': 0, 'positions': 0, 'loss_target': 0}
SHARED_INPUTS = []
_WEIGHT_DTYPES = {'ffn1_norm': _jnp.float32, 'ffn1_w_gate_up': _jnp.float32, 'ffn1_w_down': _jnp.float32, 'mix_norm': _jnp.float32, 'w_in': _jnp.float32, 'conv_w': _jnp.float32, 'conv_b': _jnp.float32, 'rg_w_a': _jnp.float32, 'rg_b_a': _jnp.float32, 'rg_w_x': _jnp.float32, 'rg_b_x': _jnp.float32, 'rg_lambda': _jnp.float32, 'mla_q_norm': _jnp.float32, 'mla_w_uq': _jnp.float32, 'mla_kv_norm': _jnp.float32, 'mla_w_ukv': _jnp.float32, 'w_branch_a': _jnp.float32, 'w_branch_b': _jnp.float32, 'w_branch_c': _jnp.float32, 'w_out': _jnp.float32, 'ffn2_norm': _jnp.float32, 'ffn2_w_gate_up': _jnp.float32, 'ffn2_w_down': _jnp.float32, 'final_norm': _jnp.float32}
MOMENT_SCALE = {'ffn1_norm': 7.867283e-02, 'ffn1_w_gate_up': 3.271738e-02, 'ffn1_w_down': 5.339120e-02, 'mix_norm': 9.929695e-02, 'w_in': 3.411288e-02, 'conv_w': 5.663757e-02, 'conv_b': 5.626590e-01, 'rg_w_a': 1.914802e-02, 'rg_b_a': 1.485960e-02, 'rg_w_x': 3.528067e-02, 'rg_b_x': 1.990864e-02, 'rg_lambda': 2.894859e-02, 'mla_q_norm': 2.562935e-02, 'mla_w_uq': 1.056668e-02, 'mla_kv_norm': 4.660813e-02, 'mla_w_ukv': 1.496486e-02, 'w_branch_a': 5.654916e-02, 'w_branch_b': 5.815694e-02, 'w_branch_c': 1.805712e-02, 'w_out': 7.976179e-02, 'ffn2_norm': 6.548075e-02, 'ffn2_w_gate_up': 2.784073e-02, 'ffn2_w_down': 4.543419e-02, 'final_norm': 3.201342e+01}


def _to_microbatches(a, axis):
    t = _jnp.moveaxis(a, axis, 0)
    t = t.reshape((N_MICROBATCH, t.shape[0] // N_MICROBATCH) + t.shape[1:])
    return _jnp.moveaxis(t, 1, axis + 1)


def setup_inputs(seed: int = 0) -> dict:
    inp = _fwd_setup_inputs(seed)
    key = _jax.random.fold_in(_jax.random.key(seed), 7919)
    shape, _ = _output_shape()
    out = dict(inp)
    out["loss_target"] = _jax.random.normal(_jax.random.fold_in(key, 0), shape, _jnp.float32)
    for i, name in enumerate(TWIN_WEIGHTS):
        w = inp[name].astype(_jnp.float32)
        if MOMENT_SCALE is None:
            s = _jnp.sqrt(_jnp.mean(_jnp.square(w)) + 1e-30)
        else:
            s = MOMENT_SCALE[name]
        km, kv = _jax.random.split(_jax.random.fold_in(key, i + 1))
        out[name] = w
        out["m_" + name] = s * _jax.random.normal(km, w.shape, _jnp.float32)
        out["v_" + name] = (s * s) * _jax.random.uniform(kv, w.shape, _jnp.float32, 0.5, 1.5)
    if N_MICROBATCH > 1:
        for name, axis in PER_EXAMPLE_BATCH_AXIS.items():
            out[name] = _to_microbatches(out[name], axis)
    return {'x': out['x'], 'positions': out['positions'], 'ffn1_norm': out['ffn1_norm'], 'ffn1_w_gate_up': out['ffn1_w_gate_up'], 'ffn1_w_down': out['ffn1_w_down'], 'mix_norm': out['mix_norm'], 'w_in': out['w_in'], 'conv_w': out['conv_w'], 'conv_b': out['conv_b'], 'rg_w_a': out['rg_w_a'], 'rg_b_a': out['rg_b_a'], 'rg_w_x': out['rg_w_x'], 'rg_b_x': out['rg_b_x'], 'rg_lambda': out['rg_lambda'], 'mla_q_norm': out['mla_q_norm'], 'mla_w_uq': out['mla_w_uq'], 'mla_kv_norm': out['mla_kv_norm'], 'mla_w_ukv': out['mla_w_ukv'], 'w_branch_a': out['w_branch_a'], 'w_branch_b': out['w_branch_b'], 'w_branch_c': out['w_branch_c'], 'w_out': out['w_out'], 'ffn2_norm': out['ffn2_norm'], 'ffn2_w_gate_up': out['ffn2_w_gate_up'], 'ffn2_w_down': out['ffn2_w_down'], 'final_norm': out['final_norm'], 'loss_target': out['loss_target'], 'm_ffn1_norm': out['m_ffn1_norm'], 'm_ffn1_w_gate_up': out['m_ffn1_w_gate_up'], 'm_ffn1_w_down': out['m_ffn1_w_down'], 'm_mix_norm': out['m_mix_norm'], 'm_w_in': out['m_w_in'], 'm_conv_w': out['m_conv_w'], 'm_conv_b': out['m_conv_b'], 'm_rg_w_a': out['m_rg_w_a'], 'm_rg_b_a': out['m_rg_b_a'], 'm_rg_w_x': out['m_rg_w_x'], 'm_rg_b_x': out['m_rg_b_x'], 'm_rg_lambda': out['m_rg_lambda'], 'm_mla_q_norm': out['m_mla_q_norm'], 'm_mla_w_uq': out['m_mla_w_uq'], 'm_mla_kv_norm': out['m_mla_kv_norm'], 'm_mla_w_ukv': out['m_mla_w_ukv'], 'm_w_branch_a': out['m_w_branch_a'], 'm_w_branch_b': out['m_w_branch_b'], 'm_w_branch_c': out['m_w_branch_c'], 'm_w_out': out['m_w_out'], 'm_ffn2_norm': out['m_ffn2_norm'], 'm_ffn2_w_gate_up': out['m_ffn2_w_gate_up'], 'm_ffn2_w_down': out['m_ffn2_w_down'], 'm_final_norm': out['m_final_norm'], 'v_ffn1_norm': out['v_ffn1_norm'], 'v_ffn1_w_gate_up': out['v_ffn1_w_gate_up'], 'v_ffn1_w_down': out['v_ffn1_w_down'], 'v_mix_norm': out['v_mix_norm'], 'v_w_in': out['v_w_in'], 'v_conv_w': out['v_conv_w'], 'v_conv_b': out['v_conv_b'], 'v_rg_w_a': out['v_rg_w_a'], 'v_rg_b_a': out['v_rg_b_a'], 'v_rg_w_x': out['v_rg_w_x'], 'v_rg_b_x': out['v_rg_b_x'], 'v_rg_lambda': out['v_rg_lambda'], 'v_mla_q_norm': out['v_mla_q_norm'], 'v_mla_w_uq': out['v_mla_w_uq'], 'v_mla_kv_norm': out['v_mla_kv_norm'], 'v_mla_w_ukv': out['v_mla_w_ukv'], 'v_w_branch_a': out['v_w_branch_a'], 'v_w_branch_b': out['v_w_branch_b'], 'v_w_branch_c': out['v_w_branch_c'], 'v_w_out': out['v_w_out'], 'v_ffn2_norm': out['v_ffn2_norm'], 'v_ffn2_w_gate_up': out['v_ffn2_w_gate_up'], 'v_ffn2_w_down': out['v_ffn2_w_down'], 'v_final_norm': out['v_final_norm']}


def _loss(weights, diff, rest, loss_target):
    with _jax.named_scope("forward"):
        args = {**rest, TWIN_DIFF_INPUT: diff, **{k: w.astype(_WEIGHT_DTYPES[k]) for k, w in weights.items()}}
        y = _forward(args)
    with _jax.named_scope("loss_head"):
        err = _jnp.square(y.astype(_jnp.float32) - loss_target)
        return 0.5 * _jnp.sum(_jnp.mean(err, axis=-1)) if err.ndim else 0.5 * err


def _adamw(w, g, m, v):
    m = ADAM_B1 * m + (1.0 - ADAM_B1) * g
    v = ADAM_B2 * v + (1.0 - ADAM_B2) * _jnp.square(g)
    m_hat = m / (1.0 - ADAM_B1 ** ADAM_STEP)
    v_hat = v / (1.0 - ADAM_B2 ** ADAM_STEP)
    delta = -ADAM_LR * (m_hat / (_jnp.sqrt(v_hat) + ADAM_EPS) + ADAM_WD * w)
    return delta, m, v


def reference(x, positions, ffn1_norm, ffn1_w_gate_up, ffn1_w_down, mix_norm, w_in, conv_w, conv_b, rg_w_a, rg_b_a, rg_w_x, rg_b_x, rg_lambda, mla_q_norm, mla_w_uq, mla_kv_norm, mla_w_ukv, w_branch_a, w_branch_b, w_branch_c, w_out, ffn2_norm, ffn2_w_gate_up, ffn2_w_down, final_norm, loss_target, m_ffn1_norm, m_ffn1_w_gate_up, m_ffn1_w_down, m_mix_norm, m_w_in, m_conv_w, m_conv_b, m_rg_w_a, m_rg_b_a, m_rg_w_x, m_rg_b_x, m_rg_lambda, m_mla_q_norm, m_mla_w_uq, m_mla_kv_norm, m_mla_w_ukv, m_w_branch_a, m_w_branch_b, m_w_branch_c, m_w_out, m_ffn2_norm, m_ffn2_w_gate_up, m_ffn2_w_down, m_final_norm, v_ffn1_norm, v_ffn1_w_gate_up, v_ffn1_w_down, v_mix_norm, v_w_in, v_conv_w, v_conv_b, v_rg_w_a, v_rg_b_a, v_rg_w_x, v_rg_b_x, v_rg_lambda, v_mla_q_norm, v_mla_w_uq, v_mla_kv_norm, v_mla_w_ukv, v_w_branch_a, v_w_branch_b, v_w_branch_c, v_w_out, v_ffn2_norm, v_ffn2_w_gate_up, v_ffn2_w_down, v_final_norm):
    given = dict(x=x, positions=positions, ffn1_norm=ffn1_norm, ffn1_w_gate_up=ffn1_w_gate_up, ffn1_w_down=ffn1_w_down, mix_norm=mix_norm, w_in=w_in, conv_w=conv_w, conv_b=conv_b, rg_w_a=rg_w_a, rg_b_a=rg_b_a, rg_w_x=rg_w_x, rg_b_x=rg_b_x, rg_lambda=rg_lambda, mla_q_norm=mla_q_norm, mla_w_uq=mla_w_uq, mla_kv_norm=mla_kv_norm, mla_w_ukv=mla_w_ukv, w_branch_a=w_branch_a, w_branch_b=w_branch_b, w_branch_c=w_branch_c, w_out=w_out, ffn2_norm=ffn2_norm, ffn2_w_gate_up=ffn2_w_gate_up, ffn2_w_down=ffn2_w_down, final_norm=final_norm, loss_target=loss_target, m_ffn1_norm=m_ffn1_norm, m_ffn1_w_gate_up=m_ffn1_w_gate_up, m_ffn1_w_down=m_ffn1_w_down, m_mix_norm=m_mix_norm, m_w_in=m_w_in, m_conv_w=m_conv_w, m_conv_b=m_conv_b, m_rg_w_a=m_rg_w_a, m_rg_b_a=m_rg_b_a, m_rg_w_x=m_rg_w_x, m_rg_b_x=m_rg_b_x, m_rg_lambda=m_rg_lambda, m_mla_q_norm=m_mla_q_norm, m_mla_w_uq=m_mla_w_uq, m_mla_kv_norm=m_mla_kv_norm, m_mla_w_ukv=m_mla_w_ukv, m_w_branch_a=m_w_branch_a, m_w_branch_b=m_w_branch_b, m_w_branch_c=m_w_branch_c, m_w_out=m_w_out, m_ffn2_norm=m_ffn2_norm, m_ffn2_w_gate_up=m_ffn2_w_gate_up, m_ffn2_w_down=m_ffn2_w_down, m_final_norm=m_final_norm, v_ffn1_norm=v_ffn1_norm, v_ffn1_w_gate_up=v_ffn1_w_gate_up, v_ffn1_w_down=v_ffn1_w_down, v_mix_norm=v_mix_norm, v_w_in=v_w_in, v_conv_w=v_conv_w, v_conv_b=v_conv_b, v_rg_w_a=v_rg_w_a, v_rg_b_a=v_rg_b_a, v_rg_w_x=v_rg_w_x, v_rg_b_x=v_rg_b_x, v_rg_lambda=v_rg_lambda, v_mla_q_norm=v_mla_q_norm, v_mla_w_uq=v_mla_w_uq, v_mla_kv_norm=v_mla_kv_norm, v_mla_w_ukv=v_mla_w_ukv, v_w_branch_a=v_w_branch_a, v_w_branch_b=v_w_branch_b, v_w_branch_c=v_w_branch_c, v_w_out=v_w_out, v_ffn2_norm=v_ffn2_norm, v_ffn2_w_gate_up=v_ffn2_w_gate_up, v_ffn2_w_down=v_ffn2_w_down, v_final_norm=v_final_norm)
    weights = {n: given[n] for n in TWIN_WEIGHTS}
    shared = {n: given[n] for n in SHARED_INPUTS}
    per_example = {n: given[n] for n in ['x', 'positions']}
    grad_fn = _jax.value_and_grad(_loss, argnums=(0, 1))

    def one_microbatch(ex, loss_target):
        ex = dict(ex)
        diff = ex.pop(TWIN_DIFF_INPUT)
        return grad_fn(weights, diff, {**shared, **ex}, loss_target)

    if N_MICROBATCH == 1:
        loss, (grad_w, grad_x) = one_microbatch(per_example, given["loss_target"])
    else:
        def body(carry, xs):
            loss_sum, grad_sum = carry
            l_k, (gw_k, gx_k) = one_microbatch(xs[0], xs[1])
            with _jax.named_scope("update"):
                return (loss_sum + l_k, _jax.tree.map(_jnp.add, grad_sum, gw_k)), gx_k

        init = (_jnp.zeros((), _jnp.float32), _jax.tree.map(_jnp.zeros_like, weights))
        (loss, grad_w), grad_x = _jax.lax.scan(body, init, (per_example, given["loss_target"]))
    with _jax.named_scope("update"):
        delta_w, new_m, new_v = {}, {}, {}
        for n in TWIN_WEIGHTS:
            delta_w[n], new_m[n], new_v[n] = _adamw(weights[n], grad_w[n], given["m_" + n], given["v_" + n])
    return (loss, grad_x, *[grad_w[n] for n in TWIN_WEIGHTS], *[delta_w[n] for n in TWIN_WEIGHTS],
            *[new_m[n] for n in TWIN_WEIGHTS], *[new_v[n] for n in TWIN_WEIGHTS])
```

```python
import functools
import math

import jax
import jax.numpy as jnp
from jax import lax
from jax.experimental import pallas as pl
from jax.experimental.pallas import tpu as pltpu

F32 = jnp.float32
BF16 = jnp.bfloat16

N_DEV = 8
D_MODEL = 1024
D_FF = 2816
NORM_EPS = 1e-6
RG_BLOCKS = 16
RG_BLOCK_W = 64
RG_C = 8.0
SB_HEADS = 8
HEAD = 128
MLA_HEADS = 8
MLA_LORA = 256
MLA_ROPE = 64
ROPE_THETA = 10000.0
CHUNK = 64
SB_SCALE = HEAD ** -0.5
MLA_SCALE = (HEAD + MLA_ROPE) ** -0.5
N_IN = 8768

Z_RGX, Z_RGG, Z_Q, Z_K, Z_V, Z_CQ, Z_CKV, Z_GATE, Z_KR, Z_W = 0, 1024, 2048, 3072, 4096, 5120, 5376, 5632, 8704, 8960

ADAM_LR, ADAM_B1, ADAM_B2, ADAM_EPS, ADAM_WD, ADAM_STEP = 0.001, 0.9, 0.999, 1e-08, 0.01, 10

LANE = 128
SUBLANE = 8
VMEM_LIMIT = 48 * 1024 * 1024
NEG = -1e30


def _pcall(body, **kw):
    return pl.pallas_call(body, **kw)


def _params(*sem):
    return pltpu.CompilerParams(dimension_semantics=sem or None, vmem_limit_bytes=VMEM_LIMIT)


def _pick(dim, target):
    best = None
    t = LANE
    while t <= min(dim, target):
        if dim % t == 0:
            best = t
        t += LANE
    return best if best is not None else dim


def _sigmoid(x):
    return 1.0 / (1.0 + jnp.exp(-x))


def _gelu_and_grad(x):
    c = math.sqrt(2.0 / math.pi)
    inner = c * (x + 0.044715 * x * x * x)
    t = jnp.tanh(inner)
    val = 0.5 * x * (1.0 + t)
    grad = 0.5 * (1.0 + t) + 0.5 * x * (1.0 - t * t) * c * (1.0 + 3.0 * 0.044715 * x * x)
    return val, grad


def _neg_expm1(y):
    series = -y * (1.0 + y * (0.5 + y * (1.0 / 6.0 + y * (1.0 / 24.0))))
    return jnp.where(jnp.abs(y) < 0.02, series, 1.0 - jnp.exp(y))


def _dot(a, b, dims):
    return lax.dot_general(a, b, (dims, ((), ())), preferred_element_type=F32)


NN = ((1,), (0,))
NT = ((1,), (1,))
TN = ((0,), (0,))


def matmul(a, b, mode, out_dtype, name, scale=1.0, res=None, tm=1024, tn=1024, tk=1024):
    if mode == "nn":
        (M, K), N = a.shape, b.shape[1]
    elif mode == "nt":
        (M, K), N = a.shape, b.shape[0]
    else:
        (K, M), N = a.shape, b.shape[1]
    tm, tn, tk = _pick(M, tm), _pick(N, tn), _pick(K, tk)
    nk = K // tk
    dims = {"nn": NN, "nt": NT, "tn": TN}[mode]

    def body(*refs):
        if res is None:
            a_ref, b_ref, o_ref, acc = refs
        else:
            a_ref, b_ref, r_ref, o_ref, acc = refs
        k = pl.program_id(2)

        @pl.when(k == 0)
        def _():
            acc[...] = jnp.zeros_like(acc)

        acc[...] += _dot(a_ref[...].astype(BF16), b_ref[...].astype(BF16), dims)

        @pl.when(k == nk - 1)
        def _():
            r = acc[...] * scale
            if res is not None:
                r = r + r_ref[...]
            o_ref[...] = r.astype(out_dtype)

    a_spec = pl.BlockSpec((tk, tm), lambda i, j, k: (k, i)) if mode == "tn" else pl.BlockSpec((tm, tk), lambda i, j, k: (i, k))
    b_spec = pl.BlockSpec((tn, tk), lambda i, j, k: (j, k)) if mode == "nt" else pl.BlockSpec((tk, tn), lambda i, j, k: (k, j))
    o_spec = pl.BlockSpec((tm, tn), lambda i, j, k: (i, j))
    in_specs = [a_spec, b_spec] + ([o_spec] if res is not None else [])
    args = (a, b) + ((res,) if res is not None else ())
    return _pcall(
        body, name=name, grid=(M // tm, N // tn, nk), in_specs=in_specs, out_specs=o_spec,
        out_shape=jax.ShapeDtypeStruct((M, N), out_dtype), scratch_shapes=[pltpu.VMEM((tm, tn), F32)],
        compiler_params=_params("parallel", "parallel", "arbitrary"),
    )(*args)


def rms_fwd(x, g, name, col=0):
    S, D = x.shape[0], g.shape[1]
    tr = _pick(S, 512)

    def body(x_ref, g_ref, o_ref):
        xv = x_ref[...].astype(F32)
        r = lax.rsqrt(jnp.mean(xv * xv, axis=-1, keepdims=True) + NORM_EPS)
        o_ref[...] = (xv * r * g_ref[...]).astype(BF16)

    return _pcall(
        body, name=name, grid=(S // tr,),
        in_specs=[pl.BlockSpec((tr, D), lambda i: (i, col)), pl.BlockSpec((1, D), lambda i: (0, 0))],
        out_specs=pl.BlockSpec((tr, D), lambda i: (i, 0)),
        out_shape=jax.ShapeDtypeStruct((S, D), BF16), compiler_params=_params("parallel"),
    )(x, g)


def _rms_bwd_math(xv, g, dh):
    r = lax.rsqrt(jnp.mean(xv * xv, axis=-1, keepdims=True) + NORM_EPS)
    xhat = xv * r
    dxhat = dh * g
    dx = r * (dxhat - xhat * jnp.mean(dxhat * xhat, axis=-1, keepdims=True))
    dg = jnp.sum(dh * xhat, axis=0, keepdims=True)
    return dx, dg


def rms_bwd(x, g, dh, dres, name):
    S, D = x.shape
    tr = _pick(S, 512)

    def body(x_ref, g_ref, dh_ref, dr_ref, dx_ref, dg_ref):
        dx, dg = _rms_bwd_math(x_ref[...], g_ref[...], dh_ref[...])
        dx_ref[...] = dx + dr_ref[...]

        @pl.when(pl.program_id(0) == 0)
        def _():
            dg_ref[...] = jnp.zeros_like(dg_ref)

        dg_ref[...] += dg

    row = pl.BlockSpec((tr, D), lambda i: (i, 0))
    vec = pl.BlockSpec((1, D), lambda i: (0, 0))
    return _pcall(
        body, name=name, grid=(S // tr,), in_specs=[row, vec, row, row], out_specs=[row, vec],
        out_shape=[jax.ShapeDtypeStruct((S, D), F32), jax.ShapeDtypeStruct((1, D), F32)],
        compiler_params=_params("arbitrary"),
    )(x, g, dh, dres)


def swiglu_fwd(gu, name):
    S = gu.shape[0]
    tr, tc = _pick(S, 512), 256
    nc = D_FF // tc

    def body(g_ref, u_ref, o_ref):
        gv = g_ref[...].astype(F32)
        o_ref[...] = (gv * _sigmoid(gv) * u_ref[...].astype(F32)).astype(BF16)

    return _pcall(
        body, name=name, grid=(S // tr, nc),
        in_specs=[pl.BlockSpec((tr, tc), lambda i, j: (i, j)), pl.BlockSpec((tr, tc), lambda i, j: (i, j + nc))],
        out_specs=pl.BlockSpec((tr, tc), lambda i, j: (i, j)),
        out_shape=jax.ShapeDtypeStruct((S, D_FF), BF16), compiler_params=_params("parallel", "parallel"),
    )(gu, gu)


def swiglu_bwd(gu, da, name):
    S = gu.shape[0]
    tr, tc = _pick(S, 512), 256
    nc = D_FF // tc

    def body(g_ref, u_ref, da_ref, o_ref):
        gv = g_ref[...].astype(F32)
        uv = u_ref[...].astype(F32)
        dav = da_ref[...].astype(F32)
        sg = _sigmoid(gv)
        half = pl.program_id(1)

        @pl.when(half == 0)
        def _():
            o_ref[...] = (dav * uv * sg * (1.0 + gv * (1.0 - sg))).astype(BF16)

        @pl.when(half == 1)
        def _():
            o_ref[...] = (dav * gv * sg).astype(BF16)

    return _pcall(
        body, name=name, grid=(S // tr, 2, nc),
        in_specs=[pl.BlockSpec((tr, tc), lambda i, h, j: (i, j)), pl.BlockSpec((tr, tc), lambda i, h, j: (i, j + nc)),
                  pl.BlockSpec((tr, tc), lambda i, h, j: (i, j))],
        out_specs=pl.BlockSpec((tr, tc), lambda i, h, j: (i, h * nc + j)),
        out_shape=jax.ShapeDtypeStruct((S, 2 * D_FF), BF16), compiler_params=_params("parallel", "parallel", "parallel"),
    )(gu, gu, da)


def _conv_taps(xpad, T, cw, cb):
    u = cb + cw[3:4, :] * xpad[pl.ds(8, T), :]
    for tap in range(3):
        u = u + cw[tap:tap + 1, :] * xpad[pl.ds(5 + tap, T), :]
    return u


def _rg_gates(u, wa_ref, wx_ref, ba, bx, lam):
    ub = u.astype(BF16)
    r = _sigmoid(_dot(ub, wa_ref[...], NN) + ba)
    ig = _sigmoid(_dot(ub, wx_ref[...], NN) + bx)
    nlam = -lam
    clam = -RG_C * (jnp.maximum(nlam, 0.0) + jnp.log(1.0 + jnp.exp(-jnp.abs(nlam))))
    la = clam * r
    return r, ig, clam, la


def rglru_fwd(z, cw, cb, wa, wx, ba, bx, lam, name):
    S, D = z.shape[0], D_MODEL
    T = _pick(S, 256)

    def body(x_ref, g_ref, cw_ref, cb_ref, wa_ref, wx_ref, ba_ref, bx_ref, lam_ref, y_ref, h_ref, xpad, a_s, b_s, hst):
        @pl.when(pl.program_id(0) == 0)
        def _():
            xpad[pl.ds(0, 8), :] = jnp.zeros((8, D), F32)
            hst[...] = jnp.zeros_like(hst)

        xpad[pl.ds(8, T), :] = x_ref[...].astype(F32)
        u = _conv_taps(xpad, T, cw_ref[...], cb_ref[...])
        xpad[pl.ds(0, 8), :] = xpad[pl.ds(T, 8), :]
        r, ig, clam, la = _rg_gates(u, wa_ref, wx_ref, ba_ref[...], bx_ref[...], lam_ref[...])
        a_s[...] = jnp.exp(la)
        b_s[...] = jnp.sqrt(_neg_expm1(2.0 * la)) * (ig * u)

        def tile(j, h):
            r0 = pl.multiple_of(j * 8, 8)
            av = a_s[pl.ds(r0, 8), :]
            bv = b_s[pl.ds(r0, 8), :]
            rows = []
            for k in range(8):
                h = av[k:k + 1, :] * h + bv[k:k + 1, :]
                rows.append(h)
            h_ref[pl.ds(r0, 8), :] = jnp.concatenate(rows, axis=0)
            return h

        hst[...] = lax.fori_loop(0, T // 8, tile, hst[...])
        gel, _ = _gelu_and_grad(g_ref[...].astype(F32))
        y_ref[...] = (h_ref[...] * gel).astype(BF16)

    blk = lambda c: pl.BlockSpec((T, D), lambda i: (i, c))
    vec = pl.BlockSpec((1, D), lambda i: (0, 0))
    full = lambda r: pl.BlockSpec((r, D), lambda i: (0, 0))
    return _pcall(
        body, name=name, grid=(S // T,),
        in_specs=[blk(0), blk(1), full(4), vec, full(D), full(D), vec, vec, vec],
        out_specs=[blk(0), blk(0)],
        out_shape=[jax.ShapeDtypeStruct((S, D), BF16), jax.ShapeDtypeStruct((S, D), F32)],
        scratch_shapes=[pltpu.VMEM((T + 8, D), F32), pltpu.VMEM((T, D), F32), pltpu.VMEM((T, D), F32), pltpu.VMEM((1, D), F32)],
        compiler_params=_params("arbitrary"),
    )(z, z, cw, cb, wa, wx, ba, bx, lam)


def rglru_bwd(z, hs, dy, cw, cb, wa, wx, ba, bx, lam, name):
    S, D = z.shape[0], D_MODEL
    T = _pick(S, 256)
    nb = S // T
    t8 = T // 8

    def body(x_ref, xp_ref, g_ref, h_ref, hp_ref, dy_ref, cw_ref, cb_ref, wa_ref, wx_ref, ba_ref, bx_ref, lam_ref,
             dx_ref, dg_ref, dwa_ref, dwx_ref, dvec_ref, xpad, hpad, dupad, a_s, d_s, carry):
        i = pl.program_id(0)
        first_block = i == nb - 1

        @pl.when(i == 0)
        def _():
            dwa_ref[...] = jnp.zeros_like(dwa_ref)
            dwx_ref[...] = jnp.zeros_like(dwx_ref)
            dvec_ref[...] = jnp.zeros_like(dvec_ref)
            carry[...] = jnp.zeros_like(carry)
            dupad[pl.ds(T, 8), :] = jnp.zeros((8, D), F32)

        keep = jnp.where(first_block, 0.0, 1.0)
        xpad[pl.ds(0, 8), :] = xp_ref[...].astype(F32) * keep
        xpad[pl.ds(8, T), :] = x_ref[...].astype(F32)
        hpad[pl.ds(0, 8), :] = hp_ref[...] * keep
        hpad[pl.ds(8, T), :] = h_ref[...]
        cwv = cw_ref[...]
        u = _conv_taps(xpad, T, cwv, cb_ref[...])
        r, ig, clam, la = _rg_gates(u, wa_ref, wx_ref, ba_ref[...], bx_ref[...], lam_ref[...])
        a = jnp.exp(la)
        a_s[...] = a
        gv = g_ref[...].astype(F32)
        gel, dgel = _gelu_and_grad(gv)
        dyv = dy_ref[...].astype(F32)
        d_s[...] = dyv * gel
        dg_ref[...] = (dyv * h_ref[...] * dgel).astype(BF16)

        def tile(j, c):
            r0 = pl.multiple_of((t8 - 1 - j) * 8, 8)
            av = a_s[pl.ds(r0, 8), :]
            dv = d_s[pl.ds(r0, 8), :]
            rows = [None] * 8
            for k in range(7, -1, -1):
                d = dv[k:k + 1, :] + c
                rows[k] = d
                c = av[k:k + 1, :] * d
            d_s[pl.ds(r0, 8), :] = jnp.concatenate(rows, axis=0)
            return c

        carry[...] = lax.fori_loop(0, t8, tile, carry[...])
        dht = d_s[...]
        hprev = hpad[pl.ds(7, T), :]
        w = _neg_expm1(2.0 * la)
        s = jnp.sqrt(w)
        e2 = 1.0 - w
        d_iu = dht * s
        dla = dht * hprev * a - dht * (ig * u) * e2 / s
        dpr = (dla * clam * r * (1.0 - r))
        dpi = (d_iu * u * ig * (1.0 - ig))
        dprb, dpib, ub = dpr.astype(BF16), dpi.astype(BF16), u.astype(BF16)
        du = d_iu * ig + _dot(dprb, wa_ref[...], NT) + _dot(dpib, wx_ref[...], NT)
        dwa_ref[...] += _dot(ub, dprb, TN)
        dwx_ref[...] += _dot(ub, dpib, TN)
        dvec_ref[0:1, :] += jnp.sum(dpr, axis=0, keepdims=True)
        dvec_ref[1:2, :] += jnp.sum(dpi, axis=0, keepdims=True)
        dvec_ref[2:3, :] += jnp.sum(dla * r, axis=0, keepdims=True)
        dvec_ref[3:4, :] += jnp.sum(du, axis=0, keepdims=True)
        for tap in range(4):
            dvec_ref[4 + tap:5 + tap, :] += jnp.sum(du * xpad[pl.ds(5 + tap, T), :], axis=0, keepdims=True)
        dupad[pl.ds(0, T), :] = du
        dx = cwv[3:4, :] * du
        for tap in range(3):
            dx = dx + cwv[tap:tap + 1, :] * dupad[pl.ds(3 - tap, T), :]
        dx_ref[...] = dx.astype(BF16)
        dupad[pl.ds(T, 8), :] = dupad[pl.ds(0, 8), :]

        @pl.when(first_block)
        def _():
            dvec_ref[2:3, :] = dvec_ref[2:3, :] * (RG_C * _sigmoid(-lam_ref[...]))

    rev = lambda c: pl.BlockSpec((T, D), lambda i: (nb - 1 - i, c))
    prev = lambda c: pl.BlockSpec((8, D), lambda i: (jnp.maximum((nb - 1 - i) * t8 - 1, 0), c))
    vec = pl.BlockSpec((1, D), lambda i: (0, 0))
    full = lambda r: pl.BlockSpec((r, D), lambda i: (0, 0))
    return _pcall(
        body, name=name, grid=(nb,),
        in_specs=[rev(0), prev(0), rev(1), rev(0), prev(0), rev(0), full(4), vec, full(D), full(D), vec, vec, vec],
        out_specs=[rev(0), rev(0), full(D), full(D), full(8)],
        out_shape=[jax.ShapeDtypeStruct((S, D), BF16), jax.ShapeDtypeStruct((S, D), BF16),
                   jax.ShapeDtypeStruct((D, D), F32), jax.ShapeDtypeStruct((D, D), F32), jax.ShapeDtypeStruct((8, D), F32)],
        scratch_shapes=[pltpu.VMEM((T + 8, D), F32), pltpu.VMEM((T + 8, D), F32), pltpu.VMEM((T + 8, D), F32),
                        pltpu.VMEM((T, D), F32), pltpu.VMEM((T, D), F32), pltpu.VMEM((1, D), F32)],
        compiler_params=_params("arbitrary"),
    )(z, z, z, hs, hs, dy, cw, cb, wa, wx, ba, bx, lam)


def _tri(n, kind):
    j = lax.broadcasted_iota(jnp.int32, (n, n), 0)
    s = lax.broadcasted_iota(jnp.int32, (n, n), 1)
    m = {"gt": j > s, "le": j <= s, "lt": j < s}[kind]
    return jnp.where(m, 1.0, 0.0).astype(BF16)


def _dot2(x, tri):
    hi = x.astype(BF16)
    lo = (x - hi.astype(F32)).astype(BF16)
    return _dot(hi, tri, NN) + _dot(lo, tri, NN)


def _sb_scores(q, kblk, q0, k0, tq, tk):
    z = _dot(q, kblk, NT) * SB_SCALE
    tpos = q0 + lax.broadcasted_iota(jnp.int32, (tq, tk), 0)
    spos = k0 + lax.broadcasted_iota(jnp.int32, (tq, tk), 1)
    mask = spos < tpos
    t = jnp.exp(-jnp.abs(z))
    lg = jnp.log(1.0 + t)
    lkeep = jnp.where(mask, -(jnp.maximum(z, 0.0) + lg), 0.0)
    lbeta = -(jnp.maximum(-z, 0.0) + lg)
    return z, mask, t, lkeep, lbeta


def sb_fwd(z, name):
    S = z.shape[0]
    tq, tk = _pick(S, 256), 128
    qc, kc, vc = Z_Q // HEAD, Z_K // HEAD, Z_V // HEAD

    def body(q_ref, k_ref, v_ref, o_ref, lt_ref):
        q0 = pl.program_id(1) * tq
        q = q_ref[...]
        nkb = (q0 + tq) // tk
        tri = _tri(tk, "gt")

        def step(i, c):
            acc, run = c
            k0 = pl.multiple_of((nkb - 1 - i) * tk, tk)
            kblk = k_ref[pl.ds(k0, tk), :]
            _, mask, _, lkeep, lbeta = _sb_scores(q, kblk, q0, k0, tq, tk)
            w = jnp.where(mask, jnp.exp(lbeta + _dot2(lkeep, tri) + run), 0.0)
            acc = acc + _dot(w.astype(BF16), v_ref[pl.ds(k0, tk), :], NN)
            return acc, run + jnp.sum(lkeep, axis=1, keepdims=True)

        acc, run = lax.fori_loop(0, nkb, step, (jnp.zeros((tq, HEAD), F32), jnp.zeros((tq, 1), F32)))
        o_ref[...] = acc.astype(BF16)
        lt_ref[0] = run

    return _pcall(
        body, name=name, grid=(SB_HEADS, S // tq),
        in_specs=[pl.BlockSpec((tq, HEAD), lambda h, i: (i, qc + h)), pl.BlockSpec((S, HEAD), lambda h, i: (0, kc + h)),
                  pl.BlockSpec((S, HEAD), lambda h, i: (0, vc + h))],
        out_specs=[pl.BlockSpec((tq, HEAD), lambda h, i: (i, h)), pl.BlockSpec((1, tq, 1), lambda h, i: (h, i, 0))],
        out_shape=[jax.ShapeDtypeStruct((S, SB_HEADS * HEAD), BF16), jax.ShapeDtypeStruct((SB_HEADS, S, 1), F32)],
        compiler_params=_params("parallel", "parallel"),
    )(z, z, z)


def sb_bwd(z, ltot, dy, name):
    S = z.shape[0]
    tq, tk = _pick(S, 256), 128
    qc, kc, vc = Z_Q // HEAD, Z_K // HEAD, Z_V // HEAD

    def body(q_ref, k_ref, v_ref, lt_ref, do_ref, dq_ref, dk_ref, dv_ref):
        qi = pl.program_id(1)
        q0 = qi * tq

        @pl.when(qi == 0)
        def _():
            dk_ref[...] = jnp.zeros_like(dk_ref)
            dv_ref[...] = jnp.zeros_like(dv_ref)

        q = q_ref[...]
        do = do_ref[...].astype(BF16)
        ltv = lt_ref[0]
        nkb = (q0 + tq) // tk
        tri_le, tri_lt = _tri(tk, "le"), _tri(tk, "lt")

        def step(kb, c):
            dq, run_l, run_g = c
            k0 = pl.multiple_of(kb * tk, tk)
            kblk = k_ref[pl.ds(k0, tk), :]
            vblk = v_ref[pl.ds(k0, tk), :]
            zz, mask, t, lkeep, lbeta = _sb_scores(q, kblk, q0, k0, tq, tk)
            between = ltv - (_dot2(lkeep, tri_le) + run_l)
            w = jnp.where(mask, jnp.exp(lbeta + between), 0.0)
            g = w * _dot(do, vblk, NT)
            gpre = _dot2(g, tri_lt) + run_g
            inv = 1.0 / (1.0 + t)
            sig_p = jnp.where(zz >= 0, inv, t * inv)
            sig_n = jnp.where(zz >= 0, t * inv, inv)
            dz = (jnp.where(mask, g * sig_n - sig_p * gpre, 0.0) * SB_SCALE).astype(BF16)
            dq = dq + _dot(dz, kblk, NN)
            dk_ref[pl.ds(k0, tk), :] += _dot(dz, q, TN)
            dv_ref[pl.ds(k0, tk), :] += _dot(w.astype(BF16), do, TN)
            return dq, run_l + jnp.sum(lkeep, axis=1, keepdims=True), run_g + jnp.sum(g, axis=1, keepdims=True)

        zero = jnp.zeros((tq, 1), F32)
        dq, _, _ = lax.fori_loop(0, nkb, step, (jnp.zeros((tq, HEAD), F32), zero, zero))
        dq_ref[...] = dq

    qblk = lambda c: pl.BlockSpec((tq, HEAD), lambda h, i: (i, c + h))
    kfull = lambda c: pl.BlockSpec((S, HEAD), lambda h, i: (0, c + h))
    out = jax.ShapeDtypeStruct((S, SB_HEADS * HEAD), F32)
    return _pcall(
        body, name=name, grid=(SB_HEADS, S // tq),
        in_specs=[qblk(qc), kfull(kc), kfull(vc), pl.BlockSpec((1, tq, 1), lambda h, i: (h, i, 0)), qblk(0)],
        out_specs=[qblk(0), kfull(0), kfull(0)], out_shape=[out, out, out],
        compiler_params=_params("arbitrary", "arbitrary"),
    )(z, z, z, ltot, dy)


def _rope(x, cs, sn, sign):
    lane = lax.broadcasted_iota(jnp.int32, x.shape, 1)
    swapped = jnp.where(lane < MLA_ROPE // 2, -pltpu.roll(x, LANE - MLA_ROPE // 2, 1), pltpu.roll(x, MLA_ROPE // 2, 1))
    return x * cs + sign * swapped * sn


def mla_prep_fwd(z, qn, kvn, cs, sn, name):
    S = z.shape[0]
    tr = _pick(S, 512)

    def body(cq_ref, ckv_ref, kr_ref, qn_ref, kvn_ref, cs_ref, sn_ref, oq_ref, okv_ref, okr_ref):
        for src, g, dst in ((cq_ref, qn_ref, oq_ref), (ckv_ref, kvn_ref, okv_ref)):
            xv = src[...].astype(F32)
            r = lax.rsqrt(jnp.mean(xv * xv, axis=-1, keepdims=True) + NORM_EPS)
            dst[...] = (xv * r * g[...]).astype(BF16)
        okr_ref[...] = _rope(kr_ref[...].astype(F32), cs_ref[...], sn_ref[...], 1.0).astype(BF16)

    lora = lambda c: pl.BlockSpec((tr, MLA_LORA), lambda i: (i, c))
    tile = lambda c: pl.BlockSpec((tr, LANE), lambda i: (i, c))
    vec = pl.BlockSpec((1, MLA_LORA), lambda i: (0, 0))
    return _pcall(
        body, name=name, grid=(S // tr,),
        in_specs=[lora(Z_CQ // MLA_LORA), lora(Z_CKV // MLA_LORA), tile(Z_KR // LANE), vec, vec, tile(0), tile(0)],
        out_specs=[lora(0), lora(0), tile(0)],
        out_shape=[jax.ShapeDtypeStruct((S, MLA_LORA), BF16), jax.ShapeDtypeStruct((S, MLA_LORA), BF16),
                   jax.ShapeDtypeStruct((S, LANE), BF16)],
        compiler_params=_params("parallel"),
    )(z, z, z, qn, kvn, cs, sn)


def mla_prep_bwd(z, qn, kvn, cs, sn, dcqn, dckvn, dkrope, name):
    S = z.shape[0]
    tr = _pick(S, 512)

    def body(cq_ref, ckv_ref, qn_ref, kvn_ref, cs_ref, sn_ref, dq_ref, dkv_ref, dkr_ref, oq_ref, okv_ref, okr_ref, gq_ref, gkv_ref):
        @pl.when(pl.program_id(0) == 0)
        def _():
            gq_ref[...] = jnp.zeros_like(gq_ref)
            gkv_ref[...] = jnp.zeros_like(gkv_ref)

        for src, g, dh, dst, gacc in ((cq_ref, qn_ref, dq_ref, oq_ref, gq_ref), (ckv_ref, kvn_ref, dkv_ref, okv_ref, gkv_ref)):
            dx, dg = _rms_bwd_math(src[...].astype(F32), g[...], dh[...])
            dst[...] = dx.astype(BF16)
            gacc[...] += dg
        okr_ref[...] = _rope(dkr_ref[...], cs_ref[...], sn_ref[...], -1.0).astype(BF16)

    lora = lambda c: pl.BlockSpec((tr, MLA_LORA), lambda i: (i, c))
    tile = lambda c: pl.BlockSpec((tr, LANE), lambda i: (i, c))
    vec = pl.BlockSpec((1, MLA_LORA), lambda i: (0, 0))
    return _pcall(
        body, name=name, grid=(S // tr,),
        in_specs=[lora(Z_CQ // MLA_LORA), lora(Z_CKV // MLA_LORA), vec, vec, tile(0), tile(0), lora(0), lora(0), tile(0)],
        out_specs=[lora(0), lora(0), tile(0), vec, vec],
        out_shape=[jax.ShapeDtypeStruct((S, MLA_LORA), BF16), jax.ShapeDtypeStruct((S, MLA_LORA), BF16),
                   jax.ShapeDtypeStruct((S, LANE), BF16), jax.ShapeDtypeStruct((1, MLA_LORA), F32), jax.ShapeDtypeStruct((1, MLA_LORA), F32)],
        compiler_params=_params("arbitrary"),
    )(z, z, qn, kvn, cs, sn, dcqn, dckvn, dkrope)


def q_rope(q, cs, sn, sign, name):
    S = q.shape[0]
    tr = _pick(S, 512)

    def body(q_ref, cs_ref, sn_ref, o_ref):
        o_ref[:, 0:LANE] = q_ref[:, 0:LANE].astype(BF16)
        o_ref[:, LANE:2 * LANE] = _rope(q_ref[:, LANE:2 * LANE], cs_ref[...], sn_ref[...], sign).astype(BF16)

    blk = pl.BlockSpec((tr, 2 * LANE), lambda i, h: (i, h))
    tile = pl.BlockSpec((tr, LANE), lambda i, h: (i, 0))
    return _pcall(
        body, name=name, grid=(S // tr, MLA_HEADS), in_specs=[blk, tile, tile], out_specs=blk,
        out_shape=jax.ShapeDtypeStruct(q.shape, BF16), compiler_params=_params("parallel", "parallel"),
    )(q, cs, sn)


def _mla_scores(qn, qr, kv_ref, kr_ref, q0, k0, tq, tk):
    kn = kv_ref[pl.ds(k0, tk), 0:HEAD]
    vv = kv_ref[pl.ds(k0, tk), HEAD:2 * HEAD]
    kr = kr_ref[pl.ds(k0, tk), :]
    s = (_dot(qn, kn, NT) + _dot(qr, kr, NT)) * MLA_SCALE
    tch = (q0 + lax.broadcasted_iota(jnp.int32, (tq, tk), 0)) // CHUNK
    sch = (k0 + lax.broadcasted_iota(jnp.int32, (tq, tk), 1)) // CHUNK
    return s, sch <= tch, kn, vv, kr


def mla_fwd(q, kv, kr, name):
    S = q.shape[0]
    tq = tk = _pick(S, 256)

    def body(q_ref, kv_ref, kr_ref, o_ref, lse_ref):
        q0 = pl.program_id(1) * tq
        qn, qr = q_ref[:, 0:HEAD], q_ref[:, HEAD:2 * HEAD]

        def step(kb, c):
            m, l, acc = c
            k0 = pl.multiple_of(kb * tk, tk)
            s, mask, _, vv, _ = _mla_scores(qn, qr, kv_ref, kr_ref, q0, k0, tq, tk)
            s = jnp.where(mask, s, NEG)
            m2 = jnp.maximum(m, jnp.max(s, axis=1, keepdims=True))
            p = jnp.where(mask, jnp.exp(s - m2), 0.0)
            alpha = jnp.exp(m - m2)
            return m2, alpha * l + jnp.sum(p, axis=1, keepdims=True), alpha * acc + _dot(p.astype(BF16), vv, NN)

        m, l, acc = lax.fori_loop(0, (q0 + tq) // tk, step,
                                  (jnp.full((tq, 1), NEG, F32), jnp.zeros((tq, 1), F32), jnp.zeros((tq, HEAD), F32)))
        o_ref[...] = (acc / l).astype(BF16)
        lse_ref[0] = m + jnp.log(l)

    return _pcall(
        body, name=name, grid=(MLA_HEADS, S // tq),
        in_specs=[pl.BlockSpec((tq, 2 * HEAD), lambda h, i: (i, h)), pl.BlockSpec((S, 2 * HEAD), lambda h, i: (0, h)),
                  pl.BlockSpec((S, LANE), lambda h, i: (0, 0))],
        out_specs=[pl.BlockSpec((tq, HEAD), lambda h, i: (i, h)), pl.BlockSpec((1, tq, 1), lambda h, i: (h, i, 0))],
        out_shape=[jax.ShapeDtypeStruct((S, MLA_HEADS * HEAD), BF16), jax.ShapeDtypeStruct((MLA_HEADS, S, 1), F32)],
        compiler_params=_params("parallel", "parallel"),
    )(q, kv, kr)


def mla_bwd(q, kv, kr, o, lse, do, name):
    S = q.shape[0]
    tq = tk = _pick(S, 256)

    def body(q_ref, kv_ref, kr_ref, o_ref, lse_ref, do_ref, dq_ref, dkv_ref, dkr_ref):
        h, qi = pl.program_id(0), pl.program_id(1)
        q0 = qi * tq

        @pl.when(qi == 0)
        def _():
            dkv_ref[...] = jnp.zeros_like(dkv_ref)

        @pl.when((qi == 0) & (h == 0))
        def _():
            dkr_ref[...] = jnp.zeros_like(dkr_ref)

        qn, qr = q_ref[:, 0:HEAD], q_ref[:, HEAD:2 * HEAD]
        dov = do_ref[...].astype(F32)
        dob = dov.astype(BF16)
        delta = jnp.sum(dov * o_ref[...].astype(F32), axis=1, keepdims=True)
        lsev = lse_ref[0]

        def step(kb, c):
            dqn, dqr = c
            k0 = pl.multiple_of(kb * tk, tk)
            s, mask, kn, vv, krb = _mla_scores(qn, qr, kv_ref, kr_ref, q0, k0, tq, tk)
            p = jnp.where(mask, jnp.exp(jnp.where(mask, s, NEG) - lsev), 0.0)
            ds = (p * (_dot(dob, vv, NT) - delta) * MLA_SCALE).astype(BF16)
            dkv_ref[pl.ds(k0, tk), 0:HEAD] += _dot(ds, qn, TN)
            dkv_ref[pl.ds(k0, tk), HEAD:2 * HEAD] += _dot(p.astype(BF16), dob, TN)
            dkr_ref[pl.ds(k0, tk), :] += _dot(ds, qr, TN)
            return dqn + _dot(ds, kn, NN), dqr + _dot(ds, krb, NN)

        dqn, dqr = lax.fori_loop(0, (q0 + tq) // tk, step, (jnp.zeros((tq, HEAD), F32), jnp.zeros((tq, LANE), F32)))
        dq_ref[:, 0:HEAD] = dqn
        dq_ref[:, HEAD:2 * HEAD] = dqr

    qblk = pl.BlockSpec((tq, 2 * HEAD), lambda h, i: (i, h))
    kvfull = pl.BlockSpec((S, 2 * HEAD), lambda h, i: (0, h))
    krfull = pl.BlockSpec((S, LANE), lambda h, i: (0, 0))
    oblk = pl.BlockSpec((tq, HEAD), lambda h, i: (i, h))
    return _pcall(
        body, name=name, grid=(MLA_HEADS, S // tq),
        in_specs=[qblk, kvfull, krfull, oblk, pl.BlockSpec((1, tq, 1), lambda h, i: (h, i, 0)), oblk],
        out_specs=[qblk, kvfull, krfull],
        out_shape=[jax.ShapeDtypeStruct(q.shape, F32), jax.ShapeDtypeStruct(kv.shape, F32), jax.ShapeDtypeStruct((S, LANE), F32)],
        compiler_params=_params("arbitrary", "arbitrary"),
    )(q, kv, kr, o, lse, do)


GATE_TC = 512


def merge_fwd(z, ya, yb, yc, name):
    S = z.shape[0]
    tr, tc = _pick(S, 512), GATE_TC
    g0 = Z_GATE // tc
    nc = D_MODEL // tc

    def body(ga_ref, gb_ref, gc_ref, ya_ref, yb_ref, yc_ref, o_ref):
        acc = None
        for g, y in ((ga_ref, ya_ref), (gb_ref, yb_ref), (gc_ref, yc_ref)):
            term = _sigmoid(g[...].astype(F32)) * y[...].astype(F32)
            acc = term if acc is None else acc + term
        o_ref[...] = acc.astype(BF16)

    gate = lambda b: pl.BlockSpec((tr, tc), lambda i, j: (i, g0 + b * nc + j))
    blk = pl.BlockSpec((tr, tc), lambda i, j: (i, j))
    return _pcall(
        body, name=name, grid=(S // tr, nc), in_specs=[gate(0), gate(1), gate(2), blk, blk, blk], out_specs=blk,
        out_shape=jax.ShapeDtypeStruct((S, D_MODEL), BF16), compiler_params=_params("parallel", "parallel"),
    )(z, z, z, ya, yb, yc)


def merge_bwd(z, ya, yb, yc, dm, name):
    S = z.shape[0]
    tr, tc = _pick(S, 512), GATE_TC
    g0 = Z_GATE // tc
    nc = D_MODEL // tc

    def body(ga_ref, gb_ref, gc_ref, ya_ref, yb_ref, yc_ref, dm_ref, da_ref, db_ref, dc_ref, dga_ref, dgb_ref, dgc_ref):
        dmv = dm_ref[...].astype(F32)
        for g, y, dy, dg in ((ga_ref, ya_ref, da_ref, dga_ref), (gb_ref, yb_ref, db_ref, dgb_ref), (gc_ref, yc_ref, dc_ref, dgc_ref)):
            sg = _sigmoid(g[...].astype(F32))
            dy[...] = (dmv * sg).astype(BF16)
            dg[...] = (dmv * y[...].astype(F32) * sg * (1.0 - sg)).astype(BF16)

    gate = lambda b: pl.BlockSpec((tr, tc), lambda i, j: (i, g0 + b * nc + j))
    blk = pl.BlockSpec((tr, tc), lambda i, j: (i, j))
    out = jax.ShapeDtypeStruct((S, D_MODEL), BF16)
    return _pcall(
        body, name=name, grid=(S // tr, nc), in_specs=[gate(0), gate(1), gate(2), blk, blk, blk, blk],
        out_specs=[blk] * 6, out_shape=[out] * 6, compiler_params=_params("parallel", "parallel"),
    )(z, z, z, ya, yb, yc, dm)


def loss_head(x, g, target, name):
    S, D = x.shape
    tr = _pick(S, 512)

    def body(x_ref, g_ref, t_ref, l_ref, dx_ref, dg_ref):
        @pl.when(pl.program_id(0) == 0)
        def _():
            l_ref[...] = jnp.zeros_like(l_ref)
            dg_ref[...] = jnp.zeros_like(dg_ref)

        xv, gv = x_ref[...], g_ref[...]
        r = lax.rsqrt(jnp.mean(xv * xv, axis=-1, keepdims=True) + NORM_EPS)
        diff = xv * r * gv - t_ref[...]
        l_ref[...] += 0.5 * jnp.sum(jnp.mean(diff * diff, axis=-1, keepdims=True), axis=0, keepdims=True)
        dx, dg = _rms_bwd_math(xv, gv, diff * (1.0 / D))
        dx_ref[...] = dx
        dg_ref[...] += dg

    row = pl.BlockSpec((tr, D), lambda i: (i, 0))
    vec = pl.BlockSpec((1, D), lambda i: (0, 0))
    return _pcall(
        body, name=name, grid=(S // tr,), in_specs=[row, vec, row],
        out_specs=[pl.BlockSpec((1, LANE), lambda i: (0, 0)), row, vec],
        out_shape=[jax.ShapeDtypeStruct((1, LANE), F32), jax.ShapeDtypeStruct((S, D), F32), jax.ShapeDtypeStruct((1, D), F32)],
        compiler_params=_params("arbitrary"),
    )(x, g, target)


def _peer(k, x, y, c):
    px = 1 - x if k & 4 else x
    py = 1 - y if k & 2 else y
    pc = 1 - c if k & 1 else c
    return (px, py, pc), 4 * px + 2 * py + pc


def exchange(arrs, gather, name):
    n = len(arrs)
    shapes = [((N_DEV,) + a.shape) if gather else a.shape for a in arrs]

    def body(*refs):
        ins, outs = refs[:n], refs[n:2 * n]
        send_sems, recv_sems, loc_sems = refs[2 * n:]
        x, y, c = lax.axis_index("x"), lax.axis_index("y"), lax.axis_index("c")
        me = 4 * x + 2 * y + c
        sends, recvs, locs = [], [], []
        for a in range(n):
            loc = pltpu.make_async_copy(ins[a] if gather else ins[a].at[me], outs[a].at[me], loc_sems.at[a])
            loc.start()
            locs.append(loc)
            for k in range(1, N_DEV):
                peer, pid = _peer(k, x, y, c)
                s = a * (N_DEV - 1) + k - 1
                src = ins[a] if gather else ins[a].at[pid]
                snd = pltpu.make_async_remote_copy(src_ref=src, dst_ref=outs[a].at[me], send_sem=send_sems.at[s],
                                                   recv_sem=recv_sems.at[s], device_id=peer, device_id_type=pl.DeviceIdType.MESH)
                snd.start()
                sends.append(snd)
                recvs.append(pltpu.make_async_remote_copy(src_ref=src, dst_ref=outs[a].at[pid], send_sem=send_sems.at[s],
                                                          recv_sem=recv_sems.at[s], device_id=peer, device_id_type=pl.DeviceIdType.MESH))
        for snd, rcv in zip(sends, recvs):
            snd.wait_send()
            rcv.wait_recv()
        for loc in locs:
            loc.wait()

    any_spec = pl.BlockSpec(memory_space=pl.ANY)
    outs = _pcall(
        body, name=name, in_specs=[any_spec] * n, out_specs=[any_spec] * n,
        out_shape=[jax.ShapeDtypeStruct(s, a.dtype) for s, a in zip(shapes, arrs)],
        scratch_shapes=[pltpu.SemaphoreType.DMA((n * (N_DEV - 1),)), pltpu.SemaphoreType.DMA((n * (N_DEV - 1),)),
                        pltpu.SemaphoreType.DMA((n,))],
        compiler_params=pltpu.CompilerParams(has_side_effects=True),
    )(*arrs)
    return list(outs)


def adamw_sum(parts, w, m, v, name):
    L, R, C = w.shape
    tr = R
    for cand in (512, 352, 256, 128, 64, 48, 32, 16, 8):
        if R % cand == 0 and cand * C * 4 <= 2 * 1024 * 1024:
            tr = cand
            break
    c1 = 1.0 - ADAM_B1 ** ADAM_STEP
    c2 = 1.0 - ADAM_B2 ** ADAM_STEP

    def body(p_ref, w_ref, m_ref, v_ref, g_ref, d_ref, nm_ref, nv_ref):
        g = p_ref[0, 0].astype(F32)
        for k in range(1, N_DEV):
            g = g + p_ref[k, 0].astype(F32)
        m2 = ADAM_B1 * m_ref[0] + (1.0 - ADAM_B1) * g
        v2 = ADAM_B2 * v_ref[0] + (1.0 - ADAM_B2) * (g * g)
        g_ref[0] = g
        nm_ref[0] = m2
        nv_ref[0] = v2
        d_ref[0] = -ADAM_LR * ((m2 / c1) / (jnp.sqrt(v2 / c2) + ADAM_EPS) + ADAM_WD * w_ref[0])

    blk = pl.BlockSpec((1, tr, C), lambda l, i: (l, i, 0))
    out = jax.ShapeDtypeStruct((L, R, C), F32)
    return _pcall(
        body, name=name, grid=(L, R // tr),
        in_specs=[pl.BlockSpec((N_DEV, 1, tr, C), lambda l, i: (0, l, i, 0)), blk, blk, blk],
        out_specs=[blk] * 4, out_shape=[out] * 4, compiler_params=_params("parallel", "parallel"),
    )(parts, w, m, v)


def _cols_full(g):
    return jnp.transpose(g, (1, 0, 2)).reshape(g.shape[1], N_DEV * g.shape[2])


def _cols_shards(w):
    R = w.shape[0]
    return jnp.transpose(w.reshape(R, N_DEV, w.shape[1] // N_DEV), (1, 0, 2))


def _w_in_to_z(w):
    kr0 = Z_GATE
    gate0 = Z_GATE + MLA_ROPE
    pad = jnp.zeros((w.shape[0], Z_W - Z_KR - MLA_ROPE), w.dtype)
    return jnp.concatenate([w[:, :kr0], w[:, gate0:], w[:, kr0:gate0], pad], axis=1)


def _z_to_w_in(dw):
    return jnp.concatenate([dw[:, :Z_GATE], dw[:, Z_KR:Z_KR + MLA_ROPE], dw[:, Z_GATE:Z_KR]], axis=1)


def _block_diag(w):
    eye = jnp.eye(RG_BLOCKS, dtype=w.dtype)
    return (w[:, :, None, :] * eye[:, None, :, None]).reshape(D_MODEL, D_MODEL).astype(BF16)


def _diag_blocks(d):
    d4 = d.reshape(RG_BLOCKS, RG_BLOCK_W, RG_BLOCKS, RG_BLOCK_W)
    return jnp.stack([d4[n, :, n, :] for n in range(RG_BLOCKS)], axis=0)


def _uq_full(g):
    p = jnp.pad(g, ((0, 0), (0, 0), (0, 2 * HEAD - HEAD - MLA_ROPE)))
    return jnp.transpose(p, (1, 0, 2)).reshape(MLA_LORA, MLA_HEADS * 2 * HEAD)


def _uq_shards(dw):
    return jnp.transpose(dw.reshape(MLA_LORA, MLA_HEADS, 2 * HEAD), (1, 0, 2))[:, :, :HEAD + MLA_ROPE]


SMALL = ("ffn1_norm", "mix_norm", "conv_b", "rg_w_a", "rg_b_a", "rg_w_x", "rg_b_x", "rg_lambda", "mla_q_norm",
         "mla_kv_norm", "ffn2_norm", "final_norm")
BIG = ("ffn1_w_gate_up", "ffn1_w_down", "w_in", "conv_w", "mla_w_uq", "mla_w_ukv", "w_branch_a", "w_branch_b",
       "w_branch_c", "w_out", "ffn2_w_gate_up", "ffn2_w_down")
ROW_SHARDED = ("ffn1_w_down", "w_branch_a", "w_branch_b", "w_branch_c", "w_out", "ffn2_w_down")


def _full_weights(gathered, l):
    g = {k: v[:, l] for k, v in gathered.items()}
    fw = {k: g[k].reshape(-1, g[k].shape[-1]) for k in ROW_SHARDED}
    fw["ffn1_w_gate_up"] = _cols_full(g["ffn1_w_gate_up"])
    fw["ffn2_w_gate_up"] = _cols_full(g["ffn2_w_gate_up"])
    fw["w_in"] = _w_in_to_z(_cols_full(g["w_in"]))
    fw["conv_w"] = _cols_full(g["conv_w"])
    fw["mla_w_uq"] = _uq_full(g["mla_w_uq"])
    fw["mla_w_ukv"] = _cols_full(g["mla_w_ukv"])
    return fw


def _grad_shards(dw):
    out = {k: dw[k].reshape(N_DEV, dw[k].shape[0] // N_DEV, dw[k].shape[1]) for k in ROW_SHARDED}
    out["ffn1_w_gate_up"] = _cols_shards(dw["ffn1_w_gate_up"])
    out["ffn2_w_gate_up"] = _cols_shards(dw["ffn2_w_gate_up"])
    out["w_in"] = _cols_shards(_z_to_w_in(dw["w_in"]))
    out["conv_w"] = _cols_shards(dw["conv_w"])
    out["mla_w_uq"] = _uq_shards(dw["mla_w_uq"])
    out["mla_w_ukv"] = _cols_shards(dw["mla_w_ukv"])
    return out


def ffn_fwd(x, norm, w_gu, w_d, tag):
    h = rms_fwd(x, norm, f"{tag}_rms")
    gu = matmul(h, w_gu, "nn", BF16, f"{tag}_gu")
    a = swiglu_fwd(gu, f"{tag}_act")
    y = matmul(a, w_d, "nn", F32, f"{tag}_down", scale=0.5, res=x, tk=1408)
    return y, (x, h, gu, a)


def ffn_bwd(dy, saved, norm, w_gu, w_d, tag):
    x, h, gu, a = saved
    da = matmul(dy, w_d, "nt", BF16, f"{tag}_dact", scale=0.5, tn=1408)
    dw_d = matmul(a, dy, "tn", BF16, f"{tag}_dwd", scale=0.5, tm=1408)
    dgu = swiglu_bwd(gu, da, f"{tag}_dgu")
    dw_gu = matmul(h, dgu, "tn", BF16, f"{tag}_dwgu")
    dh = matmul(dgu, w_gu, "nt", F32, f"{tag}_dh")
    dx, dnorm = rms_bwd(x, norm, dh, dy, f"{tag}_drms")
    return dx, dw_gu, dw_d, dnorm


def mixer_fwd(x, sp, fw, cs, sn, tag):
    h = rms_fwd(x, sp["mix_norm"], f"{tag}_rms")
    z = matmul(h, fw["w_in"], "nn", BF16, f"{tag}_in", tn=1280)
    wa, wx = _block_diag(sp["rg_w_a"]), _block_diag(sp["rg_w_x"])
    ya, hs = rglru_fwd(z, fw["conv_w"], sp["conv_b"], wa, wx, sp["rg_b_a"], sp["rg_b_x"], sp["rg_lambda"], f"{tag}_rg")
    yb, ltot = sb_fwd(z, f"{tag}_sb")
    cqn, ckvn, krope = mla_prep_fwd(z, sp["mla_q_norm"], sp["mla_kv_norm"], cs, sn, f"{tag}_mprep")
    q = q_rope(matmul(cqn, fw["mla_w_uq"], "nn", F32, f"{tag}_uq"), cs, sn, 1.0, f"{tag}_qrope")
    kv = matmul(ckvn, fw["mla_w_ukv"], "nn", BF16, f"{tag}_ukv")
    yc, lse = mla_fwd(q, kv, krope, f"{tag}_mla")
    pa = matmul(ya, fw["w_branch_a"], "nn", BF16, f"{tag}_pa")
    pb = matmul(yb, fw["w_branch_b"], "nn", BF16, f"{tag}_pb")
    pc = matmul(yc, fw["w_branch_c"], "nn", BF16, f"{tag}_pc")
    merged = merge_fwd(z, pa, pb, pc, f"{tag}_merge")
    y = matmul(merged, fw["w_out"], "nn", F32, f"{tag}_out", res=x)
    return y, (x, h, z, wa, wx, ya, hs, yb, ltot, cqn, ckvn, krope, q, kv, yc, lse, pa, pb, pc, merged)


def mixer_bwd(dy, saved, sp, fw, cs, sn, tag):
    x, h, z, wa, wx, ya, hs, yb, ltot, cqn, ckvn, krope, q, kv, yc, lse, pa, pb, pc, merged = saved
    S = x.shape[0]
    dw, ds = {}, {}
    dmerged = matmul(dy, fw["w_out"], "nt", BF16, f"{tag}_dmerged")
    dw["w_out"] = matmul(merged, dy, "tn", BF16, f"{tag}_dwout")
    dpa, dpb, dpc, dga, dgb, dgc = merge_bwd(z, pa, pb, pc, dmerged, f"{tag}_dmerge")
    dya = matmul(dpa, fw["w_branch_a"], "nt", BF16, f"{tag}_dya")
    dyb = matmul(dpb, fw["w_branch_b"], "nt", BF16, f"{tag}_dyb")
    dyc = matmul(dpc, fw["w_branch_c"], "nt", BF16, f"{tag}_dyc")
    dw["w_branch_a"] = matmul(ya, dpa, "tn", BF16, f"{tag}_dwa")
    dw["w_branch_b"] = matmul(yb, dpb, "tn", BF16, f"{tag}_dwb")
    dw["w_branch_c"] = matmul(yc, dpc, "tn", BF16, f"{tag}_dwc")
    drgx, drgg, dwa, dwx, dvec = rglru_bwd(z, hs, dya, fw["conv_w"], sp["conv_b"], wa, wx, sp["rg_b_a"], sp["rg_b_x"],
                                           sp["rg_lambda"], f"{tag}_drg")
    ds["rg_w_a"], ds["rg_w_x"] = _diag_blocks(dwa), _diag_blocks(dwx)
    ds["rg_b_a"], ds["rg_b_x"], ds["rg_lambda"], ds["conv_b"] = dvec[0], dvec[1], dvec[2], dvec[3]
    dw["conv_w"] = dvec[4:8]
    dsq, dsk, dsv = sb_bwd(z, ltot, dyb, f"{tag}_dsb")
    dq, dkv, dkr = mla_bwd(q, kv, krope, yc, lse, dyc, f"{tag}_dmla")
    dqp = q_rope(dq, cs, sn, -1.0, f"{tag}_dqrope")
    dw["mla_w_uq"] = matmul(cqn, dqp, "tn", BF16, f"{tag}_dwuq")
    dw["mla_w_ukv"] = matmul(ckvn, dkv, "tn", BF16, f"{tag}_dwukv")
    dcqn = matmul(dqp, fw["mla_w_uq"], "nt", F32, f"{tag}_dcqn")
    dckvn = matmul(dkv, fw["mla_w_ukv"], "nt", F32, f"{tag}_dckvn")
    dcq, dckv, dkrr, dqn, dkvn = mla_prep_bwd(z, sp["mla_q_norm"], sp["mla_kv_norm"], cs, sn, dcqn, dckvn, dkr, f"{tag}_dmprep")
    ds["mla_q_norm"], ds["mla_kv_norm"] = dqn[0], dkvn[0]
    dz = jnp.concatenate([drgx, drgg, dsq.astype(BF16), dsk.astype(BF16), dsv.astype(BF16), dcq, dckv, dga, dgb, dgc, dkrr,
                          jnp.zeros((S, Z_W - Z_KR - LANE), BF16)], axis=1)
    dw["w_in"] = matmul(h, dz, "tn", BF16, f"{tag}_dwin", tn=1280)
    dh = matmul(dz, fw["w_in"], "nt", F32, f"{tag}_dh", tk=1280)
    dx, dnorm = rms_bwd(x, sp["mix_norm"], dh, dy, f"{tag}_drms")
    ds["mix_norm"] = dnorm[0]
    return dx, dw, ds


def _rope_tables(positions):
    inv = ROPE_THETA ** (-jnp.arange(0, MLA_ROPE, 2, dtype=F32) / MLA_ROPE)
    ang = positions.astype(F32)[:, None] * inv
    zeros = jnp.zeros((positions.shape[0], LANE - MLA_ROPE), F32)
    cs = jnp.concatenate([jnp.cos(ang), jnp.cos(ang), zeros], axis=1)
    sn = jnp.concatenate([jnp.sin(ang), jnp.sin(ang), zeros], axis=1)
    return cs, sn


def local_step(x, positions, target, small, gathered):
    L = small["ffn1_norm"].shape[0]
    cs, sn = _rope_tables(positions)
    row = lambda v: v.reshape(1, -1)
    saved = []
    for l in range(L):
        fw = _full_weights(gathered, l)
        sp = {k: small[k][l] for k in SMALL if k != "final_norm"}
        sp = {k: (v if v.ndim == 3 else row(v)) for k, v in sp.items()}
        x, s1 = ffn_fwd(x, sp["ffn1_norm"], fw["ffn1_w_gate_up"], fw["ffn1_w_down"], f"l{l}_f1")
        x, s2 = mixer_fwd(x, sp, fw, cs, sn, f"l{l}_mx")
        x, s3 = ffn_fwd(x, sp["ffn2_norm"], fw["ffn2_w_gate_up"], fw["ffn2_w_down"], f"l{l}_f2")
        saved.append((fw, sp, s1, s2, s3))
    loss, dx, dfinal = loss_head(x, row(small["final_norm"]), target, "loss_head")
    big_grads, small_grads = [None] * L, [None] * L
    for l in reversed(range(L)):
        fw, sp, s1, s2, s3 = saved[l]
        dx, dgu2, dd2, dn2 = ffn_bwd(dx, s3, sp["ffn2_norm"], fw["ffn2_w_gate_up"], fw["ffn2_w_down"], f"l{l}_f2")
        dx, dw, ds = mixer_bwd(dx, s2, sp, fw, cs, sn, f"l{l}_mx")
        dx, dgu1, dd1, dn1 = ffn_bwd(dx, s1, sp["ffn1_norm"], fw["ffn1_w_gate_up"], fw["ffn1_w_down"], f"l{l}_f1")
        dw.update(ffn1_w_gate_up=dgu1, ffn1_w_down=dd1, ffn2_w_gate_up=dgu2, ffn2_w_down=dd2)
        ds.update(ffn1_norm=dn1[0], ffn2_norm=dn2[0])
        big_grads[l] = _grad_shards(dw)
        small_grads[l] = ds
    return loss[0, 0], dx, big_grads, small_grads, dfinal[0]


def _pack_small(tree):
    flat = jnp.concatenate([tree[k].reshape(-1).astype(F32) for k in SMALL])
    rows = -(-flat.shape[0] // (SUBLANE * D_MODEL)) * SUBLANE
    return jnp.pad(flat, (0, rows * D_MODEL - flat.shape[0])).reshape(rows, D_MODEL)


def _unpack_small(buf, like):
    flat = buf.reshape(-1)
    out, off = {}, 0
    for k in SMALL:
        n = like[k].size
        out[k] = flat[off:off + n].reshape(like[k].shape)
        off += n
    return out


NAMES = ("ffn1_norm", "ffn1_w_gate_up", "ffn1_w_down", "mix_norm", "w_in", "conv_w", "conv_b", "rg_w_a", "rg_b_a", "rg_w_x",
         "rg_b_x", "rg_lambda", "mla_q_norm", "mla_w_uq", "mla_kv_norm", "mla_w_ukv", "w_branch_a", "w_branch_b",
         "w_branch_c", "w_out", "ffn2_norm", "ffn2_w_gate_up", "ffn2_w_down", "final_norm")


def kernel(x, positions, *rest):
    n = len(NAMES)
    w = dict(zip(NAMES, rest[:n]))
    target = rest[n]
    m = dict(zip(NAMES, rest[n + 1:2 * n + 1]))
    v = dict(zip(NAMES, rest[2 * n + 1:3 * n + 1]))
    L = w["ffn1_norm"].shape[0]

    shards = [w[k] if k == "conv_w" else w[k].astype(BF16) for k in BIG]
    gathered = dict(zip(BIG, exchange(shards, True, "gather_weights")))
    small = {k: w[k] for k in SMALL}
    loss, dx, big_grads, small_grads, dfinal = local_step(x[0], positions[0], target[0], small, gathered)
    loss = lax.psum(loss, ("x", "y", "c"))

    send = [jnp.stack([big_grads[l][k] for l in range(L)], axis=1) for k in BIG]
    parts = dict(zip(BIG, exchange(send, False, "scatter_grads")))
    sg = {k: jnp.stack([small_grads[l][k] for l in range(L)], axis=0) for k in SMALL if k != "final_norm"}
    sg["final_norm"] = dfinal
    small_parts = exchange([_pack_small(sg)], True, "gather_small_grads")[0]

    res = {}
    for k in BIG:
        res[k] = adamw_sum(parts[k], w[k], m[k], v[k], f"adamw_{k}")
    packed = adamw_sum(small_parts[:, None], _pack_small(small)[None], _pack_small({k: m[k] for k in SMALL})[None],
                       _pack_small({k: v[k] for k in SMALL})[None], "adamw_small")
    unpacked = [_unpack_small(p[0], small) for p in packed]
    for k in SMALL:
        res[k] = tuple(u[k] for u in unpacked)

    outs = [loss, dx[None]]
    for i in range(4):
        outs += [res[k][i] for k in NAMES]
    return tuple(outs)
```

```python
import functools
import math

import jax
import jax.numpy as jnp
from jax import lax
from jax.experimental import pallas as pl
from jax.experimental.pallas import tpu as pltpu

F32 = jnp.float32
BF16 = jnp.bfloat16

N_DEV = 8
D_MODEL = 1024
D_FF = 2816
NORM_EPS = 1e-6
RG_BLOCKS = 16
RG_BLOCK_W = 64
RG_C = 8.0
SB_HEADS = 8
HEAD = 128
MLA_HEADS = 8
MLA_LORA = 256
MLA_ROPE = 64
ROPE_THETA = 10000.0
CHUNK = 64
SB_SCALE = HEAD ** -0.5
MLA_SCALE = (HEAD + MLA_ROPE) ** -0.5
N_IN = 8768

Z_RGX, Z_RGG, Z_Q, Z_K, Z_V, Z_CQ, Z_CKV, Z_GATE, Z_KR, Z_W = 0, 1024, 2048, 3072, 4096, 5120, 5376, 5632, 8704, 8960

ADAM_LR, ADAM_B1, ADAM_B2, ADAM_EPS, ADAM_WD, ADAM_STEP = 0.001, 0.9, 0.999, 1e-08, 0.01, 10

LANE = 128
SUBLANE = 8
VMEM_LIMIT = 48 * 1024 * 1024
NEG = -1e30


def _pcall(body, **kw):
    return pl.pallas_call(body, **kw)


def _params(*sem):
    return pltpu.CompilerParams(dimension_semantics=sem or None, vmem_limit_bytes=VMEM_LIMIT)


def _pick(dim, target):
    best = None
    t = LANE
    while t <= min(dim, target):
        if dim % t == 0:
            best = t
        t += LANE
    return best if best is not None else dim


def _sigmoid(x):
    return 1.0 / (1.0 + jnp.exp(-x))


def _gelu_and_grad(x):
    c = math.sqrt(2.0 / math.pi)
    inner = c * (x + 0.044715 * x * x * x)
    t = jnp.tanh(inner)
    val = 0.5 * x * (1.0 + t)
    grad = 0.5 * (1.0 + t) + 0.5 * x * (1.0 - t * t) * c * (1.0 + 3.0 * 0.044715 * x * x)
    return val, grad


def _neg_expm1(y):
    series = -y * (1.0 + y * (0.5 + y * (1.0 / 6.0 + y * (1.0 / 24.0))))
    return jnp.where(jnp.abs(y) < 0.02, series, 1.0 - jnp.exp(y))


def _dot(a, b, dims):
    return lax.dot_general(a, b, (dims, ((), ())), preferred_element_type=F32)


NN = ((1,), (0,))
NT = ((1,), (1,))
TN = ((0,), (0,))


def matmul(a, b, mode, out_dtype, name, scale=1.0, res=None, tm=1024, tn=1024, tk=1024):
    if mode == "nn":
        (M, K), N = a.shape, b.shape[1]
    elif mode == "nt":
        (M, K), N = a.shape, b.shape[0]
    else:
        (K, M), N = a.shape, b.shape[1]
    tm, tn, tk = _pick(M, tm), _pick(N, tn), _pick(K, tk)
    nk = K // tk
    dims = {"nn": NN, "nt": NT, "tn": TN}[mode]

    def body(*refs):
        if res is None:
            a_ref, b_ref, o_ref, acc = refs
        else:
            a_ref, b_ref, r_ref, o_ref, acc = refs
        k = pl.program_id(2)

        @pl.when(k == 0)
        def _():
            acc[...] = jnp.zeros_like(acc)

        acc[...] += _dot(a_ref[...].astype(BF16), b_ref[...].astype(BF16), dims)

        @pl.when(k == nk - 1)
        def _():
            r = acc[...] * scale
            if res is not None:
                r = r + r_ref[...]
            o_ref[...] = r.astype(out_dtype)

    a_spec = pl.BlockSpec((tk, tm), lambda i, j, k: (k, i)) if mode == "tn" else pl.BlockSpec((tm, tk), lambda i, j, k: (i, k))
    b_spec = pl.BlockSpec((tn, tk), lambda i, j, k: (j, k)) if mode == "nt" else pl.BlockSpec((tk, tn), lambda i, j, k: (k, j))
    o_spec = pl.BlockSpec((tm, tn), lambda i, j, k: (i, j))
    in_specs = [a_spec, b_spec] + ([o_spec] if res is not None else [])
    args = (a, b) + ((res,) if res is not None else ())
    return _pcall(
        body, name=name, grid=(M // tm, N // tn, nk), in_specs=in_specs, out_specs=o_spec,
        out_shape=jax.ShapeDtypeStruct((M, N), out_dtype), scratch_shapes=[pltpu.VMEM((tm, tn), F32)],
        compiler_params=_params("parallel", "parallel", "arbitrary"),
    )(*args)


def rms_fwd(x, g, name, col=0):
    S, D = x.shape[0], g.shape[1]
    tr = _pick(S, 512)

    def body(x_ref, g_ref, o_ref):
        xv = x_ref[...].astype(F32)
        r = lax.rsqrt(jnp.mean(xv * xv, axis=-1, keepdims=True) + NORM_EPS)
        o_ref[...] = (xv * r * g_ref[...]).astype(BF16)

    return _pcall(
        body, name=name, grid=(S // tr,),
        in_specs=[pl.BlockSpec((tr, D), lambda i: (i, col)), pl.BlockSpec((1, D), lambda i: (0, 0))],
        out_specs=pl.BlockSpec((tr, D), lambda i: (i, 0)),
        out_shape=jax.ShapeDtypeStruct((S, D), BF16), compiler_params=_params("parallel"),
    )(x, g)


def _rms_bwd_math(xv, g, dh):
    r = lax.rsqrt(jnp.mean(xv * xv, axis=-1, keepdims=True) + NORM_EPS)
    xhat = xv * r
    dxhat = dh * g
    dx = r * (dxhat - xhat * jnp.mean(dxhat * xhat, axis=-1, keepdims=True))
    dg = jnp.sum(dh * xhat, axis=0, keepdims=True)
    return dx, dg


def rms_bwd(x, g, dh, dres, name):
    S, D = x.shape
    tr = _pick(S, 512)

    def body(x_ref, g_ref, dh_ref, dr_ref, dx_ref, dg_ref):
        dx, dg = _rms_bwd_math(x_ref[...], g_ref[...], dh_ref[...])
        dx_ref[...] = dx + dr_ref[...]

        @pl.when(pl.program_id(0) == 0)
        def _():
            dg_ref[...] = jnp.zeros_like(dg_ref)

        dg_ref[...] += dg

    row = pl.BlockSpec((tr, D), lambda i: (i, 0))
    vec = pl.BlockSpec((1, D), lambda i: (0, 0))
    return _pcall(
        body, name=name, grid=(S // tr,), in_specs=[row, vec, row, row], out_specs=[row, vec],
        out_shape=[jax.ShapeDtypeStruct((S, D), F32), jax.ShapeDtypeStruct((1, D), F32)],
        compiler_params=_params("arbitrary"),
    )(x, g, dh, dres)


def swiglu_fwd(gu, name):
    S = gu.shape[0]
    tr, tc = _pick(S, 512), D_FF // 2
    nc = D_FF // tc

    def body(g_ref, u_ref, o_ref):
        gv = g_ref[...].astype(F32)
        o_ref[...] = (gv * _sigmoid(gv) * u_ref[...].astype(F32)).astype(BF16)

    return _pcall(
        body, name=name, grid=(S // tr, nc),
        in_specs=[pl.BlockSpec((tr, tc), lambda i, j: (i, j)), pl.BlockSpec((tr, tc), lambda i, j: (i, j + nc))],
        out_specs=pl.BlockSpec((tr, tc), lambda i, j: (i, j)),
        out_shape=jax.ShapeDtypeStruct((S, D_FF), BF16), compiler_params=_params("parallel", "parallel"),
    )(gu, gu)


def swiglu_bwd(gu, da, name):
    S = gu.shape[0]
    tr, tc = _pick(S, 512), D_FF // 2
    nc = D_FF // tc

    def body(g_ref, u_ref, da_ref, o_ref):
        gv = g_ref[...].astype(F32)
        uv = u_ref[...].astype(F32)
        dav = da_ref[...].astype(F32)
        sg = _sigmoid(gv)
        half = pl.program_id(1)

        @pl.when(half == 0)
        def _():
            o_ref[...] = (dav * uv * sg * (1.0 + gv * (1.0 - sg))).astype(BF16)

        @pl.when(half == 1)
        def _():
            o_ref[...] = (dav * gv * sg).astype(BF16)

    return _pcall(
        body, name=name, grid=(S // tr, 2, nc),
        in_specs=[pl.BlockSpec((tr, tc), lambda i, h, j: (i, j)), pl.BlockSpec((tr, tc), lambda i, h, j: (i, j + nc)),
                  pl.BlockSpec((tr, tc), lambda i, h, j: (i, j))],
        out_specs=pl.BlockSpec((tr, tc), lambda i, h, j: (i, h * nc + j)),
        out_shape=jax.ShapeDtypeStruct((S, 2 * D_FF), BF16), compiler_params=_params("parallel", "parallel", "parallel"),
    )(gu, gu, da)


def _conv_taps(xpad, T, cw, cb):
    u = cb + cw[3:4, :] * xpad[pl.ds(8, T), :]
    for tap in range(3):
        u = u + cw[tap:tap + 1, :] * xpad[pl.ds(5 + tap, T), :]
    return u


def _rg_gates(u, wa_ref, wx_ref, ba, bx, lam):
    ub = u.astype(BF16)
    r = _sigmoid(_dot(ub, wa_ref[...], NN) + ba)
    ig = _sigmoid(_dot(ub, wx_ref[...], NN) + bx)
    nlam = -lam
    clam = -RG_C * (jnp.maximum(nlam, 0.0) + jnp.log(1.0 + jnp.exp(-jnp.abs(nlam))))
    la = clam * r
    return r, ig, clam, la


def rglru_fwd(z, cw, cb, wa, wx, ba, bx, lam, name):
    S, D = z.shape[0], D_MODEL
    T = _pick(S, 256)

    def body(x_ref, g_ref, cw_ref, cb_ref, wa_ref, wx_ref, ba_ref, bx_ref, lam_ref, y_ref, h_ref, xpad, a_s, b_s, hst):
        @pl.when(pl.program_id(0) == 0)
        def _():
            xpad[pl.ds(0, 8), :] = jnp.zeros((8, D), F32)
            hst[...] = jnp.zeros_like(hst)

        xpad[pl.ds(8, T), :] = x_ref[...].astype(F32)
        u = _conv_taps(xpad, T, cw_ref[...], cb_ref[...])
        xpad[pl.ds(0, 8), :] = xpad[pl.ds(T, 8), :]
        r, ig, clam, la = _rg_gates(u, wa_ref, wx_ref, ba_ref[...], bx_ref[...], lam_ref[...])
        a_s[...] = jnp.exp(la)
        b_s[...] = jnp.sqrt(_neg_expm1(2.0 * la)) * (ig * u)

        def tile(j, h):
            r0 = pl.multiple_of(j * 8, 8)
            av = a_s[pl.ds(r0, 8), :]
            bv = b_s[pl.ds(r0, 8), :]
            rows = []
            for k in range(8):
                h = av[k:k + 1, :] * h + bv[k:k + 1, :]
                rows.append(h)
            h_ref[pl.ds(r0, 8), :] = jnp.concatenate(rows, axis=0)
            return h

        hst[...] = lax.fori_loop(0, T // 8, tile, hst[...])
        gel, _ = _gelu_and_grad(g_ref[...].astype(F32))
        y_ref[...] = (h_ref[...] * gel).astype(BF16)

    blk = lambda c: pl.BlockSpec((T, D), lambda i: (i, c))
    vec = pl.BlockSpec((1, D), lambda i: (0, 0))
    full = lambda r: pl.BlockSpec((r, D), lambda i: (0, 0))
    return _pcall(
        body, name=name, grid=(S // T,),
        in_specs=[blk(0), blk(1), full(4), vec, full(D), full(D), vec, vec, vec],
        out_specs=[blk(0), blk(0)],
        out_shape=[jax.ShapeDtypeStruct((S, D), BF16), jax.ShapeDtypeStruct((S, D), F32)],
        scratch_shapes=[pltpu.VMEM((T + 8, D), F32), pltpu.VMEM((T, D), F32), pltpu.VMEM((T, D), F32), pltpu.VMEM((1, D), F32)],
        compiler_params=_params("arbitrary"),
    )(z, z, cw, cb, wa, wx, ba, bx, lam)


def rglru_bwd(z, hs, dy, cw, cb, wa, wx, ba, bx, lam, name):
    S, D = z.shape[0], D_MODEL
    T = _pick(S, 256)
    nb = S // T
    t8 = T // 8

    def body(x_ref, xp_ref, g_ref, h_ref, hp_ref, dy_ref, cw_ref, cb_ref, wa_ref, wx_ref, ba_ref, bx_ref, lam_ref,
             dx_ref, dg_ref, dwa_ref, dwx_ref, dvec_ref, xpad, hpad, dupad, a_s, d_s, carry):
        i = pl.program_id(0)
        first_block = i == nb - 1

        @pl.when(i == 0)
        def _():
            dwa_ref[...] = jnp.zeros_like(dwa_ref)
            dwx_ref[...] = jnp.zeros_like(dwx_ref)
            dvec_ref[...] = jnp.zeros_like(dvec_ref)
            carry[...] = jnp.zeros_like(carry)
            dupad[pl.ds(T, 8), :] = jnp.zeros((8, D), F32)

        keep = jnp.where(first_block, 0.0, 1.0)
        xpad[pl.ds(0, 8), :] = xp_ref[...].astype(F32) * keep
        xpad[pl.ds(8, T), :] = x_ref[...].astype(F32)
        hpad[pl.ds(0, 8), :] = hp_ref[...] * keep
        hpad[pl.ds(8, T), :] = h_ref[...]
        cwv = cw_ref[...]
        u = _conv_taps(xpad, T, cwv, cb_ref[...])
        r, ig, clam, la = _rg_gates(u, wa_ref, wx_ref, ba_ref[...], bx_ref[...], lam_ref[...])
        a = jnp.exp(la)
        a_s[...] = a
        gv = g_ref[...].astype(F32)
        gel, dgel = _gelu_and_grad(gv)
        dyv = dy_ref[...].astype(F32)
        d_s[...] = dyv * gel
        dg_ref[...] = (dyv * h_ref[...] * dgel).astype(BF16)

        def tile(j, c):
            r0 = pl.multiple_of((t8 - 1 - j) * 8, 8)
            av = a_s[pl.ds(r0, 8), :]
            dv = d_s[pl.ds(r0, 8), :]
            rows = [None] * 8
            for k in range(7, -1, -1):
                d = dv[k:k + 1, :] + c
                rows[k] = d
                c = av[k:k + 1, :] * d
            d_s[pl.ds(r0, 8), :] = jnp.concatenate(rows, axis=0)
            return c

        carry[...] = lax.fori_loop(0, t8, tile, carry[...])
        dht = d_s[...]
        hprev = hpad[pl.ds(7, T), :]
        w = _neg_expm1(2.0 * la)
        s = jnp.sqrt(w)
        e2 = 1.0 - w
        d_iu = dht * s
        dla = dht * hprev * a - dht * (ig * u) * e2 / s
        dpr = (dla * clam * r * (1.0 - r))
        dpi = (d_iu * u * ig * (1.0 - ig))
        dprb, dpib, ub = dpr.astype(BF16), dpi.astype(BF16), u.astype(BF16)
        du = d_iu * ig + _dot(dprb, wa_ref[...], NT) + _dot(dpib, wx_ref[...], NT)
        dwa_ref[...] += _dot(ub, dprb, TN)
        dwx_ref[...] += _dot(ub, dpib, TN)
        dvec_ref[0:1, :] += jnp.sum(dpr, axis=0, keepdims=True)
        dvec_ref[1:2, :] += jnp.sum(dpi, axis=0, keepdims=True)
        dvec_ref[2:3, :] += jnp.sum(dla * r, axis=0, keepdims=True)
        dvec_ref[3:4, :] += jnp.sum(du, axis=0, keepdims=True)
        for tap in range(4):
            dvec_ref[4 + tap:5 + tap, :] += jnp.sum(du * xpad[pl.ds(5 + tap, T), :], axis=0, keepdims=True)
        dupad[pl.ds(0, T), :] = du
        dx = cwv[3:4, :] * du
        for tap in range(3):
            dx = dx + cwv[tap:tap + 1, :] * dupad[pl.ds(3 - tap, T), :]
        dx_ref[...] = dx.astype(BF16)
        dupad[pl.ds(T, 8), :] = dupad[pl.ds(0, 8), :]

        @pl.when(first_block)
        def _():
            dvec_ref[2:3, :] = dvec_ref[2:3, :] * (RG_C * _sigmoid(-lam_ref[...]))

    rev = lambda c: pl.BlockSpec((T, D), lambda i: (nb - 1 - i, c))
    prev = lambda c: pl.BlockSpec((8, D), lambda i: (jnp.maximum((nb - 1 - i) * t8 - 1, 0), c))
    vec = pl.BlockSpec((1, D), lambda i: (0, 0))
    full = lambda r: pl.BlockSpec((r, D), lambda i: (0, 0))
    return _pcall(
        body, name=name, grid=(nb,),
        in_specs=[rev(0), prev(0), rev(1), rev(0), prev(0), rev(0), full(4), vec, full(D), full(D), vec, vec, vec],
        out_specs=[rev(0), rev(0), full(D), full(D), full(8)],
        out_shape=[jax.ShapeDtypeStruct((S, D), BF16), jax.ShapeDtypeStruct((S, D), BF16),
                   jax.ShapeDtypeStruct((D, D), F32), jax.ShapeDtypeStruct((D, D), F32), jax.ShapeDtypeStruct((8, D), F32)],
        scratch_shapes=[pltpu.VMEM((T + 8, D), F32), pltpu.VMEM((T + 8, D), F32), pltpu.VMEM((T + 8, D), F32),
                        pltpu.VMEM((T, D), F32), pltpu.VMEM((T, D), F32), pltpu.VMEM((1, D), F32)],
        compiler_params=_params("arbitrary"),
    )(z, z, z, hs, hs, dy, cw, cb, wa, wx, ba, bx, lam)


def _tri(n, kind):
    j = lax.broadcasted_iota(jnp.int32, (n, n), 0)
    s = lax.broadcasted_iota(jnp.int32, (n, n), 1)
    m = {"gt": j > s, "le": j <= s, "lt": j < s}[kind]
    return jnp.where(m, 1.0, 0.0).astype(BF16)


def _dot2(x, tri):
    hi = x.astype(BF16)
    lo = (x - hi.astype(F32)).astype(BF16)
    return _dot(hi, tri, NN) + _dot(lo, tri, NN)


SB_TK = 128


def _sb_logits(q, kblk, q0, k0, tq, masked):
    z = _dot(q, kblk, NT) * SB_SCALE
    sp = jnp.maximum(z, 0.0) + jnp.log(1.0 + jnp.exp(-jnp.abs(z)))
    lkeep = -sp
    mask = None
    if masked:
        tpos = q0 + lax.broadcasted_iota(jnp.int32, (tq, SB_TK), 0)
        spos = k0 + lax.broadcasted_iota(jnp.int32, (tq, SB_TK), 1)
        mask = spos < tpos
        lkeep = jnp.where(mask, lkeep, 0.0)
    return mask, lkeep, z - sp


def sb_fwd(z, name):
    S = z.shape[0]
    tq, tk = _pick(S, 512), SB_TK
    nd = tq // tk
    U = min(4, nd)
    qc, kc, vc = Z_Q // HEAD, Z_K // HEAD, Z_V // HEAD

    def body(q_ref, k_ref, v_ref, o_ref, lt_ref, acc, run):
        qi = pl.program_id(1)
        q0 = qi * tq
        q = q_ref[...]
        tri = _tri(tk, "gt")
        acc[...] = jnp.zeros_like(acc)
        run[...] = jnp.zeros_like(run)

        def group(k0s, masked):
            parts = [(k0,) + _sb_logits(q, k_ref[pl.ds(k0, tk), :], q0, k0, tq, masked) for k0 in k0s]
            cums = [_dot2(p[2], tri) for p in parts]
            r, a = run[...], acc[...]
            for (k0, mask, lkeep, lbeta), cum in zip(parts, cums):
                w = jnp.exp(lbeta + cum + r)
                if masked:
                    w = jnp.where(mask, w, 0.0)
                a = a + _dot(w.astype(BF16), v_ref[pl.ds(k0, tk), :], NN)
                r = r + jnp.sum(lkeep, axis=1, keepdims=True)
            acc[...] = a
            run[...] = r

        for g in range(nd // U):
            group([pl.multiple_of(q0 + (nd - 1 - g * U - u) * tk, tk) for u in range(U)], True)

        def step(i, c):
            base = qi * nd - 1 - i * U
            group([pl.multiple_of((base - u) * tk, tk) for u in range(U)], False)
            return c

        lax.fori_loop(0, qi * nd // U, step, 0)
        o_ref[...] = acc[...].astype(BF16)
        lt_ref[0] = run[...]

    return _pcall(
        body, name=name, grid=(SB_HEADS, S // tq),
        in_specs=[pl.BlockSpec((tq, HEAD), lambda h, i: (i, qc + h)), pl.BlockSpec((S, HEAD), lambda h, i: (0, kc + h)),
                  pl.BlockSpec((S, HEAD), lambda h, i: (0, vc + h))],
        out_specs=[pl.BlockSpec((tq, HEAD), lambda h, i: (i, h)), pl.BlockSpec((1, tq, 1), lambda h, i: (h, i, 0))],
        out_shape=[jax.ShapeDtypeStruct((S, SB_HEADS * HEAD), BF16), jax.ShapeDtypeStruct((SB_HEADS, S, 1), F32)],
        scratch_shapes=[pltpu.VMEM((tq, HEAD), F32), pltpu.VMEM((tq, 1), F32)],
        compiler_params=_params("parallel", "parallel"),
    )(z, z, z)


def sb_bwd(z, ltot, dy, name):
    S = z.shape[0]
    tq, tk = _pick(S, 512), SB_TK
    nd = tq // tk
    U = min(4, nd)
    nkb = S // tk
    qc, kc, vc = Z_Q // HEAD, Z_K // HEAD, Z_V // HEAD

    def body(q_ref, k_ref, v_ref, lt_ref, do_ref, dq_ref, dk_ref, dv_ref, dq_s, run_l, run_g, dkT, dvT):
        qi = pl.program_id(1)
        q0 = qi * tq

        @pl.when(qi == 0)
        def _():
            dkT[...] = jnp.zeros_like(dkT)
            dvT[...] = jnp.zeros_like(dvT)

        q = q_ref[...]
        do = do_ref[...].astype(BF16)
        qT, doT = q.T, do.T
        ltv = lt_ref[0]
        tri_le, tri_lt = _tri(tk, "le"), _tri(tk, "lt")
        dq_s[...] = jnp.zeros_like(dq_s)
        run_l[...] = jnp.zeros_like(run_l)
        run_g[...] = jnp.zeros_like(run_g)

        def group(k0s, masked):
            parts = []
            for k0 in k0s:
                kblk = k_ref[pl.ds(k0, tk), :]
                mask, lkeep, lbeta = _sb_logits(q, kblk, q0, k0, tq, masked)
                parts.append((k0, kblk, mask, lkeep, lbeta, _dot(do, v_ref[pl.ds(k0, tk), :], NT)))
            pres = [_dot2(p[3], tri_le) for p in parts]
            rl = run_l[...]
            ws = []
            for (k0, kblk, mask, lkeep, lbeta, dw), pre in zip(parts, pres):
                w = jnp.exp(lbeta + (ltv - (pre + rl)))
                if masked:
                    w = jnp.where(mask, w, 0.0)
                ws.append((w, w * dw))
                rl = rl + jnp.sum(lkeep, axis=1, keepdims=True)
            run_l[...] = rl
            gpres = [_dot2(g, tri_lt) for _, g in ws]
            rg, dq = run_g[...], dq_s[...]
            for (k0, kblk, mask, lkeep, lbeta, dw), (w, g), gpre in zip(parts, ws, gpres):
                dz = (g * jnp.exp(lkeep) - jnp.exp(lbeta) * (gpre + rg)) * SB_SCALE
                if masked:
                    dz = jnp.where(mask, dz, 0.0)
                dz = dz.astype(BF16)
                dq = dq + _dot(dz, kblk, NN)
                kb = k0 // tk
                dkT[kb] += _dot(qT, dz, NN)
                dvT[kb] += _dot(doT, w.astype(BF16), NN)
                rg = rg + jnp.sum(g, axis=1, keepdims=True)
            run_g[...] = rg
            dq_s[...] = dq

        def step(i, c):
            group([pl.multiple_of((i * U + u) * tk, tk) for u in range(U)], False)
            return c

        lax.fori_loop(0, qi * nd // U, step, 0)
        for g in range(nd // U):
            group([pl.multiple_of(q0 + (g * U + u) * tk, tk) for u in range(U)], True)
        dq_ref[...] = dq_s[...]

        @pl.when(qi == pl.num_programs(1) - 1)
        def _():
            def flush(kb, c):
                r0 = pl.multiple_of(kb * tk, tk)
                dk_ref[pl.ds(r0, tk), :] = dkT[kb].T
                dv_ref[pl.ds(r0, tk), :] = dvT[kb].T
                return c

            lax.fori_loop(0, nkb, flush, 0)

    qblk = lambda c: pl.BlockSpec((tq, HEAD), lambda h, i: (i, c + h))
    kfull = lambda c: pl.BlockSpec((S, HEAD), lambda h, i: (0, c + h))
    out = jax.ShapeDtypeStruct((S, SB_HEADS * HEAD), F32)
    return _pcall(
        body, name=name, grid=(SB_HEADS, S // tq),
        in_specs=[qblk(qc), kfull(kc), kfull(vc), pl.BlockSpec((1, tq, 1), lambda h, i: (h, i, 0)), qblk(0)],
        out_specs=[qblk(0), kfull(0), kfull(0)], out_shape=[out, out, out],
        scratch_shapes=[pltpu.VMEM((tq, HEAD), F32), pltpu.VMEM((tq, 1), F32), pltpu.VMEM((tq, 1), F32),
                        pltpu.VMEM((nkb, HEAD, tk), F32), pltpu.VMEM((nkb, HEAD, tk), F32)],
        compiler_params=_params("arbitrary", "arbitrary"),
    )(z, z, z, ltot, dy)


def _rope(x, cs, sn, sign):
    lane = lax.broadcasted_iota(jnp.int32, x.shape, 1)
    swapped = jnp.where(lane < MLA_ROPE // 2, -pltpu.roll(x, LANE - MLA_ROPE // 2, 1), pltpu.roll(x, MLA_ROPE // 2, 1))
    return x * cs + sign * swapped * sn


def mla_prep_fwd(z, qn, kvn, cs, sn, name):
    S = z.shape[0]
    tr = _pick(S, 512)

    def body(cq_ref, ckv_ref, kr_ref, qn_ref, kvn_ref, cs_ref, sn_ref, oq_ref, okv_ref, okr_ref):
        for src, g, dst in ((cq_ref, qn_ref, oq_ref), (ckv_ref, kvn_ref, okv_ref)):
            xv = src[...].astype(F32)
            r = lax.rsqrt(jnp.mean(xv * xv, axis=-1, keepdims=True) + NORM_EPS)
            dst[...] = (xv * r * g[...]).astype(BF16)
        okr_ref[...] = _rope(kr_ref[...].astype(F32), cs_ref[...], sn_ref[...], 1.0).astype(BF16)

    lora = lambda c: pl.BlockSpec((tr, MLA_LORA), lambda i: (i, c))
    tile = lambda c: pl.BlockSpec((tr, LANE), lambda i: (i, c))
    vec = pl.BlockSpec((1, MLA_LORA), lambda i: (0, 0))
    return _pcall(
        body, name=name, grid=(S // tr,),
        in_specs=[lora(Z_CQ // MLA_LORA), lora(Z_CKV // MLA_LORA), tile(Z_KR // LANE), vec, vec, tile(0), tile(0)],
        out_specs=[lora(0), lora(0), tile(0)],
        out_shape=[jax.ShapeDtypeStruct((S, MLA_LORA), BF16), jax.ShapeDtypeStruct((S, MLA_LORA), BF16),
                   jax.ShapeDtypeStruct((S, LANE), BF16)],
        compiler_params=_params("parallel"),
    )(z, z, z, qn, kvn, cs, sn)


def mla_prep_bwd(z, qn, kvn, cs, sn, dcqn, dckvn, dkrope, name):
    S = z.shape[0]
    tr = _pick(S, 512)

    def body(cq_ref, ckv_ref, qn_ref, kvn_ref, cs_ref, sn_ref, dq_ref, dkv_ref, dkr_ref, oq_ref, okv_ref, okr_ref, gq_ref, gkv_ref):
        @pl.when(pl.program_id(0) == 0)
        def _():
            gq_ref[...] = jnp.zeros_like(gq_ref)
            gkv_ref[...] = jnp.zeros_like(gkv_ref)

        for src, g, dh, dst, gacc in ((cq_ref, qn_ref, dq_ref, oq_ref, gq_ref), (ckv_ref, kvn_ref, dkv_ref, okv_ref, gkv_ref)):
            dx, dg = _rms_bwd_math(src[...].astype(F32), g[...], dh[...])
            dst[...] = dx.astype(BF16)
            gacc[...] += dg
        okr_ref[...] = _rope(dkr_ref[...], cs_ref[...], sn_ref[...], -1.0).astype(BF16)

    lora = lambda c: pl.BlockSpec((tr, MLA_LORA), lambda i: (i, c))
    tile = lambda c: pl.BlockSpec((tr, LANE), lambda i: (i, c))
    vec = pl.BlockSpec((1, MLA_LORA), lambda i: (0, 0))
    return _pcall(
        body, name=name, grid=(S // tr,),
        in_specs=[lora(Z_CQ // MLA_LORA), lora(Z_CKV // MLA_LORA), vec, vec, tile(0), tile(0), lora(0), lora(0), tile(0)],
        out_specs=[lora(0), lora(0), tile(0), vec, vec],
        out_shape=[jax.ShapeDtypeStruct((S, MLA_LORA), BF16), jax.ShapeDtypeStruct((S, MLA_LORA), BF16),
                   jax.ShapeDtypeStruct((S, LANE), BF16), jax.ShapeDtypeStruct((1, MLA_LORA), F32), jax.ShapeDtypeStruct((1, MLA_LORA), F32)],
        compiler_params=_params("arbitrary"),
    )(z, z, qn, kvn, cs, sn, dcqn, dckvn, dkrope)


def q_rope(q, cs, sn, sign, name):
    S = q.shape[0]
    tr = _pick(S, 512)

    def body(q_ref, cs_ref, sn_ref, o_ref):
        o_ref[:, 0:LANE] = q_ref[:, 0:LANE].astype(BF16)
        o_ref[:, LANE:2 * LANE] = _rope(q_ref[:, LANE:2 * LANE], cs_ref[...], sn_ref[...], sign).astype(BF16)

    blk = pl.BlockSpec((tr, 2 * LANE), lambda i, h: (i, h))
    tile = pl.BlockSpec((tr, LANE), lambda i, h: (i, 0))
    return _pcall(
        body, name=name, grid=(S // tr, MLA_HEADS), in_specs=[blk, tile, tile], out_specs=blk,
        out_shape=jax.ShapeDtypeStruct(q.shape, BF16), compiler_params=_params("parallel", "parallel"),
    )(q, cs, sn)


def _mla_scores(qn, qr, kv_ref, kr_ref, k0, t, masked):
    kn = kv_ref[pl.ds(k0, t), 0:HEAD]
    vv = kv_ref[pl.ds(k0, t), HEAD:2 * HEAD]
    kr = kr_ref[pl.ds(k0, t), :]
    s = (_dot(qn, kn, NT) + _dot(qr, kr, NT)) * MLA_SCALE
    mask = None
    if masked:
        tch = lax.broadcasted_iota(jnp.int32, (t, t), 0) // CHUNK
        sch = lax.broadcasted_iota(jnp.int32, (t, t), 1) // CHUNK
        mask = sch <= tch
    return s, mask, kn, vv, kr


def mla_fwd(q, kv, kr, name):
    S = q.shape[0]
    t = _pick(S, 512)

    def body(q_ref, kv_ref, kr_ref, o_ref, lse_ref, m_s, l_s, acc):
        qi = pl.program_id(1)
        qn, qr = q_ref[:, 0:HEAD], q_ref[:, HEAD:2 * HEAD]
        m_s[...] = jnp.full_like(m_s, NEG)
        l_s[...] = jnp.zeros_like(l_s)
        acc[...] = jnp.zeros_like(acc)

        def block(kb, masked):
            s, mask, _, vv, _ = _mla_scores(qn, qr, kv_ref, kr_ref, pl.multiple_of(kb * t, t), t, masked)
            if masked:
                s = jnp.where(mask, s, NEG)
            m = m_s[...]
            m2 = jnp.maximum(m, jnp.max(s, axis=1, keepdims=True))
            p = jnp.exp(s - m2)
            alpha = jnp.exp(m - m2)
            l_s[...] = alpha * l_s[...] + jnp.sum(p, axis=1, keepdims=True)
            acc[...] = alpha * acc[...] + _dot(p.astype(BF16), vv, NN)
            m_s[...] = m2

        def step(kb, c):
            block(kb, False)
            return c

        lax.fori_loop(0, qi, step, 0)
        block(qi, True)
        o_ref[...] = (acc[...] / l_s[...]).astype(BF16)
        lse_ref[0] = m_s[...] + jnp.log(l_s[...])

    return _pcall(
        body, name=name, grid=(MLA_HEADS, S // t),
        in_specs=[pl.BlockSpec((t, 2 * HEAD), lambda h, i: (i, h)), pl.BlockSpec((S, 2 * HEAD), lambda h, i: (0, h)),
                  pl.BlockSpec((S, LANE), lambda h, i: (0, 0))],
        out_specs=[pl.BlockSpec((t, HEAD), lambda h, i: (i, h)), pl.BlockSpec((1, t, 1), lambda h, i: (h, i, 0))],
        out_shape=[jax.ShapeDtypeStruct((S, MLA_HEADS * HEAD), BF16), jax.ShapeDtypeStruct((MLA_HEADS, S, 1), F32)],
        scratch_shapes=[pltpu.VMEM((t, 1), F32), pltpu.VMEM((t, 1), F32), pltpu.VMEM((t, HEAD), F32)],
        compiler_params=_params("parallel", "parallel"),
    )(q, kv, kr)


def mla_bwd(q, kv, kr, o, lse, do, name):
    S = q.shape[0]
    t = _pick(S, 512)
    nkb = S // t

    def body(q_ref, kv_ref, kr_ref, o_ref, lse_ref, do_ref, dq_ref, dkv_ref, dkr_ref, dqn_s, dqr_s, dknT, dvT, dkrT):
        h, qi = pl.program_id(0), pl.program_id(1)

        @pl.when(qi == 0)
        def _():
            dknT[...] = jnp.zeros_like(dknT)
            dvT[...] = jnp.zeros_like(dvT)

        @pl.when((qi == 0) & (h == 0))
        def _():
            dkrT[...] = jnp.zeros_like(dkrT)

        qn, qr = q_ref[:, 0:HEAD], q_ref[:, HEAD:2 * HEAD]
        dov = do_ref[...].astype(F32)
        dob = dov.astype(BF16)
        qnT, qrT, doT = qn.T, qr.T, dob.T
        delta = jnp.sum(dov * o_ref[...].astype(F32), axis=1, keepdims=True)
        lsev = lse_ref[0]
        dqn_s[...] = jnp.zeros_like(dqn_s)
        dqr_s[...] = jnp.zeros_like(dqr_s)

        def block(kb, masked):
            s, mask, kn, vv, krb = _mla_scores(qn, qr, kv_ref, kr_ref, pl.multiple_of(kb * t, t), t, masked)
            p = jnp.exp(s - lsev)
            if masked:
                p = jnp.where(mask, p, 0.0)
            ds = (p * (_dot(dob, vv, NT) - delta) * MLA_SCALE).astype(BF16)
            dknT[kb] += _dot(qnT, ds, NN)
            dvT[kb] += _dot(doT, p.astype(BF16), NN)
            dkrT[kb] += _dot(qrT, ds, NN)
            dqn_s[...] += _dot(ds, kn, NN)
            dqr_s[...] += _dot(ds, krb, NN)

        def step(kb, c):
            block(kb, False)
            return c

        lax.fori_loop(0, qi, step, 0)
        block(qi, True)
        dq_ref[:, 0:HEAD] = dqn_s[...]
        dq_ref[:, HEAD:2 * HEAD] = dqr_s[...]
        last_q = qi == pl.num_programs(1) - 1

        @pl.when(last_q)
        def _():
            def flush(kb, c):
                r0 = pl.multiple_of(kb * t, t)
                dkv_ref[pl.ds(r0, t), 0:HEAD] = dknT[kb].T
                dkv_ref[pl.ds(r0, t), HEAD:2 * HEAD] = dvT[kb].T
                return c

            lax.fori_loop(0, nkb, flush, 0)

        @pl.when(last_q & (h == pl.num_programs(0) - 1))
        def _():
            def flush(kb, c):
                r0 = pl.multiple_of(kb * t, t)
                dkr_ref[pl.ds(r0, t), :] = dkrT[kb].T
                return c

            lax.fori_loop(0, nkb, flush, 0)

    qblk = pl.BlockSpec((t, 2 * HEAD), lambda h, i: (i, h))
    kvfull = pl.BlockSpec((S, 2 * HEAD), lambda h, i: (0, h))
    krfull = pl.BlockSpec((S, LANE), lambda h, i: (0, 0))
    oblk = pl.BlockSpec((t, HEAD), lambda h, i: (i, h))
    return _pcall(
        body, name=name, grid=(MLA_HEADS, S // t),
        in_specs=[qblk, kvfull, krfull, oblk, pl.BlockSpec((1, t, 1), lambda h, i: (h, i, 0)), oblk],
        out_specs=[qblk, kvfull, krfull],
        out_shape=[jax.ShapeDtypeStruct(q.shape, F32), jax.ShapeDtypeStruct(kv.shape, F32), jax.ShapeDtypeStruct((S, LANE), F32)],
        scratch_shapes=[pltpu.VMEM((t, HEAD), F32), pltpu.VMEM((t, LANE), F32), pltpu.VMEM((nkb, HEAD, t), F32),
                        pltpu.VMEM((nkb, HEAD, t), F32), pltpu.VMEM((nkb, LANE, t), F32)],
        compiler_params=_params("arbitrary", "arbitrary"),
    )(q, kv, kr, o, lse, do)


GATE_TC = 512


def merge_fwd(z, ya, yb, yc, name):
    S = z.shape[0]
    tr, tc = _pick(S, 512), GATE_TC
    g0 = Z_GATE // tc
    nc = D_MODEL // tc

    def body(ga_ref, gb_ref, gc_ref, ya_ref, yb_ref, yc_ref, o_ref):
        acc = None
        for g, y in ((ga_ref, ya_ref), (gb_ref, yb_ref), (gc_ref, yc_ref)):
            term = _sigmoid(g[...].astype(F32)) * y[...].astype(F32)
            acc = term if acc is None else acc + term
        o_ref[...] = acc.astype(BF16)

    gate = lambda b: pl.BlockSpec((tr, tc), lambda i, j: (i, g0 + b * nc + j))
    blk = pl.BlockSpec((tr, tc), lambda i, j: (i, j))
    return _pcall(
        body, name=name, grid=(S // tr, nc), in_specs=[gate(0), gate(1), gate(2), blk, blk, blk], out_specs=blk,
        out_shape=jax.ShapeDtypeStruct((S, D_MODEL), BF16), compiler_params=_params("parallel", "parallel"),
    )(z, z, z, ya, yb, yc)


def merge_bwd(z, ya, yb, yc, dm, name):
    S = z.shape[0]
    tr, tc = _pick(S, 512), GATE_TC
    g0 = Z_GATE // tc
    nc = D_MODEL // tc

    def body(ga_ref, gb_ref, gc_ref, ya_ref, yb_ref, yc_ref, dm_ref, da_ref, db_ref, dc_ref, dga_ref, dgb_ref, dgc_ref):
        dmv = dm_ref[...].astype(F32)
        for g, y, dy, dg in ((ga_ref, ya_ref, da_ref, dga_ref), (gb_ref, yb_ref, db_ref, dgb_ref), (gc_ref, yc_ref, dc_ref, dgc_ref)):
            sg = _sigmoid(g[...].astype(F32))
            dy[...] = (dmv * sg).astype(BF16)
            dg[...] = (dmv * y[...].astype(F32) * sg * (1.0 - sg)).astype(BF16)

    gate = lambda b: pl.BlockSpec((tr, tc), lambda i, j: (i, g0 + b * nc + j))
    blk = pl.BlockSpec((tr, tc), lambda i, j: (i, j))
    out = jax.ShapeDtypeStruct((S, D_MODEL), BF16)
    return _pcall(
        body, name=name, grid=(S // tr, nc), in_specs=[gate(0), gate(1), gate(2), blk, blk, blk, blk],
        out_specs=[blk] * 6, out_shape=[out] * 6, compiler_params=_params("parallel", "parallel"),
    )(z, z, z, ya, yb, yc, dm)


def loss_head(x, g, target, name):
    S, D = x.shape
    tr = _pick(S, 512)

    def body(x_ref, g_ref, t_ref, l_ref, dx_ref, dg_ref):
        @pl.when(pl.program_id(0) == 0)
        def _():
            l_ref[...] = jnp.zeros_like(l_ref)
            dg_ref[...] = jnp.zeros_like(dg_ref)

        xv, gv = x_ref[...], g_ref[...]
        r = lax.rsqrt(jnp.mean(xv * xv, axis=-1, keepdims=True) + NORM_EPS)
        diff = xv * r * gv - t_ref[...]
        l_ref[...] += 0.5 * jnp.sum(jnp.mean(diff * diff, axis=-1, keepdims=True), axis=0, keepdims=True)
        dx, dg = _rms_bwd_math(xv, gv, diff * (1.0 / D))
        dx_ref[...] = dx
        dg_ref[...] += dg

    row = pl.BlockSpec((tr, D), lambda i: (i, 0))
    vec = pl.BlockSpec((1, D), lambda i: (0, 0))
    return _pcall(
        body, name=name, grid=(S // tr,), in_specs=[row, vec, row],
        out_specs=[pl.BlockSpec((1, LANE), lambda i: (0, 0)), row, vec],
        out_shape=[jax.ShapeDtypeStruct((1, LANE), F32), jax.ShapeDtypeStruct((S, D), F32), jax.ShapeDtypeStruct((1, D), F32)],
        compiler_params=_params("arbitrary"),
    )(x, g, target)


def _peer(k, x, y, c):
    px = 1 - x if k & 4 else x
    py = 1 - y if k & 2 else y
    pc = 1 - c if k & 1 else c
    return (px, py, pc), 4 * px + 2 * py + pc


def exchange(arrs, gather, name):
    n = len(arrs)
    shapes = [((N_DEV,) + a.shape) if gather else a.shape for a in arrs]

    def body(*refs):
        ins, outs = refs[:n], refs[n:2 * n]
        send_sems, recv_sems, loc_sems = refs[2 * n:]
        x, y, c = lax.axis_index("x"), lax.axis_index("y"), lax.axis_index("c")
        me = 4 * x + 2 * y + c
        sends, recvs, locs = [], [], []
        for a in range(n):
            loc = pltpu.make_async_copy(ins[a] if gather else ins[a].at[me], outs[a].at[me], loc_sems.at[a])
            loc.start()
            locs.append(loc)
            for k in range(1, N_DEV):
                peer, pid = _peer(k, x, y, c)
                s = a * (N_DEV - 1) + k - 1
                src = ins[a] if gather else ins[a].at[pid]
                snd = pltpu.make_async_remote_copy(src_ref=src, dst_ref=outs[a].at[me], send_sem=send_sems.at[s],
                                                   recv_sem=recv_sems.at[s], device_id=peer, device_id_type=pl.DeviceIdType.MESH)
                snd.start()
                sends.append(snd)
                recvs.append(pltpu.make_async_remote_copy(src_ref=src, dst_ref=outs[a].at[pid], send_sem=send_sems.at[s],
                                                          recv_sem=recv_sems.at[s], device_id=peer, device_id_type=pl.DeviceIdType.MESH))
        for snd, rcv in zip(sends, recvs):
            snd.wait_send()
            rcv.wait_recv()
        for loc in locs:
            loc.wait()

    any_spec = pl.BlockSpec(memory_space=pl.ANY)
    outs = _pcall(
        body, name=name, in_specs=[any_spec] * n, out_specs=[any_spec] * n,
        out_shape=[jax.ShapeDtypeStruct(s, a.dtype) for s, a in zip(shapes, arrs)],
        scratch_shapes=[pltpu.SemaphoreType.DMA((n * (N_DEV - 1),)), pltpu.SemaphoreType.DMA((n * (N_DEV - 1),)),
                        pltpu.SemaphoreType.DMA((n,))],
        compiler_params=pltpu.CompilerParams(has_side_effects=True),
    )(*arrs)
    return list(outs)


_HBM = pl.BlockSpec(memory_space=pltpu.HBM)
_SEM = pl.BlockSpec(memory_space=pltpu.SEMAPHORE)
_EFFECT = pltpu.SideEffectType.DATAFLOW_SIDE_EFFECTING


def _peer_copies(srcs, lands, send_sems, recv_sems, gather):
    x, y, c = lax.axis_index("x"), lax.axis_index("y"), lax.axis_index("c")
    me = 4 * x + 2 * y + c
    out = []
    for a, (src, land) in enumerate(zip(srcs, lands)):
        for k in range(1, N_DEV):
            peer, pid = _peer(k, x, y, c)
            s = a * (N_DEV - 1) + k - 1
            mk = lambda dst: pltpu.make_async_remote_copy(
                src_ref=src if gather else src.at[pid], dst_ref=dst, send_sem=send_sems.at[s], recv_sem=recv_sems.at[s],
                device_id=peer, device_id_type=pl.DeviceIdType.MESH)
            out.append((mk(land.at[me]), mk(land.at[pid])))
    return out


def exchange_start(arrs, gather, after, name):
    n = len(arrs)
    nsem = n * (N_DEV - 1)
    lands = [lax.empty(((N_DEV,) + a.shape) if gather else a.shape, a.dtype) for a in arrs]

    def body(*refs):
        srcs, land_refs = refs[:n], refs[n:2 * n]
        send_sems, recv_sems = refs[2 * n + 1], refs[2 * n + 2]
        token = refs[-1]
        for snd, _ in _peer_copies(srcs, land_refs, send_sems, recv_sems, gather):
            snd.start()
        token[...] = jnp.zeros_like(token)

    hbm = lambda a: pltpu.HBM(a.shape, a.dtype)
    outs = _pcall(
        body, name=name, in_specs=[_HBM] * (2 * n) + [pl.BlockSpec(memory_space=pl.ANY)],
        out_specs=[_SEM, _SEM] + [_HBM] * (2 * n) + [pl.BlockSpec(memory_space=pltpu.VMEM)],
        out_shape=[pltpu.SemaphoreType.DMA((nsem,)), pltpu.SemaphoreType.DMA((nsem,))] + [hbm(a) for a in arrs]
        + [hbm(a) for a in lands] + [jax.ShapeDtypeStruct((SUBLANE, LANE), F32)],
        input_output_aliases={i: i + 2 for i in range(2 * n)},
        compiler_params=pltpu.CompilerParams(has_side_effects=_EFFECT),
    )(*[pltpu.with_memory_space_constraint(a, pltpu.HBM) for a in list(arrs) + lands], after)
    return (outs[0], outs[1], list(outs[2:2 + n]), list(outs[2 + n:2 + 2 * n])), outs[-1]


def exchange_wait(handle, gather, after, name):
    send_sems, recv_sems, srcs, lands = handle
    n = len(srcs)

    def body(*refs):
        src_refs, land_refs = refs[:n], refs[n:2 * n]
        for snd, rcv in _peer_copies(src_refs, land_refs, refs[2 * n], refs[2 * n + 1], gather):
            snd.wait_send()
            rcv.wait_recv()

    hbm = lambda a: pltpu.HBM(a.shape, a.dtype)
    outs = _pcall(
        body, name=name, in_specs=[_HBM] * (2 * n) + [_SEM, _SEM, pl.BlockSpec(memory_space=pl.ANY)],
        out_specs=[_HBM] * (2 * n), out_shape=[hbm(a) for a in srcs] + [hbm(a) for a in lands],
        input_output_aliases={i: i for i in range(2 * n)},
        compiler_params=pltpu.CompilerParams(has_side_effects=_EFFECT),
    )(*srcs, *lands, send_sems, recv_sems, after)
    return list(outs[n:])


def _my_index():
    return 4 * lax.axis_index("x") + 2 * lax.axis_index("y") + lax.axis_index("c")


def adamw_sum(parts, w, m, v, name):
    L, R, C = w.shape
    tr = R
    for cand in (512, 352, 256, 128, 64, 48, 32, 16, 8):
        if R % cand == 0 and cand * C * 4 <= 2 * 1024 * 1024:
            tr = cand
            break
    c1 = 1.0 - ADAM_B1 ** ADAM_STEP
    c2 = 1.0 - ADAM_B2 ** ADAM_STEP

    def body(p_ref, w_ref, m_ref, v_ref, g_ref, d_ref, nm_ref, nv_ref):
        g = p_ref[0, 0].astype(F32)
        for k in range(1, N_DEV):
            g = g + p_ref[k, 0].astype(F32)
        m2 = ADAM_B1 * m_ref[0] + (1.0 - ADAM_B1) * g
        v2 = ADAM_B2 * v_ref[0] + (1.0 - ADAM_B2) * (g * g)
        g_ref[0] = g
        nm_ref[0] = m2
        nv_ref[0] = v2
        d_ref[0] = -ADAM_LR * ((m2 / c1) / (jnp.sqrt(v2 / c2) + ADAM_EPS) + ADAM_WD * w_ref[0])

    blk = pl.BlockSpec((1, tr, C), lambda l, i: (l, i, 0))
    out = jax.ShapeDtypeStruct((L, R, C), F32)
    return _pcall(
        body, name=name, grid=(L, R // tr),
        in_specs=[pl.BlockSpec((N_DEV, 1, tr, C), lambda l, i: (0, l, i, 0)), blk, blk, blk],
        out_specs=[blk] * 4, out_shape=[out] * 4, compiler_params=_params("parallel", "parallel"),
    )(parts, w, m, v)


def _cols_full(g):
    return jnp.transpose(g, (1, 0, 2)).reshape(g.shape[1], N_DEV * g.shape[2])


def _cols_shards(w):
    R = w.shape[0]
    return jnp.transpose(w.reshape(R, N_DEV, w.shape[1] // N_DEV), (1, 0, 2))


def _w_in_to_z(w):
    kr0 = Z_GATE
    gate0 = Z_GATE + MLA_ROPE
    pad = jnp.zeros((w.shape[0], Z_W - Z_KR - MLA_ROPE), w.dtype)
    return jnp.concatenate([w[:, :kr0], w[:, gate0:], w[:, kr0:gate0], pad], axis=1)


def _z_to_w_in(dw):
    return jnp.concatenate([dw[:, :Z_GATE], dw[:, Z_KR:Z_KR + MLA_ROPE], dw[:, Z_GATE:Z_KR]], axis=1)


def _block_diag(w):
    eye = jnp.eye(RG_BLOCKS, dtype=w.dtype)
    return (w[:, :, None, :] * eye[:, None, :, None]).reshape(D_MODEL, D_MODEL).astype(BF16)


def _diag_blocks(d):
    d4 = d.reshape(RG_BLOCKS, RG_BLOCK_W, RG_BLOCKS, RG_BLOCK_W)
    return jnp.stack([d4[n, :, n, :] for n in range(RG_BLOCKS)], axis=0)


def _uq_full(g):
    p = jnp.pad(g, ((0, 0), (0, 0), (0, 2 * HEAD - HEAD - MLA_ROPE)))
    return jnp.transpose(p, (1, 0, 2)).reshape(MLA_LORA, MLA_HEADS * 2 * HEAD)


def _uq_shards(dw):
    return jnp.transpose(dw.reshape(MLA_LORA, MLA_HEADS, 2 * HEAD), (1, 0, 2))[:, :, :HEAD + MLA_ROPE]


SMALL = ("ffn1_norm", "mix_norm", "conv_b", "rg_w_a", "rg_b_a", "rg_w_x", "rg_b_x", "rg_lambda", "mla_q_norm",
         "mla_kv_norm", "ffn2_norm", "final_norm")
BIG = ("ffn1_w_gate_up", "ffn1_w_down", "w_in", "conv_w", "mla_w_uq", "mla_w_ukv", "w_branch_a", "w_branch_b",
       "w_branch_c", "w_out", "ffn2_w_gate_up", "ffn2_w_down")
ROW_SHARDED = ("ffn1_w_down", "w_branch_a", "w_branch_b", "w_branch_c", "w_out", "ffn2_w_down")


def _full_weights(g):
    fw = {k: g[k].reshape(-1, g[k].shape[-1]) for k in ROW_SHARDED}
    fw["ffn1_w_gate_up"] = _cols_full(g["ffn1_w_gate_up"])
    fw["ffn2_w_gate_up"] = _cols_full(g["ffn2_w_gate_up"])
    fw["w_in"] = _w_in_to_z(_cols_full(g["w_in"]))
    fw["conv_w"] = _cols_full(g["conv_w"])
    fw["mla_w_uq"] = _uq_full(g["mla_w_uq"])
    fw["mla_w_ukv"] = _cols_full(g["mla_w_ukv"])
    return fw


def _grad_shards(dw):
    out = {k: dw[k].reshape(N_DEV, dw[k].shape[0] // N_DEV, dw[k].shape[1]) for k in ROW_SHARDED}
    out["ffn1_w_gate_up"] = _cols_shards(dw["ffn1_w_gate_up"])
    out["ffn2_w_gate_up"] = _cols_shards(dw["ffn2_w_gate_up"])
    out["w_in"] = _cols_shards(_z_to_w_in(dw["w_in"]))
    out["conv_w"] = _cols_shards(dw["conv_w"])
    out["mla_w_uq"] = _uq_shards(dw["mla_w_uq"])
    out["mla_w_ukv"] = _cols_shards(dw["mla_w_ukv"])
    return out


def ffn_fwd(x, norm, w_gu, w_d, tag):
    h = rms_fwd(x, norm, f"{tag}_rms")
    gu = matmul(h, w_gu, "nn", BF16, f"{tag}_gu")
    a = swiglu_fwd(gu, f"{tag}_act")
    y = matmul(a, w_d, "nn", F32, f"{tag}_down", scale=0.5, res=x, tk=1408)
    return y, (x, h, gu, a)


def ffn_bwd(dy, saved, norm, w_gu, w_d, tag):
    x, h, gu, a = saved
    da = matmul(dy, w_d, "nt", BF16, f"{tag}_dact", scale=0.5, tn=1408)
    dw_d = matmul(a, dy, "tn", BF16, f"{tag}_dwd", scale=0.5, tm=1408)
    dgu = swiglu_bwd(gu, da, f"{tag}_dgu")
    dw_gu = matmul(h, dgu, "tn", BF16, f"{tag}_dwgu")
    dh = matmul(dgu, w_gu, "nt", F32, f"{tag}_dh")
    dx, dnorm = rms_bwd(x, norm, dh, dy, f"{tag}_drms")
    return dx, dw_gu, dw_d, dnorm


def mixer_fwd(x, sp, fw, cs, sn, tag):
    h = rms_fwd(x, sp["mix_norm"], f"{tag}_rms")
    z = matmul(h, fw["w_in"], "nn", BF16, f"{tag}_in", tn=1280)
    wa, wx = _block_diag(sp["rg_w_a"]), _block_diag(sp["rg_w_x"])
    ya, hs = rglru_fwd(z, fw["conv_w"], sp["conv_b"], wa, wx, sp["rg_b_a"], sp["rg_b_x"], sp["rg_lambda"], f"{tag}_rg")
    yb, ltot = sb_fwd(z, f"{tag}_sb")
    cqn, ckvn, krope = mla_prep_fwd(z, sp["mla_q_norm"], sp["mla_kv_norm"], cs, sn, f"{tag}_mprep")
    q = q_rope(matmul(cqn, fw["mla_w_uq"], "nn", F32, f"{tag}_uq"), cs, sn, 1.0, f"{tag}_qrope")
    kv = matmul(ckvn, fw["mla_w_ukv"], "nn", BF16, f"{tag}_ukv")
    yc, lse = mla_fwd(q, kv, krope, f"{tag}_mla")
    pa = matmul(ya, fw["w_branch_a"], "nn", BF16, f"{tag}_pa")
    pb = matmul(yb, fw["w_branch_b"], "nn", BF16, f"{tag}_pb")
    pc = matmul(yc, fw["w_branch_c"], "nn", BF16, f"{tag}_pc")
    merged = merge_fwd(z, pa, pb, pc, f"{tag}_merge")
    y = matmul(merged, fw["w_out"], "nn", F32, f"{tag}_out", res=x)
    return y, (x, h, z, wa, wx, ya, hs, yb, ltot, cqn, ckvn, krope, q, kv, yc, lse, pa, pb, pc, merged)


def mixer_bwd(dy, saved, sp, fw, cs, sn, tag):
    x, h, z, wa, wx, ya, hs, yb, ltot, cqn, ckvn, krope, q, kv, yc, lse, pa, pb, pc, merged = saved
    S = x.shape[0]
    dw, ds = {}, {}
    dmerged = matmul(dy, fw["w_out"], "nt", BF16, f"{tag}_dmerged")
    dw["w_out"] = matmul(merged, dy, "tn", BF16, f"{tag}_dwout")
    dpa, dpb, dpc, dga, dgb, dgc = merge_bwd(z, pa, pb, pc, dmerged, f"{tag}_dmerge")
    dya = matmul(dpa, fw["w_branch_a"], "nt", BF16, f"{tag}_dya")
    dyb = matmul(dpb, fw["w_branch_b"], "nt", BF16, f"{tag}_dyb")
    dyc = matmul(dpc, fw["w_branch_c"], "nt", BF16, f"{tag}_dyc")
    dw["w_branch_a"] = matmul(ya, dpa, "tn", BF16, f"{tag}_dwa")
    dw["w_branch_b"] = matmul(yb, dpb, "tn", BF16, f"{tag}_dwb")
    dw["w_branch_c"] = matmul(yc, dpc, "tn", BF16, f"{tag}_dwc")
    drgx, drgg, dwa, dwx, dvec = rglru_bwd(z, hs, dya, fw["conv_w"], sp["conv_b"], wa, wx, sp["rg_b_a"], sp["rg_b_x"],
                                           sp["rg_lambda"], f"{tag}_drg")
    ds["rg_w_a"], ds["rg_w_x"] = _diag_blocks(dwa), _diag_blocks(dwx)
    ds["rg_b_a"], ds["rg_b_x"], ds["rg_lambda"], ds["conv_b"] = dvec[0], dvec[1], dvec[2], dvec[3]
    dw["conv_w"] = dvec[4:8]
    dsq, dsk, dsv = sb_bwd(z, ltot, dyb, f"{tag}_dsb")
    dq, dkv, dkr = mla_bwd(q, kv, krope, yc, lse, dyc, f"{tag}_dmla")
    dqp = q_rope(dq, cs, sn, -1.0, f"{tag}_dqrope")
    dw["mla_w_uq"] = matmul(cqn, dqp, "tn", BF16, f"{tag}_dwuq")
    dw["mla_w_ukv"] = matmul(ckvn, dkv, "tn", BF16, f"{tag}_dwukv")
    dcqn = matmul(dqp, fw["mla_w_uq"], "nt", F32, f"{tag}_dcqn")
    dckvn = matmul(dkv, fw["mla_w_ukv"], "nt", F32, f"{tag}_dckvn")
    dcq, dckv, dkrr, dqn, dkvn = mla_prep_bwd(z, sp["mla_q_norm"], sp["mla_kv_norm"], cs, sn, dcqn, dckvn, dkr, f"{tag}_dmprep")
    ds["mla_q_norm"], ds["mla_kv_norm"] = dqn[0], dkvn[0]
    dz = jnp.concatenate([drgx, drgg, dsq.astype(BF16), dsk.astype(BF16), dsv.astype(BF16), dcq, dckv, dga, dgb, dgc, dkrr,
                          jnp.zeros((S, Z_W - Z_KR - LANE), BF16)], axis=1)
    dw["w_in"] = matmul(h, dz, "tn", BF16, f"{tag}_dwin", tn=1280)
    dh = matmul(dz, fw["w_in"], "nt", F32, f"{tag}_dh", tk=1280)
    dx, dnorm = rms_bwd(x, sp["mix_norm"], dh, dy, f"{tag}_drms")
    ds["mix_norm"] = dnorm[0]
    return dx, dw, ds


def _rope_tables(positions):
    inv = ROPE_THETA ** (-jnp.arange(0, MLA_ROPE, 2, dtype=F32) / MLA_ROPE)
    ang = positions.astype(F32)[:, None] * inv
    zeros = jnp.zeros((positions.shape[0], LANE - MLA_ROPE), F32)
    cs = jnp.concatenate([jnp.cos(ang), jnp.cos(ang), zeros], axis=1)
    sn = jnp.concatenate([jnp.sin(ang), jnp.sin(ang), zeros], axis=1)
    return cs, sn


def local_step(x, positions, target, small, fetch, emit):
    L = small["ffn1_norm"].shape[0]
    cs, sn = _rope_tables(positions)
    row = lambda v: v.reshape(1, -1)
    saved = []
    for l in range(L):
        g, token = fetch(l, x)
        fw = _full_weights(g)
        sp = {k: small[k][l] for k in SMALL if k != "final_norm"}
        sp = {k: (v if v.ndim == 3 else row(v)) for k, v in sp.items()}
        if token is not None:
            sp["ffn1_norm"] = sp["ffn1_norm"] + token[0:1, 0:1]
        x, s1 = ffn_fwd(x, sp["ffn1_norm"], fw["ffn1_w_gate_up"], fw["ffn1_w_down"], f"l{l}_f1")
        x, s2 = mixer_fwd(x, sp, fw, cs, sn, f"l{l}_mx")
        x, s3 = ffn_fwd(x, sp["ffn2_norm"], fw["ffn2_w_gate_up"], fw["ffn2_w_down"], f"l{l}_f2")
        saved.append((fw, sp, s1, s2, s3))
    loss, dx, dfinal = loss_head(x, row(small["final_norm"]), target, "loss_head")
    small_grads = [None] * L
    for l in reversed(range(L)):
        fw, sp, s1, s2, s3 = saved[l]
        dx, dgu2, dd2, dn2 = ffn_bwd(dx, s3, sp["ffn2_norm"], fw["ffn2_w_gate_up"], fw["ffn2_w_down"], f"l{l}_f2")
        dx, dw, ds = mixer_bwd(dx, s2, sp, fw, cs, sn, f"l{l}_mx")
        dx, dgu1, dd1, dn1 = ffn_bwd(dx, s1, sp["ffn1_norm"], fw["ffn1_w_gate_up"], fw["ffn1_w_down"], f"l{l}_f1")
        dw.update(ffn1_w_gate_up=dgu1, ffn1_w_down=dd1, ffn2_w_gate_up=dgu2, ffn2_w_down=dd2)
        ds.update(ffn1_norm=dn1[0], ffn2_norm=dn2[0])
        small_grads[l] = ds
        dx = emit(l, _grad_shards(dw), dx)
    return loss[0, 0], dx, small_grads, dfinal[0]


def _pack_small(tree):
    flat = jnp.concatenate([tree[k].reshape(-1).astype(F32) for k in SMALL])
    rows = -(-flat.shape[0] // (SUBLANE * D_MODEL)) * SUBLANE
    return jnp.pad(flat, (0, rows * D_MODEL - flat.shape[0])).reshape(rows, D_MODEL)


def _unpack_small(buf, like):
    flat = buf.reshape(-1)
    out, off = {}, 0
    for k in SMALL:
        n = like[k].size
        out[k] = flat[off:off + n].reshape(like[k].shape)
        off += n
    return out


NAMES = ("ffn1_norm", "ffn1_w_gate_up", "ffn1_w_down", "mix_norm", "w_in", "conv_w", "conv_b", "rg_w_a", "rg_b_a", "rg_w_x",
         "rg_b_x", "rg_lambda", "mla_q_norm", "mla_w_uq", "mla_kv_norm", "mla_w_ukv", "w_branch_a", "w_branch_b",
         "w_branch_c", "w_out", "ffn2_norm", "ffn2_w_gate_up", "ffn2_w_down", "final_norm")


def kernel(x, positions, *rest):
    n = len(NAMES)
    w = dict(zip(NAMES, rest[:n]))
    target = rest[n]
    m = dict(zip(NAMES, rest[n + 1:2 * n + 1]))
    v = dict(zip(NAMES, rest[2 * n + 1:3 * n + 1]))
    L = w["ffn1_norm"].shape[0]
    me = _my_index()

    shards = [[w[k][l] if k == "conv_w" else w[k][l].astype(BF16) for k in BIG] for l in range(L)]
    pending = {0: exchange_start(shards[0], True, x, "gather_start_0")[0]}

    def fetch(l, after):
        landed = exchange_wait(pending.pop(l), True, after, f"gather_wait_{l}")
        token = None
        if l + 1 < L:
            pending[l + 1], token = exchange_start(shards[l + 1], True, landed[0], f"gather_start_{l + 1}")
        mine = [lax.dynamic_update_slice_in_dim(g, s[None], me, 0) for g, s in zip(landed, shards[l])]
        return dict(zip(BIG, mine)), token

    parts, flying = [None] * L, {}

    def land(l, after):
        handle, own = flying.pop(l)
        landed = exchange_wait(handle, False, after, f"scatter_wait_{l}")
        parts[l] = [lax.dynamic_update_slice_in_dim(g, o, me, 0) for g, o in zip(landed, own)]

    def emit(l, gshards, dx):
        send = [gshards[k] for k in BIG]
        own = [lax.dynamic_slice_in_dim(s, me, 1, 0) for s in send]
        if l + 1 < L:
            land(l + 1, dx)
        handle, token = exchange_start(send, False, parts[l + 1][0] if l + 1 < L else dx, f"scatter_start_{l}")
        flying[l] = (handle, own)
        return dx + token[0:1, 0:1]

    small = {k: w[k] for k in SMALL}
    loss, dx, small_grads, dfinal = local_step(x[0], positions[0], target[0], small, fetch, emit)
    loss = lax.psum(loss, ("x", "y", "c"))
    land(0, dx)

    sg = {k: jnp.stack([small_grads[l][k] for l in range(L)], axis=0) for k in SMALL if k != "final_norm"}
    sg["final_norm"] = dfinal
    small_parts = exchange([_pack_small(sg)], True, "gather_small_grads")[0]

    res = {}
    for i, k in enumerate(BIG):
        stacked = jnp.stack([parts[l][i] for l in range(L)], axis=1)
        res[k] = adamw_sum(stacked, w[k], m[k], v[k], f"adamw_{k}")
    packed = adamw_sum(small_parts[:, None], _pack_small(small)[None], _pack_small({k: m[k] for k in SMALL})[None],
                       _pack_small({k: v[k] for k in SMALL})[None], "adamw_small")
    unpacked = [_unpack_small(p[0], small) for p in packed]
    for k in SMALL:
        res[k] = tuple(u[k] for u in unpacked)

    outs = [loss, dx[None]]
    for i in range(4):
        outs += [res[k][i] for k in NAMES]
    return tuple(outs)
```

```python
import functools
import math

import jax
import jax.numpy as jnp
from jax import lax
from jax.experimental import pallas as pl
from jax.experimental.pallas import tpu as pltpu

F32 = jnp.float32
BF16 = jnp.bfloat16

N_DEV = 8
D_MODEL = 1024
D_FF = 2816
NORM_EPS = 1e-6
RG_BLOCKS = 16
RG_BLOCK_W = 64
RG_C = 8.0
SB_HEADS = 8
HEAD = 128
MLA_HEADS = 8
MLA_LORA = 256
MLA_ROPE = 64
ROPE_THETA = 10000.0
CHUNK = 64
SB_SCALE = HEAD ** -0.5
MLA_SCALE = (HEAD + MLA_ROPE) ** -0.5
N_IN = 8768

Z_RGX, Z_RGG, Z_Q, Z_K, Z_V, Z_CQ, Z_CKV, Z_GATE, Z_KR, Z_W = 0, 1024, 2048, 3072, 4096, 5120, 5376, 5632, 8704, 8960

ADAM_LR, ADAM_B1, ADAM_B2, ADAM_EPS, ADAM_WD, ADAM_STEP = 0.001, 0.9, 0.999, 1e-08, 0.01, 10

LANE = 128
SUBLANE = 8
VMEM_LIMIT = 48 * 1024 * 1024
NEG = -1e30


def _pcall(body, **kw):
    return pl.pallas_call(body, **kw)


def _params(*sem):
    return pltpu.CompilerParams(dimension_semantics=sem or None, vmem_limit_bytes=VMEM_LIMIT)


def _pick(dim, target):
    best = None
    t = LANE
    while t <= min(dim, target):
        if dim % t == 0:
            best = t
        t += LANE
    return best if best is not None else dim


def _sigmoid(x):
    return 1.0 / (1.0 + jnp.exp(-x))


def _gelu_and_grad(x):
    c = math.sqrt(2.0 / math.pi)
    inner = c * (x + 0.044715 * x * x * x)
    t = jnp.tanh(inner)
    val = 0.5 * x * (1.0 + t)
    grad = 0.5 * (1.0 + t) + 0.5 * x * (1.0 - t * t) * c * (1.0 + 3.0 * 0.044715 * x * x)
    return val, grad


def _neg_expm1(y):
    series = -y * (1.0 + y * (0.5 + y * (1.0 / 6.0 + y * (1.0 / 24.0))))
    return jnp.where(jnp.abs(y) < 0.02, series, 1.0 - jnp.exp(y))


def _dot(a, b, dims):
    return lax.dot_general(a, b, (dims, ((), ())), preferred_element_type=F32)


NN = ((1,), (0,))
NT = ((1,), (1,))
TN = ((0,), (0,))


def matmul(a, b, mode, out_dtype, name, scale=1.0, res=None, tm=1024, tn=1024, tk=1024):
    if mode == "nn":
        (M, K), N = a.shape, b.shape[1]
    elif mode == "nt":
        (M, K), N = a.shape, b.shape[0]
    else:
        (K, M), N = a.shape, b.shape[1]
    tm, tn, tk = _pick(M, tm), _pick(N, tn), _pick(K, tk)
    nk = K // tk
    dims = {"nn": NN, "nt": NT, "tn": TN}[mode]

    def body(*refs):
        if res is None:
            a_ref, b_ref, o_ref, acc = refs
        else:
            a_ref, b_ref, r_ref, o_ref, acc = refs
        k = pl.program_id(2)

        @pl.when(k == 0)
        def _():
            acc[...] = jnp.zeros_like(acc)

        acc[...] += _dot(a_ref[...].astype(BF16), b_ref[...].astype(BF16), dims)

        @pl.when(k == nk - 1)
        def _():
            r = acc[...] * scale
            if res is not None:
                r = r + r_ref[...]
            o_ref[...] = r.astype(out_dtype)

    a_spec = pl.BlockSpec((tk, tm), lambda i, j, k: (k, i)) if mode == "tn" else pl.BlockSpec((tm, tk), lambda i, j, k: (i, k))
    b_spec = pl.BlockSpec((tn, tk), lambda i, j, k: (j, k)) if mode == "nt" else pl.BlockSpec((tk, tn), lambda i, j, k: (k, j))
    o_spec = pl.BlockSpec((tm, tn), lambda i, j, k: (i, j))
    in_specs = [a_spec, b_spec] + ([o_spec] if res is not None else [])
    args = (a, b) + ((res,) if res is not None else ())
    return _pcall(
        body, name=name, grid=(M // tm, N // tn, nk), in_specs=in_specs, out_specs=o_spec,
        out_shape=jax.ShapeDtypeStruct((M, N), out_dtype), scratch_shapes=[pltpu.VMEM((tm, tn), F32)],
        compiler_params=_params("parallel", "parallel", "arbitrary"),
    )(*args)


def rms_fwd(x, g, name, col=0):
    S, D = x.shape[0], g.shape[1]
    tr = _pick(S, 512)

    def body(x_ref, g_ref, o_ref):
        xv = x_ref[...].astype(F32)
        r = lax.rsqrt(jnp.mean(xv * xv, axis=-1, keepdims=True) + NORM_EPS)
        o_ref[...] = (xv * r * g_ref[...]).astype(BF16)

    return _pcall(
        body, name=name, grid=(S // tr,),
        in_specs=[pl.BlockSpec((tr, D), lambda i: (i, col)), pl.BlockSpec((1, D), lambda i: (0, 0))],
        out_specs=pl.BlockSpec((tr, D), lambda i: (i, 0)),
        out_shape=jax.ShapeDtypeStruct((S, D), BF16), compiler_params=_params("parallel"),
    )(x, g)


def _rms_bwd_math(xv, g, dh):
    r = lax.rsqrt(jnp.mean(xv * xv, axis=-1, keepdims=True) + NORM_EPS)
    xhat = xv * r
    dxhat = dh * g
    dx = r * (dxhat - xhat * jnp.mean(dxhat * xhat, axis=-1, keepdims=True))
    dg = jnp.sum(dh * xhat, axis=0, keepdims=True)
    return dx, dg


def rms_bwd(x, g, dh, dres, name):
    S, D = x.shape
    tr = _pick(S, 512)

    def body(x_ref, g_ref, dh_ref, dr_ref, dx_ref, dg_ref):
        dx, dg = _rms_bwd_math(x_ref[...], g_ref[...], dh_ref[...])
        dx_ref[...] = dx + dr_ref[...]

        @pl.when(pl.program_id(0) == 0)
        def _():
            dg_ref[...] = jnp.zeros_like(dg_ref)

        dg_ref[...] += dg

    row = pl.BlockSpec((tr, D), lambda i: (i, 0))
    vec = pl.BlockSpec((1, D), lambda i: (0, 0))
    return _pcall(
        body, name=name, grid=(S // tr,), in_specs=[row, vec, row, row], out_specs=[row, vec],
        out_shape=[jax.ShapeDtypeStruct((S, D), F32), jax.ShapeDtypeStruct((1, D), F32)],
        compiler_params=_params("arbitrary"),
    )(x, g, dh, dres)


def swiglu_fwd(gu, name):
    S = gu.shape[0]
    tr, tc = _pick(S, 512), D_FF // 2
    nc = D_FF // tc

    def body(g_ref, u_ref, o_ref):
        gv = g_ref[...].astype(F32)
        o_ref[...] = (gv * _sigmoid(gv) * u_ref[...].astype(F32)).astype(BF16)

    return _pcall(
        body, name=name, grid=(S // tr, nc),
        in_specs=[pl.BlockSpec((tr, tc), lambda i, j: (i, j)), pl.BlockSpec((tr, tc), lambda i, j: (i, j + nc))],
        out_specs=pl.BlockSpec((tr, tc), lambda i, j: (i, j)),
        out_shape=jax.ShapeDtypeStruct((S, D_FF), BF16), compiler_params=_params("parallel", "parallel"),
    )(gu, gu)


def swiglu_bwd(gu, da, name):
    S = gu.shape[0]
    tr, tc = _pick(S, 512), D_FF // 2
    nc = D_FF // tc

    def body(g_ref, u_ref, da_ref, o_ref):
        gv = g_ref[...].astype(F32)
        uv = u_ref[...].astype(F32)
        dav = da_ref[...].astype(F32)
        sg = _sigmoid(gv)
        half = pl.program_id(1)

        @pl.when(half == 0)
        def _():
            o_ref[...] = (dav * uv * sg * (1.0 + gv * (1.0 - sg))).astype(BF16)

        @pl.when(half == 1)
        def _():
            o_ref[...] = (dav * gv * sg).astype(BF16)

    return _pcall(
        body, name=name, grid=(S // tr, 2, nc),
        in_specs=[pl.BlockSpec((tr, tc), lambda i, h, j: (i, j)), pl.BlockSpec((tr, tc), lambda i, h, j: (i, j + nc)),
                  pl.BlockSpec((tr, tc), lambda i, h, j: (i, j))],
        out_specs=pl.BlockSpec((tr, tc), lambda i, h, j: (i, h * nc + j)),
        out_shape=jax.ShapeDtypeStruct((S, 2 * D_FF), BF16), compiler_params=_params("parallel", "parallel", "parallel"),
    )(gu, gu, da)


def _conv_taps(xpad, T, cw, cb):
    u = cb + cw[3:4, :] * xpad[pl.ds(8, T), :]
    for tap in range(3):
        u = u + cw[tap:tap + 1, :] * xpad[pl.ds(5 + tap, T), :]
    return u


def _rg_gates(u, wa_ref, wx_ref, ba, bx, lam):
    ub = u.astype(BF16)
    r = _sigmoid(_dot(ub, wa_ref[...], NN) + ba)
    ig = _sigmoid(_dot(ub, wx_ref[...], NN) + bx)
    nlam = -lam
    clam = -RG_C * (jnp.maximum(nlam, 0.0) + jnp.log(1.0 + jnp.exp(-jnp.abs(nlam))))
    la = clam * r
    return r, ig, clam, la


def rglru_fwd(z, cw, cb, wa, wx, ba, bx, lam, name):
    S, D = z.shape[0], D_MODEL
    T = _pick(S, 256)

    def body(x_ref, g_ref, cw_ref, cb_ref, wa_ref, wx_ref, ba_ref, bx_ref, lam_ref, y_ref, h_ref, xpad, a_s, b_s, hst):
        @pl.when(pl.program_id(0) == 0)
        def _():
            xpad[pl.ds(0, 8), :] = jnp.zeros((8, D), F32)
            hst[...] = jnp.zeros_like(hst)

        xpad[pl.ds(8, T), :] = x_ref[...].astype(F32)
        u = _conv_taps(xpad, T, cw_ref[...], cb_ref[...])
        xpad[pl.ds(0, 8), :] = xpad[pl.ds(T, 8), :]
        r, ig, clam, la = _rg_gates(u, wa_ref, wx_ref, ba_ref[...], bx_ref[...], lam_ref[...])
        a_s[...] = jnp.exp(la)
        b_s[...] = jnp.sqrt(_neg_expm1(2.0 * la)) * (ig * u)

        def tile(j, h):
            r0 = pl.multiple_of(j * 8, 8)
            av = a_s[pl.ds(r0, 8), :]
            bv = b_s[pl.ds(r0, 8), :]
            rows = []
            for k in range(8):
                h = av[k:k + 1, :] * h + bv[k:k + 1, :]
                rows.append(h)
            h_ref[pl.ds(r0, 8), :] = jnp.concatenate(rows, axis=0)
            return h

        hst[...] = lax.fori_loop(0, T // 8, tile, hst[...])
        gel, _ = _gelu_and_grad(g_ref[...].astype(F32))
        y_ref[...] = (h_ref[...] * gel).astype(BF16)

    blk = lambda c: pl.BlockSpec((T, D), lambda i: (i, c))
    vec = pl.BlockSpec((1, D), lambda i: (0, 0))
    full = lambda r: pl.BlockSpec((r, D), lambda i: (0, 0))
    return _pcall(
        body, name=name, grid=(S // T,),
        in_specs=[blk(0), blk(1), full(4), vec, full(D), full(D), vec, vec, vec],
        out_specs=[blk(0), blk(0)],
        out_shape=[jax.ShapeDtypeStruct((S, D), BF16), jax.ShapeDtypeStruct((S, D), F32)],
        scratch_shapes=[pltpu.VMEM((T + 8, D), F32), pltpu.VMEM((T, D), F32), pltpu.VMEM((T, D), F32), pltpu.VMEM((1, D), F32)],
        compiler_params=_params("arbitrary"),
    )(z, z, cw, cb, wa, wx, ba, bx, lam)


def rglru_bwd(z, hs, dy, cw, cb, wa, wx, ba, bx, lam, name):
    S, D = z.shape[0], D_MODEL
    T = _pick(S, 256)
    nb = S // T
    t8 = T // 8

    def body(x_ref, xp_ref, g_ref, h_ref, hp_ref, dy_ref, cw_ref, cb_ref, wa_ref, wx_ref, ba_ref, bx_ref, lam_ref,
             dx_ref, dg_ref, dwa_ref, dwx_ref, dvec_ref, xpad, hpad, dupad, a_s, d_s, carry):
        i = pl.program_id(0)
        first_block = i == nb - 1

        @pl.when(i == 0)
        def _():
            dwa_ref[...] = jnp.zeros_like(dwa_ref)
            dwx_ref[...] = jnp.zeros_like(dwx_ref)
            dvec_ref[...] = jnp.zeros_like(dvec_ref)
            carry[...] = jnp.zeros_like(carry)
            dupad[pl.ds(T, 8), :] = jnp.zeros((8, D), F32)

        keep = jnp.where(first_block, 0.0, 1.0)
        xpad[pl.ds(0, 8), :] = xp_ref[...].astype(F32) * keep
        xpad[pl.ds(8, T), :] = x_ref[...].astype(F32)
        hpad[pl.ds(0, 8), :] = hp_ref[...] * keep
        hpad[pl.ds(8, T), :] = h_ref[...]
        cwv = cw_ref[...]
        u = _conv_taps(xpad, T, cwv, cb_ref[...])
        r, ig, clam, la = _rg_gates(u, wa_ref, wx_ref, ba_ref[...], bx_ref[...], lam_ref[...])
        a = jnp.exp(la)
        a_s[...] = a
        gv = g_ref[...].astype(F32)
        gel, dgel = _gelu_and_grad(gv)
        dyv = dy_ref[...].astype(F32)
        d_s[...] = dyv * gel
        dg_ref[...] = (dyv * h_ref[...] * dgel).astype(BF16)

        def tile(j, c):
            r0 = pl.multiple_of((t8 - 1 - j) * 8, 8)
            av = a_s[pl.ds(r0, 8), :]
            dv = d_s[pl.ds(r0, 8), :]
            rows = [None] * 8
            for k in range(7, -1, -1):
                d = dv[k:k + 1, :] + c
                rows[k] = d
                c = av[k:k + 1, :] * d
            d_s[pl.ds(r0, 8), :] = jnp.concatenate(rows, axis=0)
            return c

        carry[...] = lax.fori_loop(0, t8, tile, carry[...])
        dht = d_s[...]
        hprev = hpad[pl.ds(7, T), :]
        w = _neg_expm1(2.0 * la)
        s = jnp.sqrt(w)
        e2 = 1.0 - w
        d_iu = dht * s
        dla = dht * hprev * a - dht * (ig * u) * e2 / s
        dpr = (dla * clam * r * (1.0 - r))
        dpi = (d_iu * u * ig * (1.0 - ig))
        dprb, dpib, ub = dpr.astype(BF16), dpi.astype(BF16), u.astype(BF16)
        du = d_iu * ig + _dot(dprb, wa_ref[...], NT) + _dot(dpib, wx_ref[...], NT)
        dwa_ref[...] += _dot(ub, dprb, TN)
        dwx_ref[...] += _dot(ub, dpib, TN)
        dvec_ref[0:1, :] += jnp.sum(dpr, axis=0, keepdims=True)
        dvec_ref[1:2, :] += jnp.sum(dpi, axis=0, keepdims=True)
        dvec_ref[2:3, :] += jnp.sum(dla * r, axis=0, keepdims=True)
        dvec_ref[3:4, :] += jnp.sum(du, axis=0, keepdims=True)
        for tap in range(4):
            dvec_ref[4 + tap:5 + tap, :] += jnp.sum(du * xpad[pl.ds(5 + tap, T), :], axis=0, keepdims=True)
        dupad[pl.ds(0, T), :] = du
        dx = cwv[3:4, :] * du
        for tap in range(3):
            dx = dx + cwv[tap:tap + 1, :] * dupad[pl.ds(3 - tap, T), :]
        dx_ref[...] = dx.astype(BF16)
        dupad[pl.ds(T, 8), :] = dupad[pl.ds(0, 8), :]

        @pl.when(first_block)
        def _():
            dvec_ref[2:3, :] = dvec_ref[2:3, :] * (RG_C * _sigmoid(-lam_ref[...]))

    rev = lambda c: pl.BlockSpec((T, D), lambda i: (nb - 1 - i, c))
    prev = lambda c: pl.BlockSpec((8, D), lambda i: (jnp.maximum((nb - 1 - i) * t8 - 1, 0), c))
    vec = pl.BlockSpec((1, D), lambda i: (0, 0))
    full = lambda r: pl.BlockSpec((r, D), lambda i: (0, 0))
    return _pcall(
        body, name=name, grid=(nb,),
        in_specs=[rev(0), prev(0), rev(1), rev(0), prev(0), rev(0), full(4), vec, full(D), full(D), vec, vec, vec],
        out_specs=[rev(0), rev(0), full(D), full(D), full(8)],
        out_shape=[jax.ShapeDtypeStruct((S, D), BF16), jax.ShapeDtypeStruct((S, D), BF16),
                   jax.ShapeDtypeStruct((D, D), F32), jax.ShapeDtypeStruct((D, D), F32), jax.ShapeDtypeStruct((8, D), F32)],
        scratch_shapes=[pltpu.VMEM((T + 8, D), F32), pltpu.VMEM((T + 8, D), F32), pltpu.VMEM((T + 8, D), F32),
                        pltpu.VMEM((T, D), F32), pltpu.VMEM((T, D), F32), pltpu.VMEM((1, D), F32)],
        compiler_params=_params("arbitrary"),
    )(z, z, z, hs, hs, dy, cw, cb, wa, wx, ba, bx, lam)


def _tri(n, kind):
    j = lax.broadcasted_iota(jnp.int32, (n, n), 0)
    s = lax.broadcasted_iota(jnp.int32, (n, n), 1)
    m = {"gt": j > s, "le": j <= s, "lt": j < s}[kind]
    return jnp.where(m, 1.0, 0.0).astype(BF16)


def _dot2(x, tri):
    hi = x.astype(BF16)
    lo = (x - hi.astype(F32)).astype(BF16)
    return _dot(hi, tri, NN) + _dot(lo, tri, NN)


SB_TK = 128

def _sb_logits(q, kblk, q0, k0, tq, masked):
    z = _dot(q, kblk, NT) * SB_SCALE
    sp = jnp.maximum(z, 0.0) + jnp.log(1.0 + jnp.exp(-jnp.abs(z)))
    lkeep = -sp
    mask = None
    if masked:
        tpos = q0 + lax.broadcasted_iota(jnp.int32, (tq, SB_TK), 0)
        spos = k0 + lax.broadcasted_iota(jnp.int32, (tq, SB_TK), 1)
        mask = spos < tpos
        lkeep = jnp.where(mask, lkeep, 0.0)
    return mask, lkeep, z - sp


def sb_fwd(z, name):
    S = z.shape[0]
    tq, tk = _pick(S, 512), SB_TK
    nd = tq // tk
    U = min(4, nd)
    qc, kc, vc = Z_Q // HEAD, Z_K // HEAD, Z_V // HEAD

    def body(q_ref, k_ref, v_ref, o_ref, lt_ref, acc, run):
        qi = pl.program_id(1)
        q0 = qi * tq
        q = q_ref[...]
        tri = _tri(tk, "gt")
        acc[...] = jnp.zeros_like(acc)
        run[...] = jnp.zeros_like(run)

        def group(k0s, masked):
            parts = [(k0,) + _sb_logits(q, k_ref[pl.ds(k0, tk), :], q0, k0, tq, masked) for k0 in k0s]
            cums = [_dot2(p[2], tri) for p in parts]
            r, a = run[...], acc[...]
            for (k0, mask, lkeep, lbeta), cum in zip(parts, cums):
                w = jnp.exp(lbeta + cum + r)
                if masked:
                    w = jnp.where(mask, w, 0.0)
                a = a + _dot(w.astype(BF16), v_ref[pl.ds(k0, tk), :], NN)
                r = r + jnp.sum(lkeep, axis=1, keepdims=True)
            acc[...] = a
            run[...] = r

        for g in range(nd // U):
            group([pl.multiple_of(q0 + (nd - 1 - g * U - u) * tk, tk) for u in range(U)], True)

        def step(i, c):
            base = qi * nd - 1 - i * U
            group([pl.multiple_of((base - u) * tk, tk) for u in range(U)], False)
            return c

        lax.fori_loop(0, qi * nd // U, step, 0)
        o_ref[...] = acc[...].astype(BF16)
        lt_ref[0] = run[...]

    return _pcall(
        body, name=name, grid=(SB_HEADS, S // tq),
        in_specs=[pl.BlockSpec((tq, HEAD), lambda h, i: (i, qc + h)), pl.BlockSpec((S, HEAD), lambda h, i: (0, kc + h)),
                  pl.BlockSpec((S, HEAD), lambda h, i: (0, vc + h))],
        out_specs=[pl.BlockSpec((tq, HEAD), lambda h, i: (i, h)), pl.BlockSpec((1, tq, 1), lambda h, i: (h, i, 0))],
        out_shape=[jax.ShapeDtypeStruct((S, SB_HEADS * HEAD), BF16), jax.ShapeDtypeStruct((SB_HEADS, S, 1), F32)],
        scratch_shapes=[pltpu.VMEM((tq, HEAD), F32), pltpu.VMEM((tq, 1), F32)],
        compiler_params=_params("parallel", "parallel"),
    )(z, z, z)


def sb_bwd(z, ltot, dy, name):
    S = z.shape[0]
    tq, tk = _pick(S, 512), SB_TK
    nd = tq // tk
    U = min(4, nd)
    nkb = S // tk
    qc, kc, vc = Z_Q // HEAD, Z_K // HEAD, Z_V // HEAD

    def body(q_ref, k_ref, v_ref, lt_ref, do_ref, dq_ref, dk_ref, dv_ref, dq_s, run_l, run_g, dkT, dvT):
        qi = pl.program_id(1)
        q0 = qi * tq

        @pl.when(qi == 0)
        def _():
            dkT[...] = jnp.zeros_like(dkT)
            dvT[...] = jnp.zeros_like(dvT)

        q = q_ref[...]
        do = do_ref[...].astype(BF16)
        qT, doT = q.T, do.T
        ltv = lt_ref[0]
        tri_le, tri_lt = _tri(tk, "le"), _tri(tk, "lt")
        dq_s[...] = jnp.zeros_like(dq_s)
        run_l[...] = jnp.zeros_like(run_l)
        run_g[...] = jnp.zeros_like(run_g)

        def group(k0s, masked):
            parts = []
            for k0 in k0s:
                kblk = k_ref[pl.ds(k0, tk), :]
                mask, lkeep, lbeta = _sb_logits(q, kblk, q0, k0, tq, masked)
                parts.append((k0, kblk, mask, lkeep, lbeta, _dot(do, v_ref[pl.ds(k0, tk), :], NT)))
            pres = [_dot2(p[3], tri_le) for p in parts]
            rl = run_l[...]
            ws = []
            for (k0, kblk, mask, lkeep, lbeta, dw), pre in zip(parts, pres):
                w = jnp.exp(lbeta + (ltv - (pre + rl)))
                if masked:
                    w = jnp.where(mask, w, 0.0)
                ws.append((w, w * dw))
                rl = rl + jnp.sum(lkeep, axis=1, keepdims=True)
            run_l[...] = rl
            gpres = [_dot2(g, tri_lt) for _, g in ws]
            rg, dq = run_g[...], dq_s[...]
            for (k0, kblk, mask, lkeep, lbeta, dw), (w, g), gpre in zip(parts, ws, gpres):
                dz = (g * jnp.exp(lkeep) - jnp.exp(lbeta) * (gpre + rg)) * SB_SCALE
                if masked:
                    dz = jnp.where(mask, dz, 0.0)
                dz = dz.astype(BF16)
                dq = dq + _dot(dz, kblk, NN)
                kb = k0 // tk
                dkT[kb] += _dot(qT, dz, NN)
                dvT[kb] += _dot(doT, w.astype(BF16), NN)
                rg = rg + jnp.sum(g, axis=1, keepdims=True)
            run_g[...] = rg
            dq_s[...] = dq

        def step(i, c):
            group([pl.multiple_of((i * U + u) * tk, tk) for u in range(U)], False)
            return c

        lax.fori_loop(0, qi * nd // U, step, 0)
        for g in range(nd // U):
            group([pl.multiple_of(q0 + (g * U + u) * tk, tk) for u in range(U)], True)
        dq_ref[...] = dq_s[...]

        @pl.when(qi == pl.num_programs(1) - 1)
        def _():
            def flush(kb, c):
                r0 = pl.multiple_of(kb * tk, tk)
                dk_ref[pl.ds(r0, tk), :] = dkT[kb].T
                dv_ref[pl.ds(r0, tk), :] = dvT[kb].T
                return c

            lax.fori_loop(0, nkb, flush, 0)

    qblk = lambda c: pl.BlockSpec((tq, HEAD), lambda h, i: (i, c + h))
    kfull = lambda c: pl.BlockSpec((S, HEAD), lambda h, i: (0, c + h))
    out = jax.ShapeDtypeStruct((S, SB_HEADS * HEAD), F32)
    return _pcall(
        body, name=name, grid=(SB_HEADS, S // tq),
        in_specs=[qblk(qc), kfull(kc), kfull(vc), pl.BlockSpec((1, tq, 1), lambda h, i: (h, i, 0)), qblk(0)],
        out_specs=[qblk(0), kfull(0), kfull(0)], out_shape=[out, out, out],
        scratch_shapes=[pltpu.VMEM((tq, HEAD), F32), pltpu.VMEM((tq, 1), F32), pltpu.VMEM((tq, 1), F32),
                        pltpu.VMEM((nkb, HEAD, tk), F32), pltpu.VMEM((nkb, HEAD, tk), F32)],
        compiler_params=_params("arbitrary", "arbitrary"),
    )(z, z, z, ltot, dy)


def _rope(x, cs, sn, sign):
    lane = lax.broadcasted_iota(jnp.int32, x.shape, 1)
    swapped = jnp.where(lane < MLA_ROPE // 2, -pltpu.roll(x, LANE - MLA_ROPE // 2, 1), pltpu.roll(x, MLA_ROPE // 2, 1))
    return x * cs + sign * swapped * sn


def mla_prep_fwd(z, qn, kvn, cs, sn, name):
    S = z.shape[0]
    tr = _pick(S, 512)

    def body(cq_ref, ckv_ref, kr_ref, qn_ref, kvn_ref, cs_ref, sn_ref, oq_ref, okv_ref, okr_ref):
        for src, g, dst in ((cq_ref, qn_ref, oq_ref), (ckv_ref, kvn_ref, okv_ref)):
            xv = src[...].astype(F32)
            r = lax.rsqrt(jnp.mean(xv * xv, axis=-1, keepdims=True) + NORM_EPS)
            dst[...] = (xv * r * g[...]).astype(BF16)
        okr_ref[...] = _rope(kr_ref[...].astype(F32), cs_ref[...], sn_ref[...], 1.0).astype(BF16)

    lora = lambda c: pl.BlockSpec((tr, MLA_LORA), lambda i: (i, c))
    tile = lambda c: pl.BlockSpec((tr, LANE), lambda i: (i, c))
    vec = pl.BlockSpec((1, MLA_LORA), lambda i: (0, 0))
    return _pcall(
        body, name=name, grid=(S // tr,),
        in_specs=[lora(Z_CQ // MLA_LORA), lora(Z_CKV // MLA_LORA), tile(Z_KR // LANE), vec, vec, tile(0), tile(0)],
        out_specs=[lora(0), lora(0), tile(0)],
        out_shape=[jax.ShapeDtypeStruct((S, MLA_LORA), BF16), jax.ShapeDtypeStruct((S, MLA_LORA), BF16),
                   jax.ShapeDtypeStruct((S, LANE), BF16)],
        compiler_params=_params("parallel"),
    )(z, z, z, qn, kvn, cs, sn)


def mla_prep_bwd(z, qn, kvn, cs, sn, dcqn, dckvn, dkrope, name):
    S = z.shape[0]
    tr = _pick(S, 512)

    def body(cq_ref, ckv_ref, qn_ref, kvn_ref, cs_ref, sn_ref, dq_ref, dkv_ref, dkr_ref, oq_ref, okv_ref, okr_ref, gq_ref, gkv_ref):
        @pl.when(pl.program_id(0) == 0)
        def _():
            gq_ref[...] = jnp.zeros_like(gq_ref)
            gkv_ref[...] = jnp.zeros_like(gkv_ref)

        for src, g, dh, dst, gacc in ((cq_ref, qn_ref, dq_ref, oq_ref, gq_ref), (ckv_ref, kvn_ref, dkv_ref, okv_ref, gkv_ref)):
            dx, dg = _rms_bwd_math(src[...].astype(F32), g[...], dh[...])
            dst[...] = dx.astype(BF16)
            gacc[...] += dg
        okr_ref[...] = _rope(dkr_ref[...], cs_ref[...], sn_ref[...], -1.0).astype(BF16)

    lora = lambda c: pl.BlockSpec((tr, MLA_LORA), lambda i: (i, c))
    tile = lambda c: pl.BlockSpec((tr, LANE), lambda i: (i, c))
    vec = pl.BlockSpec((1, MLA_LORA), lambda i: (0, 0))
    return _pcall(
        body, name=name, grid=(S // tr,),
        in_specs=[lora(Z_CQ // MLA_LORA), lora(Z_CKV // MLA_LORA), vec, vec, tile(0), tile(0), lora(0), lora(0), tile(0)],
        out_specs=[lora(0), lora(0), tile(0), vec, vec],
        out_shape=[jax.ShapeDtypeStruct((S, MLA_LORA), BF16), jax.ShapeDtypeStruct((S, MLA_LORA), BF16),
                   jax.ShapeDtypeStruct((S, LANE), BF16), jax.ShapeDtypeStruct((1, MLA_LORA), F32), jax.ShapeDtypeStruct((1, MLA_LORA), F32)],
        compiler_params=_params("arbitrary"),
    )(z, z, qn, kvn, cs, sn, dcqn, dckvn, dkrope)


def q_rope(q, cs, sn, sign, name):
    S = q.shape[0]
    tr = _pick(S, 512)

    def body(q_ref, cs_ref, sn_ref, o_ref):
        o_ref[:, 0:LANE] = q_ref[:, 0:LANE].astype(BF16)
        o_ref[:, LANE:2 * LANE] = _rope(q_ref[:, LANE:2 * LANE], cs_ref[...], sn_ref[...], sign).astype(BF16)

    blk = pl.BlockSpec((tr, 2 * LANE), lambda i, h: (i, h))
    tile = pl.BlockSpec((tr, LANE), lambda i, h: (i, 0))
    return _pcall(
        body, name=name, grid=(S // tr, MLA_HEADS), in_specs=[blk, tile, tile], out_specs=blk,
        out_shape=jax.ShapeDtypeStruct(q.shape, BF16), compiler_params=_params("parallel", "parallel"),
    )(q, cs, sn)


def _chunk_mask(rows, cols):
    tch = lax.broadcasted_iota(jnp.int32, (rows, cols), 0) // CHUNK
    sch = lax.broadcasted_iota(jnp.int32, (rows, cols), 1) // CHUNK
    return sch <= tch


def _fill_kcat(kcat, kv_ref, kr_ref):
    kcat[:, 0:HEAD] = kv_ref[:, 0:HEAD]
    kcat[:, HEAD:2 * HEAD] = kr_ref[...]


def mla_fwd(q, kv, kr, name):
    S = q.shape[0]
    t = _pick(S, 512)

    def body(q_ref, kv_ref, kr_ref, o_ref, lse_ref, kcat, m_s, l_s, acc):
        qi = pl.program_id(1)

        @pl.when(qi == 0)
        def _():
            _fill_kcat(kcat, kv_ref, kr_ref)

        q = q_ref[...]
        m_s[...] = jnp.full_like(m_s, NEG)
        l_s[...] = jnp.zeros_like(l_s)
        acc[...] = jnp.zeros_like(acc)

        def block(k0, width, masked):
            s = _dot(q, kcat[pl.ds(k0, width), :], NT) * MLA_SCALE
            if masked:
                s = jnp.where(_chunk_mask(t, width), s, NEG)
            m = m_s[...]
            m2 = jnp.maximum(m, jnp.max(s, axis=1, keepdims=True))
            p = jnp.exp(s - m2)
            alpha = jnp.exp(m - m2)
            l_s[...] = alpha * l_s[...] + jnp.sum(p, axis=1, keepdims=True)
            acc[...] = alpha * acc[...] + _dot(p.astype(BF16), kv_ref[pl.ds(k0, width), HEAD:2 * HEAD], NN)
            m_s[...] = m2

        if S >= 2 * t:
            def step(i, c):
                block(pl.multiple_of(i * 2 * t, 2 * t), 2 * t, False)
                return c

            lax.fori_loop(0, qi // 2, step, 0)

            @pl.when(qi % 2 == 1)
            def _():
                block(pl.multiple_of((qi - 1) * t, t), t, False)

        block(pl.multiple_of(qi * t, t), t, True)
        o_ref[...] = (acc[...] / l_s[...]).astype(BF16)
        lse_ref[0] = m_s[...] + jnp.log(l_s[...])

    return _pcall(
        body, name=name, grid=(MLA_HEADS, S // t),
        in_specs=[pl.BlockSpec((t, 2 * HEAD), lambda h, i: (i, h)), pl.BlockSpec((S, 2 * HEAD), lambda h, i: (0, h)),
                  pl.BlockSpec((S, LANE), lambda h, i: (0, 0))],
        out_specs=[pl.BlockSpec((t, HEAD), lambda h, i: (i, h)), pl.BlockSpec((1, t, 1), lambda h, i: (h, i, 0))],
        out_shape=[jax.ShapeDtypeStruct((S, MLA_HEADS * HEAD), BF16), jax.ShapeDtypeStruct((MLA_HEADS, S, 1), F32)],
        scratch_shapes=[pltpu.VMEM((S, 2 * HEAD), BF16), pltpu.VMEM((t, 1), F32), pltpu.VMEM((t, 1), F32), pltpu.VMEM((t, HEAD), F32)],
        compiler_params=_params("arbitrary", "arbitrary"),
    )(q, kv, kr)


def mla_bwd(q, kv, kr, o, lse, do, name):
    S = q.shape[0]
    t = _pick(S, 512)
    nkb = S // t

    def body(q_ref, kv_ref, kr_ref, o_ref, lse_ref, do_ref, dq_ref, dkv_ref, dkr_ref, kcat, dq_s, dkT, dvT, dkrT):
        h, qi = pl.program_id(0), pl.program_id(1)

        @pl.when(qi == 0)
        def _():
            _fill_kcat(kcat, kv_ref, kr_ref)
            dkT[...] = jnp.zeros_like(dkT)
            dvT[...] = jnp.zeros_like(dvT)

        @pl.when((qi == 0) & (h == 0))
        def _():
            dkrT[...] = jnp.zeros_like(dkrT)

        q = q_ref[...]
        dov = do_ref[...].astype(F32)
        dob = dov.astype(BF16)
        qT, doT = q.T, dob.T
        delta = jnp.sum(dov * o_ref[...].astype(F32), axis=1, keepdims=True)
        lsev = lse_ref[0]
        dq_s[...] = jnp.zeros_like(dq_s)

        def block(kb, masked):
            k0 = pl.multiple_of(kb * t, t)
            kc = kcat[pl.ds(k0, t), :]
            p = jnp.exp(_dot(q, kc, NT) * MLA_SCALE - lsev)
            if masked:
                p = jnp.where(_chunk_mask(t, t), p, 0.0)
            ds = (p * (_dot(dob, kv_ref[pl.ds(k0, t), HEAD:2 * HEAD], NT) - delta) * MLA_SCALE).astype(BF16)
            dkT[kb] += _dot(qT, ds, NN)
            dvT[kb] += _dot(doT, p.astype(BF16), NN)
            dq_s[...] += _dot(ds, kc, NN)

        def step(kb, c):
            block(kb, False)
            return c

        lax.fori_loop(0, qi, step, 0)
        block(qi, True)
        dq_ref[...] = dq_s[...]
        last_q = qi == pl.num_programs(1) - 1

        @pl.when(last_q)
        def _():
            def flush(kb, c):
                r0 = pl.multiple_of(kb * t, t)
                dkv_ref[pl.ds(r0, t), 0:HEAD] = dkT[kb, 0:HEAD, :].T
                dkv_ref[pl.ds(r0, t), HEAD:2 * HEAD] = dvT[kb].T
                dkrT[kb] += dkT[kb, HEAD:2 * HEAD, :]
                return c

            lax.fori_loop(0, nkb, flush, 0)

        @pl.when(last_q & (h == pl.num_programs(0) - 1))
        def _():
            def flush(kb, c):
                r0 = pl.multiple_of(kb * t, t)
                dkr_ref[pl.ds(r0, t), :] = dkrT[kb].T
                return c

            lax.fori_loop(0, nkb, flush, 0)

    qblk = pl.BlockSpec((t, 2 * HEAD), lambda h, i: (i, h))
    kvfull = pl.BlockSpec((S, 2 * HEAD), lambda h, i: (0, h))
    krfull = pl.BlockSpec((S, LANE), lambda h, i: (0, 0))
    oblk = pl.BlockSpec((t, HEAD), lambda h, i: (i, h))
    return _pcall(
        body, name=name, grid=(MLA_HEADS, S // t),
        in_specs=[qblk, kvfull, krfull, oblk, pl.BlockSpec((1, t, 1), lambda h, i: (h, i, 0)), oblk],
        out_specs=[qblk, kvfull, krfull],
        out_shape=[jax.ShapeDtypeStruct(q.shape, F32), jax.ShapeDtypeStruct(kv.shape, F32), jax.ShapeDtypeStruct((S, LANE), F32)],
        scratch_shapes=[pltpu.VMEM((S, 2 * HEAD), BF16), pltpu.VMEM((t, 2 * HEAD), F32), pltpu.VMEM((nkb, 2 * HEAD, t), F32),
                        pltpu.VMEM((nkb, HEAD, t), F32), pltpu.VMEM((nkb, LANE, t), F32)],
        compiler_params=_params("arbitrary", "arbitrary"),
    )(q, kv, kr, o, lse, do)


GATE_TC = 512


def merge_fwd(z, ya, yb, yc, name):
    S = z.shape[0]
    tr, tc = _pick(S, 512), GATE_TC
    g0 = Z_GATE // tc
    nc = D_MODEL // tc

    def body(ga_ref, gb_ref, gc_ref, ya_ref, yb_ref, yc_ref, o_ref):
        acc = None
        for g, y in ((ga_ref, ya_ref), (gb_ref, yb_ref), (gc_ref, yc_ref)):
            term = _sigmoid(g[...].astype(F32)) * y[...].astype(F32)
            acc = term if acc is None else acc + term
        o_ref[...] = acc.astype(BF16)

    gate = lambda b: pl.BlockSpec((tr, tc), lambda i, j: (i, g0 + b * nc + j))
    blk = pl.BlockSpec((tr, tc), lambda i, j: (i, j))
    return _pcall(
        body, name=name, grid=(S // tr, nc), in_specs=[gate(0), gate(1), gate(2), blk, blk, blk], out_specs=blk,
        out_shape=jax.ShapeDtypeStruct((S, D_MODEL), BF16), compiler_params=_params("parallel", "parallel"),
    )(z, z, z, ya, yb, yc)


def merge_bwd(z, ya, yb, yc, dm, name):
    S = z.shape[0]
    tr, tc = _pick(S, 512), GATE_TC
    g0 = Z_GATE // tc
    nc = D_MODEL // tc

    def body(ga_ref, gb_ref, gc_ref, ya_ref, yb_ref, yc_ref, dm_ref, da_ref, db_ref, dc_ref, dga_ref, dgb_ref, dgc_ref):
        dmv = dm_ref[...].astype(F32)
        for g, y, dy, dg in ((ga_ref, ya_ref, da_ref, dga_ref), (gb_ref, yb_ref, db_ref, dgb_ref), (gc_ref, yc_ref, dc_ref, dgc_ref)):
            sg = _sigmoid(g[...].astype(F32))
            dy[...] = (dmv * sg).astype(BF16)
            dg[...] = (dmv * y[...].astype(F32) * sg * (1.0 - sg)).astype(BF16)

    gate = lambda b: pl.BlockSpec((tr, tc), lambda i, j: (i, g0 + b * nc + j))
    blk = pl.BlockSpec((tr, tc), lambda i, j: (i, j))
    out = jax.ShapeDtypeStruct((S, D_MODEL), BF16)
    return _pcall(
        body, name=name, grid=(S // tr, nc), in_specs=[gate(0), gate(1), gate(2), blk, blk, blk, blk],
        out_specs=[blk] * 6, out_shape=[out] * 6, compiler_params=_params("parallel", "parallel"),
    )(z, z, z, ya, yb, yc, dm)


def loss_head(x, g, target, name):
    S, D = x.shape
    tr = _pick(S, 512)

    def body(x_ref, g_ref, t_ref, l_ref, dx_ref, dg_ref):
        @pl.when(pl.program_id(0) == 0)
        def _():
            l_ref[...] = jnp.zeros_like(l_ref)
            dg_ref[...] = jnp.zeros_like(dg_ref)

        xv, gv = x_ref[...], g_ref[...]
        r = lax.rsqrt(jnp.mean(xv * xv, axis=-1, keepdims=True) + NORM_EPS)
        diff = xv * r * gv - t_ref[...]
        l_ref[...] += 0.5 * jnp.sum(jnp.mean(diff * diff, axis=-1, keepdims=True), axis=0, keepdims=True)
        dx, dg = _rms_bwd_math(xv, gv, diff * (1.0 / D))
        dx_ref[...] = dx
        dg_ref[...] += dg

    row = pl.BlockSpec((tr, D), lambda i: (i, 0))
    vec = pl.BlockSpec((1, D), lambda i: (0, 0))
    return _pcall(
        body, name=name, grid=(S // tr,), in_specs=[row, vec, row],
        out_specs=[pl.BlockSpec((1, LANE), lambda i: (0, 0)), row, vec],
        out_shape=[jax.ShapeDtypeStruct((1, LANE), F32), jax.ShapeDtypeStruct((S, D), F32), jax.ShapeDtypeStruct((1, D), F32)],
        compiler_params=_params("arbitrary"),
    )(x, g, target)


def _peer(k, x, y, c):
    px = 1 - x if k & 4 else x
    py = 1 - y if k & 2 else y
    pc = 1 - c if k & 1 else c
    return (px, py, pc), 4 * px + 2 * py + pc


def exchange(arrs, gather, name):
    n = len(arrs)
    shapes = [((N_DEV,) + a.shape) if gather else a.shape for a in arrs]

    def body(*refs):
        ins, outs = refs[:n], refs[n:2 * n]
        send_sems, recv_sems, loc_sems = refs[2 * n:]
        x, y, c = lax.axis_index("x"), lax.axis_index("y"), lax.axis_index("c")
        me = 4 * x + 2 * y + c
        sends, recvs, locs = [], [], []
        for a in range(n):
            loc = pltpu.make_async_copy(ins[a] if gather else ins[a].at[me], outs[a].at[me], loc_sems.at[a])
            loc.start()
            locs.append(loc)
            for k in range(1, N_DEV):
                peer, pid = _peer(k, x, y, c)
                s = a * (N_DEV - 1) + k - 1
                src = ins[a] if gather else ins[a].at[pid]
                snd = pltpu.make_async_remote_copy(src_ref=src, dst_ref=outs[a].at[me], send_sem=send_sems.at[s],
                                                   recv_sem=recv_sems.at[s], device_id=peer, device_id_type=pl.DeviceIdType.MESH)
                snd.start()
                sends.append(snd)
                recvs.append(pltpu.make_async_remote_copy(src_ref=src, dst_ref=outs[a].at[pid], send_sem=send_sems.at[s],
                                                          recv_sem=recv_sems.at[s], device_id=peer, device_id_type=pl.DeviceIdType.MESH))
        for snd, rcv in zip(sends, recvs):
            snd.wait_send()
            rcv.wait_recv()
        for loc in locs:
            loc.wait()

    any_spec = pl.BlockSpec(memory_space=pl.ANY)
    outs = _pcall(
        body, name=name, in_specs=[any_spec] * n, out_specs=[any_spec] * n,
        out_shape=[jax.ShapeDtypeStruct(s, a.dtype) for s, a in zip(shapes, arrs)],
        scratch_shapes=[pltpu.SemaphoreType.DMA((n * (N_DEV - 1),)), pltpu.SemaphoreType.DMA((n * (N_DEV - 1),)),
                        pltpu.SemaphoreType.DMA((n,))],
        compiler_params=pltpu.CompilerParams(has_side_effects=True),
    )(*arrs)
    return list(outs)


_HBM = pl.BlockSpec(memory_space=pltpu.HBM)
_SEM = pl.BlockSpec(memory_space=pltpu.SEMAPHORE)
_EFFECT = pltpu.SideEffectType.DATAFLOW_SIDE_EFFECTING


def _peer_copies(srcs, lands, send_sems, recv_sems, gather):
    x, y, c = lax.axis_index("x"), lax.axis_index("y"), lax.axis_index("c")
    me = 4 * x + 2 * y + c
    out = []
    for a, (src, land) in enumerate(zip(srcs, lands)):
        for k in range(1, N_DEV):
            peer, pid = _peer(k, x, y, c)
            s = a * (N_DEV - 1) + k - 1
            mk = lambda dst: pltpu.make_async_remote_copy(
                src_ref=src if gather else src.at[pid], dst_ref=dst, send_sem=send_sems.at[s], recv_sem=recv_sems.at[s],
                device_id=peer, device_id_type=pl.DeviceIdType.MESH)
            out.append((mk(land.at[me]), mk(land.at[pid])))
    return out


def exchange_start(arrs, gather, after, name):
    n = len(arrs)
    nsem = n * (N_DEV - 1)
    lands = [lax.empty(((N_DEV,) + a.shape) if gather else a.shape, a.dtype) for a in arrs]

    def body(*refs):
        srcs, land_refs = refs[:n], refs[n:2 * n]
        send_sems, recv_sems = refs[2 * n + 1], refs[2 * n + 2]
        token = refs[-1]
        for snd, _ in _peer_copies(srcs, land_refs, send_sems, recv_sems, gather):
            snd.start()
        token[...] = jnp.zeros_like(token)

    hbm = lambda a: pltpu.HBM(a.shape, a.dtype)
    outs = _pcall(
        body, name=name, in_specs=[_HBM] * (2 * n) + [pl.BlockSpec(memory_space=pl.ANY)],
        out_specs=[_SEM, _SEM] + [_HBM] * (2 * n) + [pl.BlockSpec(memory_space=pltpu.VMEM)],
        out_shape=[pltpu.SemaphoreType.DMA((nsem,)), pltpu.SemaphoreType.DMA((nsem,))] + [hbm(a) for a in arrs]
        + [hbm(a) for a in lands] + [jax.ShapeDtypeStruct((SUBLANE, LANE), F32)],
        input_output_aliases={i: i + 2 for i in range(2 * n)},
        compiler_params=pltpu.CompilerParams(has_side_effects=_EFFECT),
    )(*[pltpu.with_memory_space_constraint(a, pltpu.HBM) for a in list(arrs) + lands], after)
    return (outs[0], outs[1], list(outs[2:2 + n]), list(outs[2 + n:2 + 2 * n])), outs[-1]


def exchange_wait(handle, gather, after, name):
    send_sems, recv_sems, srcs, lands = handle
    n = len(srcs)

    def body(*refs):
        src_refs, land_refs = refs[:n], refs[n:2 * n]
        for snd, rcv in _peer_copies(src_refs, land_refs, refs[2 * n], refs[2 * n + 1], gather):
            snd.wait_send()
            rcv.wait_recv()

    hbm = lambda a: pltpu.HBM(a.shape, a.dtype)
    outs = _pcall(
        body, name=name, in_specs=[_HBM] * (2 * n) + [_SEM, _SEM, pl.BlockSpec(memory_space=pl.ANY)],
        out_specs=[_HBM] * (2 * n), out_shape=[hbm(a) for a in srcs] + [hbm(a) for a in lands],
        input_output_aliases={i: i for i in range(2 * n)},
        compiler_params=pltpu.CompilerParams(has_side_effects=_EFFECT),
    )(*srcs, *lands, send_sems, recv_sems, after)
    return list(outs[n:])


def _my_index():
    return 4 * lax.axis_index("x") + 2 * lax.axis_index("y") + lax.axis_index("c")


def adamw_sum(parts, w, m, v, name):
    L, R, C = w.shape
    tr = R
    for cand in (512, 352, 256, 128, 64, 48, 32, 16, 8):
        if R % cand == 0 and cand * C * 4 <= 2 * 1024 * 1024:
            tr = cand
            break
    c1 = 1.0 - ADAM_B1 ** ADAM_STEP
    c2 = 1.0 - ADAM_B2 ** ADAM_STEP

    def body(p_ref, w_ref, m_ref, v_ref, g_ref, d_ref, nm_ref, nv_ref):
        g = p_ref[0, 0].astype(F32)
        for k in range(1, N_DEV):
            g = g + p_ref[k, 0].astype(F32)
        m2 = ADAM_B1 * m_ref[0] + (1.0 - ADAM_B1) * g
        v2 = ADAM_B2 * v_ref[0] + (1.0 - ADAM_B2) * (g * g)
        g_ref[0] = g
        nm_ref[0] = m2
        nv_ref[0] = v2
        d_ref[0] = -ADAM_LR * ((m2 / c1) / (jnp.sqrt(v2 / c2) + ADAM_EPS) + ADAM_WD * w_ref[0])

    blk = pl.BlockSpec((1, tr, C), lambda l, i: (l, i, 0))
    out = jax.ShapeDtypeStruct((L, R, C), F32)
    return _pcall(
        body, name=name, grid=(L, R // tr),
        in_specs=[pl.BlockSpec((N_DEV, 1, tr, C), lambda l, i: (0, l, i, 0)), blk, blk, blk],
        out_specs=[blk] * 4, out_shape=[out] * 4, compiler_params=_params("parallel", "parallel"),
    )(parts, w, m, v)


def adamw_layer(parts, w, m, v, layer, prev, name):
    L, R, C = w.shape
    tr = R
    for cand in (512, 352, 256, 128, 64, 48, 32, 16, 8):
        if R % cand == 0 and cand * C * 4 <= 2 * 1024 * 1024:
            tr = cand
            break
    c1 = 1.0 - ADAM_B1 ** ADAM_STEP
    c2 = 1.0 - ADAM_B2 ** ADAM_STEP
    n_prev = 0 if prev is None else 4

    def body(*refs):
        p_ref, w_ref, m_ref, v_ref = refs[:4]
        g_ref, d_ref, nm_ref, nv_ref = refs[4 + n_prev:]
        g = p_ref[0].astype(F32)
        for k in range(1, N_DEV):
            g = g + p_ref[k].astype(F32)
        m2 = ADAM_B1 * m_ref[0] + (1.0 - ADAM_B1) * g
        v2 = ADAM_B2 * v_ref[0] + (1.0 - ADAM_B2) * (g * g)
        g_ref[0] = g
        nm_ref[0] = m2
        nv_ref[0] = v2
        d_ref[0] = -ADAM_LR * ((m2 / c1) / (jnp.sqrt(v2 / c2) + ADAM_EPS) + ADAM_WD * w_ref[0])

    blk = pl.BlockSpec((1, tr, C), lambda i: (layer, i, 0))
    out = jax.ShapeDtypeStruct((L, R, C), F32)
    return _pcall(
        body, name=name, grid=(R // tr,),
        in_specs=[pl.BlockSpec((N_DEV, tr, C), lambda i: (0, i, 0)), blk, blk, blk] + [pl.BlockSpec(memory_space=pl.ANY)] * n_prev,
        out_specs=[blk] * 4, out_shape=[out] * 4, input_output_aliases={4 + j: j for j in range(n_prev)},
        compiler_params=_params("parallel"),
    )(parts, w, m, v, *(prev or ()))


def _cols_full(g):
    return jnp.transpose(g, (1, 0, 2)).reshape(g.shape[1], N_DEV * g.shape[2])


def _cols_shards(w):
    R = w.shape[0]
    return jnp.transpose(w.reshape(R, N_DEV, w.shape[1] // N_DEV), (1, 0, 2))


def _w_in_to_z(w):
    kr0 = Z_GATE
    gate0 = Z_GATE + MLA_ROPE
    pad = jnp.zeros((w.shape[0], Z_W - Z_KR - MLA_ROPE), w.dtype)
    return jnp.concatenate([w[:, :kr0], w[:, gate0:], w[:, kr0:gate0], pad], axis=1)


def _z_to_w_in(dw):
    return jnp.concatenate([dw[:, :Z_GATE], dw[:, Z_KR:Z_KR + MLA_ROPE], dw[:, Z_GATE:Z_KR]], axis=1)


def _block_diag(w):
    eye = jnp.eye(RG_BLOCKS, dtype=w.dtype)
    return (w[:, :, None, :] * eye[:, None, :, None]).reshape(D_MODEL, D_MODEL).astype(BF16)


def _diag_blocks(d):
    d4 = d.reshape(RG_BLOCKS, RG_BLOCK_W, RG_BLOCKS, RG_BLOCK_W)
    return jnp.stack([d4[n, :, n, :] for n in range(RG_BLOCKS)], axis=0)


def _uq_full(g):
    p = jnp.pad(g, ((0, 0), (0, 0), (0, 2 * HEAD - HEAD - MLA_ROPE)))
    return jnp.transpose(p, (1, 0, 2)).reshape(MLA_LORA, MLA_HEADS * 2 * HEAD)


def _uq_shards(dw):
    return jnp.transpose(dw.reshape(MLA_LORA, MLA_HEADS, 2 * HEAD), (1, 0, 2))[:, :, :HEAD + MLA_ROPE]


SMALL = ("ffn1_norm", "mix_norm", "conv_b", "rg_w_a", "rg_b_a", "rg_w_x", "rg_b_x", "rg_lambda", "mla_q_norm",
         "mla_kv_norm", "ffn2_norm", "final_norm")
BIG = ("ffn1_w_gate_up", "ffn1_w_down", "w_in", "conv_w", "mla_w_uq", "mla_w_ukv", "w_branch_a", "w_branch_b",
       "w_branch_c", "w_out", "ffn2_w_gate_up", "ffn2_w_down")
ROW_SHARDED = ("ffn1_w_down", "w_branch_a", "w_branch_b", "w_branch_c", "w_out", "ffn2_w_down")


FIRST = ("ffn1_w_gate_up", "ffn1_w_down")
REST = tuple(k for k in BIG if k not in FIRST)


def _full_weights(g):
    fw = {}
    for k, s in g.items():
        if k in ROW_SHARDED:
            fw[k] = s.reshape(-1, s.shape[-1])
        elif k == "w_in":
            fw[k] = _w_in_to_z(_cols_full(s))
        elif k == "mla_w_uq":
            fw[k] = _uq_full(s)
        else:
            fw[k] = _cols_full(s)
    return fw


def _grad_shards(dw):
    out = {}
    for k, g in dw.items():
        if k in ROW_SHARDED:
            out[k] = g.reshape(N_DEV, g.shape[0] // N_DEV, g.shape[1])
        elif k == "w_in":
            out[k] = _cols_shards(_z_to_w_in(g))
        elif k == "mla_w_uq":
            out[k] = _uq_shards(g)
        else:
            out[k] = _cols_shards(g)
    return out


def ffn_fwd(x, norm, w_gu, w_d, tag):
    h = rms_fwd(x, norm, f"{tag}_rms")
    gu = matmul(h, w_gu, "nn", BF16, f"{tag}_gu")
    a = swiglu_fwd(gu, f"{tag}_act")
    y = matmul(a, w_d, "nn", F32, f"{tag}_down", scale=0.5, res=x, tk=1408)
    return y, (x, h, gu, a)


def ffn_bwd(dy, saved, norm, w_gu, w_d, tag):
    x, h, gu, a = saved
    da = matmul(dy, w_d, "nt", BF16, f"{tag}_dact", scale=0.5, tn=1408)
    dw_d = matmul(a, dy, "tn", BF16, f"{tag}_dwd", scale=0.5, tm=1408)
    dgu = swiglu_bwd(gu, da, f"{tag}_dgu")
    dw_gu = matmul(h, dgu, "tn", BF16, f"{tag}_dwgu")
    dh = matmul(dgu, w_gu, "nt", F32, f"{tag}_dh")
    dx, dnorm = rms_bwd(x, norm, dh, dy, f"{tag}_drms")
    return dx, dw_gu, dw_d, dnorm


def mixer_fwd(x, sp, fw, cs, sn, tag):
    h = rms_fwd(x, sp["mix_norm"], f"{tag}_rms")
    z = matmul(h, fw["w_in"], "nn", BF16, f"{tag}_in", tn=1280)
    wa, wx = _block_diag(sp["rg_w_a"]), _block_diag(sp["rg_w_x"])
    ya, hs = rglru_fwd(z, fw["conv_w"], sp["conv_b"], wa, wx, sp["rg_b_a"], sp["rg_b_x"], sp["rg_lambda"], f"{tag}_rg")
    yb, ltot = sb_fwd(z, f"{tag}_sb")
    cqn, ckvn, krope = mla_prep_fwd(z, sp["mla_q_norm"], sp["mla_kv_norm"], cs, sn, f"{tag}_mprep")
    q = q_rope(matmul(cqn, fw["mla_w_uq"], "nn", F32, f"{tag}_uq"), cs, sn, 1.0, f"{tag}_qrope")
    kv = matmul(ckvn, fw["mla_w_ukv"], "nn", BF16, f"{tag}_ukv")
    yc, lse = mla_fwd(q, kv, krope, f"{tag}_mla")
    pa = matmul(ya, fw["w_branch_a"], "nn", BF16, f"{tag}_pa")
    pb = matmul(yb, fw["w_branch_b"], "nn", BF16, f"{tag}_pb")
    pc = matmul(yc, fw["w_branch_c"], "nn", BF16, f"{tag}_pc")
    merged = merge_fwd(z, pa, pb, pc, f"{tag}_merge")
    y = matmul(merged, fw["w_out"], "nn", F32, f"{tag}_out", res=x)
    return y, (x, h, z, wa, wx, ya, hs, yb, ltot, cqn, ckvn, krope, q, kv, yc, lse, pa, pb, pc, merged)


def mixer_bwd(dy, saved, sp, fw, cs, sn, tag):
    x, h, z, wa, wx, ya, hs, yb, ltot, cqn, ckvn, krope, q, kv, yc, lse, pa, pb, pc, merged = saved
    S = x.shape[0]
    dw, ds = {}, {}
    dmerged = matmul(dy, fw["w_out"], "nt", BF16, f"{tag}_dmerged")
    dw["w_out"] = matmul(merged, dy, "tn", BF16, f"{tag}_dwout")
    dpa, dpb, dpc, dga, dgb, dgc = merge_bwd(z, pa, pb, pc, dmerged, f"{tag}_dmerge")
    dya = matmul(dpa, fw["w_branch_a"], "nt", BF16, f"{tag}_dya")
    dyb = matmul(dpb, fw["w_branch_b"], "nt", BF16, f"{tag}_dyb")
    dyc = matmul(dpc, fw["w_branch_c"], "nt", BF16, f"{tag}_dyc")
    dw["w_branch_a"] = matmul(ya, dpa, "tn", BF16, f"{tag}_dwa")
    dw["w_branch_b"] = matmul(yb, dpb, "tn", BF16, f"{tag}_dwb")
    dw["w_branch_c"] = matmul(yc, dpc, "tn", BF16, f"{tag}_dwc")
    drgx, drgg, dwa, dwx, dvec = rglru_bwd(z, hs, dya, fw["conv_w"], sp["conv_b"], wa, wx, sp["rg_b_a"], sp["rg_b_x"],
                                           sp["rg_lambda"], f"{tag}_drg")
    ds["rg_w_a"], ds["rg_w_x"] = _diag_blocks(dwa), _diag_blocks(dwx)
    ds["rg_b_a"], ds["rg_b_x"], ds["rg_lambda"], ds["conv_b"] = dvec[0], dvec[1], dvec[2], dvec[3]
    dw["conv_w"] = dvec[4:8]
    dsq, dsk, dsv = sb_bwd(z, ltot, dyb, f"{tag}_dsb")
    dq, dkv, dkr = mla_bwd(q, kv, krope, yc, lse, dyc, f"{tag}_dmla")
    dqp = q_rope(dq, cs, sn, -1.0, f"{tag}_dqrope")
    dw["mla_w_uq"] = matmul(cqn, dqp, "tn", BF16, f"{tag}_dwuq")
    dw["mla_w_ukv"] = matmul(ckvn, dkv, "tn", BF16, f"{tag}_dwukv")
    dcqn = matmul(dqp, fw["mla_w_uq"], "nt", F32, f"{tag}_dcqn")
    dckvn = matmul(dkv, fw["mla_w_ukv"], "nt", F32, f"{tag}_dckvn")
    dcq, dckv, dkrr, dqn, dkvn = mla_prep_bwd(z, sp["mla_q_norm"], sp["mla_kv_norm"], cs, sn, dcqn, dckvn, dkr, f"{tag}_dmprep")
    ds["mla_q_norm"], ds["mla_kv_norm"] = dqn[0], dkvn[0]
    dz = jnp.concatenate([drgx, drgg, dsq.astype(BF16), dsk.astype(BF16), dsv.astype(BF16), dcq, dckv, dga, dgb, dgc, dkrr,
                          jnp.zeros((S, Z_W - Z_KR - LANE), BF16)], axis=1)
    dw["w_in"] = matmul(h, dz, "tn", BF16, f"{tag}_dwin", tn=1280)
    dh = matmul(dz, fw["w_in"], "nt", F32, f"{tag}_dh", tk=1280)
    dx, dnorm = rms_bwd(x, sp["mix_norm"], dh, dy, f"{tag}_drms")
    ds["mix_norm"] = dnorm[0]
    return dx, dw, ds


def _rope_tables(positions):
    inv = ROPE_THETA ** (-jnp.arange(0, MLA_ROPE, 2, dtype=F32) / MLA_ROPE)
    ang = positions.astype(F32)[:, None] * inv
    zeros = jnp.zeros((positions.shape[0], LANE - MLA_ROPE), F32)
    cs = jnp.concatenate([jnp.cos(ang), jnp.cos(ang), zeros], axis=1)
    sn = jnp.concatenate([jnp.sin(ang), jnp.sin(ang), zeros], axis=1)
    return cs, sn


def local_step(x, positions, target, small, fetch, emit):
    L = small["ffn1_norm"].shape[0]
    cs, sn = _rope_tables(positions)
    row = lambda v: v.reshape(1, -1)
    saved = []
    for l in range(L):
        sp = {k: small[k][l] for k in SMALL if k != "final_norm"}
        sp = {k: (v if v.ndim == 3 else row(v)) for k, v in sp.items()}
        g, token = fetch(l, "first", x)
        fw = _full_weights(g)
        if token is not None:
            sp["ffn1_norm"] = sp["ffn1_norm"] + token[0:1, 0:1]
        x, s1 = ffn_fwd(x, sp["ffn1_norm"], fw["ffn1_w_gate_up"], fw["ffn1_w_down"], f"l{l}_f1")
        g, token = fetch(l, "rest", x)
        fw.update(_full_weights(g))
        if token is not None:
            sp["mix_norm"] = sp["mix_norm"] + token[0:1, 0:1]
        x, s2 = mixer_fwd(x, sp, fw, cs, sn, f"l{l}_mx")
        x, s3 = ffn_fwd(x, sp["ffn2_norm"], fw["ffn2_w_gate_up"], fw["ffn2_w_down"], f"l{l}_f2")
        saved.append((fw, sp, s1, s2, s3))
    loss, dx, dfinal = loss_head(x, row(small["final_norm"]), target, "loss_head")
    small_grads = [None] * L
    for l in reversed(range(L)):
        fw, sp, s1, s2, s3 = saved[l]
        dx, dgu2, dd2, dn2 = ffn_bwd(dx, s3, sp["ffn2_norm"], fw["ffn2_w_gate_up"], fw["ffn2_w_down"], f"l{l}_f2")
        dx, dw, ds = mixer_bwd(dx, s2, sp, fw, cs, sn, f"l{l}_mx")
        dw.update(ffn2_w_gate_up=dgu2, ffn2_w_down=dd2)
        dx = emit(l, "rest", _grad_shards(dw), dx)
        dx, dgu1, dd1, dn1 = ffn_bwd(dx, s1, sp["ffn1_norm"], fw["ffn1_w_gate_up"], fw["ffn1_w_down"], f"l{l}_f1")
        ds.update(ffn1_norm=dn1[0], ffn2_norm=dn2[0])
        small_grads[l] = ds
        dx = emit(l, "first", _grad_shards(dict(ffn1_w_gate_up=dgu1, ffn1_w_down=dd1)), dx)
    return loss[0, 0], dx, small_grads, dfinal[0]


def _pack_small(tree):
    flat = jnp.concatenate([tree[k].reshape(-1).astype(F32) for k in SMALL])
    rows = -(-flat.shape[0] // (SUBLANE * D_MODEL)) * SUBLANE
    return jnp.pad(flat, (0, rows * D_MODEL - flat.shape[0])).reshape(rows, D_MODEL)


def _unpack_small(buf, like):
    flat = buf.reshape(-1)
    out, off = {}, 0
    for k in SMALL:
        n = like[k].size
        out[k] = flat[off:off + n].reshape(like[k].shape)
        off += n
    return out


NAMES = ("ffn1_norm", "ffn1_w_gate_up", "ffn1_w_down", "mix_norm", "w_in", "conv_w", "conv_b", "rg_w_a", "rg_b_a", "rg_w_x",
         "rg_b_x", "rg_lambda", "mla_q_norm", "mla_w_uq", "mla_kv_norm", "mla_w_ukv", "w_branch_a", "w_branch_b",
         "w_branch_c", "w_out", "ffn2_norm", "ffn2_w_gate_up", "ffn2_w_down", "final_norm")


def kernel(x, positions, *rest):
    n = len(NAMES)
    w = dict(zip(NAMES, rest[:n]))
    target = rest[n]
    m = dict(zip(NAMES, rest[n + 1:2 * n + 1]))
    v = dict(zip(NAMES, rest[2 * n + 1:3 * n + 1]))
    L = w["ffn1_norm"].shape[0]
    me = _my_index()

    stages = [(0, FIRST), (0, REST)] + [(l, BIG) for l in range(1, L)]
    shard = lambda l, k: w[k][l] if k == "conv_w" else w[k][l].astype(BF16)

    pending, got = {}, {}

    def gather_start(s, after):
        l, names = stages[s]
        mine = [shard(l, k) for k in names]
        handle, token = exchange_start(mine, True, after, f"gather_start_{s}")
        pending[(l, names[0])] = (s, handle, mine)
        return token

    def fetch(l, part, after):
        token = None
        key = (l, FIRST[0] if part == "first" else REST[0])
        if key in pending:
            s, handle, mine = pending.pop(key)
            landed = exchange_wait(handle, True, after, f"gather_wait_{s}")
            if s + 1 < len(stages):
                token = gather_start(s + 1, landed[0])
            filled = [lax.dynamic_update_slice_in_dim(g, a[None], me, 0) for g, a in zip(landed, mine)]
            got.update({(l, k): a for k, a in zip(stages[s][1], filled)})
        return {k: got.pop((l, k)) for k in (FIRST if part == "first" else REST)}, token

    gather_start(0, x)

    flying, stash, res = [], {}, {}

    def land(after):
        s, handle, own = flying.pop()
        l, names = stages[s]
        landed = exchange_wait(handle, False, after, f"scatter_wait_{s}")
        for k, g, o in zip(names, landed, own):
            parts = lax.dynamic_update_slice_in_dim(g, o, me, 0)
            res[k] = adamw_layer(parts, w[k], m[k], v[k], l, res.get(k), f"adamw_{k}_{l}")
        return landed[0]

    def emit(l, part, gshards, dx):
        stash.update(gshards)
        s = next(i for i, (sl, names) in enumerate(stages) if sl == l and (names[0] == FIRST[0]) == (part == "first" or l > 0))
        if l > 0 and part == "rest":
            return dx
        send = [stash.pop(k) for k in stages[s][1]]
        own = [lax.dynamic_slice_in_dim(a, me, 1, 0) for a in send]
        after = land(dx) if flying else dx
        handle, token = exchange_start(send, False, after, f"scatter_start_{s}")
        flying.append((s, handle, own))
        return dx + token[0:1, 0:1]

    small = {k: w[k] for k in SMALL}
    loss, dx, small_grads, dfinal = local_step(x[0], positions[0], target[0], small, fetch, emit)
    loss = lax.psum(loss, ("x", "y", "c"))

    sg = {k: jnp.stack([small_grads[l][k] for l in range(L)], axis=0) for k in SMALL if k != "final_norm"}
    sg["final_norm"] = dfinal
    small_parts = exchange([_pack_small(sg)], True, "gather_small_grads")[0]
    land(small_parts)
    packed = adamw_sum(small_parts[:, None], _pack_small(small)[None], _pack_small({k: m[k] for k in SMALL})[None],
                       _pack_small({k: v[k] for k in SMALL})[None], "adamw_small")
    unpacked = [_unpack_small(p[0], small) for p in packed]
    for k in SMALL:
        res[k] = tuple(u[k] for u in unpacked)

    outs = [loss, dx[None]]
    for i in range(4):
        outs += [res[k][i] for k in NAMES]
    return tuple(outs)
```

```python
import functools
import math

import jax
import jax.numpy as jnp
from jax import lax
from jax.experimental import pallas as pl
from jax.experimental.pallas import tpu as pltpu

F32 = jnp.float32
BF16 = jnp.bfloat16

N_DEV = 8
D_MODEL = 1024
D_FF = 2816
NORM_EPS = 1e-6
RG_BLOCKS = 16
RG_BLOCK_W = 64
RG_C = 8.0
SB_HEADS = 8
HEAD = 128
MLA_HEADS = 8
MLA_LORA = 256
MLA_ROPE = 64
ROPE_THETA = 10000.0
CHUNK = 64
SB_SCALE = HEAD ** -0.5
MLA_SCALE = (HEAD + MLA_ROPE) ** -0.5
N_IN = 8768

Z_RGX, Z_RGG, Z_Q, Z_K, Z_V, Z_CQ, Z_CKV, Z_GATE, Z_KR, Z_W = 0, 1024, 2048, 3072, 4096, 5120, 5376, 5632, 8704, 8960

ADAM_LR, ADAM_B1, ADAM_B2, ADAM_EPS, ADAM_WD, ADAM_STEP = 0.001, 0.9, 0.999, 1e-08, 0.01, 10

LANE = 128
SUBLANE = 8
VMEM_LIMIT = 48 * 1024 * 1024
NEG = -1e30


def _pcall(body, **kw):
    return pl.pallas_call(body, **kw)


def _params(*sem):
    return pltpu.CompilerParams(dimension_semantics=sem or None, vmem_limit_bytes=VMEM_LIMIT)


def _pick(dim, target):
    best = None
    t = LANE
    while t <= min(dim, target):
        if dim % t == 0:
            best = t
        t += LANE
    return best if best is not None else dim


def _sigmoid(x):
    return 1.0 / (1.0 + jnp.exp(-x))


def _gelu_and_grad(x):
    c = math.sqrt(2.0 / math.pi)
    inner = c * (x + 0.044715 * x * x * x)
    t = jnp.tanh(inner)
    val = 0.5 * x * (1.0 + t)
    grad = 0.5 * (1.0 + t) + 0.5 * x * (1.0 - t * t) * c * (1.0 + 3.0 * 0.044715 * x * x)
    return val, grad


def _neg_expm1(y):
    series = -y * (1.0 + y * (0.5 + y * (1.0 / 6.0 + y * (1.0 / 24.0))))
    return jnp.where(jnp.abs(y) < 0.02, series, 1.0 - jnp.exp(y))


def _dot(a, b, dims):
    return lax.dot_general(a, b, (dims, ((), ())), preferred_element_type=F32)


NN = ((1,), (0,))
NT = ((1,), (1,))
TN = ((0,), (0,))


def matmul(a, b, mode, out_dtype, name, scale=1.0, res=None, tm=1024, tn=1024, tk=1024, b_koff=0):
    if mode == "nn":
        (M, K), N = a.shape, b.shape[1]
    elif mode == "nt":
        (M, K), N = a.shape, b.shape[0]
    else:
        (K, M), N = a.shape, b.shape[1]
    tm, tn, tk = _pick(M, tm), _pick(N, tn), _pick(K, tk)
    nk = K // tk
    dims = {"nn": NN, "nt": NT, "tn": TN}[mode]

    def body(*refs):
        if res is None:
            a_ref, b_ref, o_ref, acc = refs
        else:
            a_ref, b_ref, r_ref, o_ref, acc = refs
        k = pl.program_id(2)

        @pl.when(k == 0)
        def _():
            acc[...] = jnp.zeros_like(acc)

        acc[...] += _dot(a_ref[...].astype(BF16), b_ref[...].astype(BF16), dims)

        @pl.when(k == nk - 1)
        def _():
            r = acc[...] * scale
            if res is not None:
                r = r + r_ref[...]
            o_ref[...] = r.astype(out_dtype)

    a_spec = pl.BlockSpec((tk, tm), lambda i, j, k: (k, i)) if mode == "tn" else pl.BlockSpec((tm, tk), lambda i, j, k: (i, k))
    b_spec = pl.BlockSpec((tn, tk), lambda i, j, k: (j, k + b_koff)) if mode == "nt" else pl.BlockSpec((tk, tn), lambda i, j, k: (k, j))
    o_spec = pl.BlockSpec((tm, tn), lambda i, j, k: (i, j))
    in_specs = [a_spec, b_spec] + ([o_spec] if res is not None else [])
    args = (a, b) + ((res,) if res is not None else ())
    return _pcall(
        body, name=name, grid=(M // tm, N // tn, nk), in_specs=in_specs, out_specs=o_spec,
        out_shape=jax.ShapeDtypeStruct((M, N), out_dtype), scratch_shapes=[pltpu.VMEM((tm, tn), F32)],
        compiler_params=_params("parallel", "parallel", "arbitrary"),
    )(*args)


def rms_fwd(x, g, name, col=0):
    S, D = x.shape[0], g.shape[1]
    tr = _pick(S, 512)

    def body(x_ref, g_ref, o_ref):
        xv = x_ref[...].astype(F32)
        r = lax.rsqrt(jnp.mean(xv * xv, axis=-1, keepdims=True) + NORM_EPS)
        o_ref[...] = (xv * r * g_ref[...]).astype(BF16)

    return _pcall(
        body, name=name, grid=(S // tr,),
        in_specs=[pl.BlockSpec((tr, D), lambda i: (i, col)), pl.BlockSpec((1, D), lambda i: (0, 0))],
        out_specs=pl.BlockSpec((tr, D), lambda i: (i, 0)),
        out_shape=jax.ShapeDtypeStruct((S, D), BF16), compiler_params=_params("parallel"),
    )(x, g)


def _rms_bwd_math(xv, g, dh):
    r = lax.rsqrt(jnp.mean(xv * xv, axis=-1, keepdims=True) + NORM_EPS)
    xhat = xv * r
    dxhat = dh * g
    dx = r * (dxhat - xhat * jnp.mean(dxhat * xhat, axis=-1, keepdims=True))
    dg = jnp.sum(dh * xhat, axis=0, keepdims=True)
    return dx, dg


def rms_bwd(x, g, dh, dres, name):
    S, D = x.shape
    tr = _pick(S, 512)

    def body(x_ref, g_ref, dh_ref, dr_ref, dx_ref, dg_ref):
        dx, dg = _rms_bwd_math(x_ref[...], g_ref[...], dh_ref[...])
        dx_ref[...] = dx + dr_ref[...]

        @pl.when(pl.program_id(0) == 0)
        def _():
            dg_ref[...] = jnp.zeros_like(dg_ref)

        dg_ref[...] += dg

    row = pl.BlockSpec((tr, D), lambda i: (i, 0))
    vec = pl.BlockSpec((1, D), lambda i: (0, 0))
    return _pcall(
        body, name=name, grid=(S // tr,), in_specs=[row, vec, row, row], out_specs=[row, vec],
        out_shape=[jax.ShapeDtypeStruct((S, D), F32), jax.ShapeDtypeStruct((1, D), F32)],
        compiler_params=_params("arbitrary"),
    )(x, g, dh, dres)


def ffn_up(h, w_gu, name):
    S, D = h.shape
    tm, tn = _pick(S, 512), D_FF // 2
    nc = D_FF // tn

    def body(h_ref, wg_ref, wu_ref, g_ref, u_ref, a_ref):
        hv = h_ref[...]
        g = _dot(hv, wg_ref[...], NN)
        u = _dot(hv, wu_ref[...], NN)
        g_ref[...] = g.astype(BF16)
        u_ref[...] = u.astype(BF16)
        a_ref[...] = (g * _sigmoid(g) * u).astype(BF16)

    blk = pl.BlockSpec((tm, tn), lambda j, i: (i, j))
    out = jax.ShapeDtypeStruct((S, D_FF), BF16)
    return _pcall(
        body, name=name, grid=(nc, S // tm),
        in_specs=[pl.BlockSpec((tm, D), lambda j, i: (i, 0)), pl.BlockSpec((D, tn), lambda j, i: (0, j)),
                  pl.BlockSpec((D, tn), lambda j, i: (0, j + nc))],
        out_specs=[blk] * 3, out_shape=[out] * 3, compiler_params=_params("parallel", "parallel"),
    )(h, w_gu, w_gu)


def ffn_dact(dy, w_d, g, u, name):
    S, D = dy.shape
    tm, tn = _pick(S, 512), D_FF // 2
    nc = D_FF // tn

    def body(dy_ref, wd_ref, g_ref, u_ref, dg_ref, du_ref):
        da = _dot(dy_ref[...].astype(BF16), wd_ref[...], NT) * 0.5
        gv = g_ref[...].astype(F32)
        sg = _sigmoid(gv)
        dg_ref[...] = (da * u_ref[...].astype(F32) * sg * (1.0 + gv * (1.0 - sg))).astype(BF16)
        du_ref[...] = (da * gv * sg).astype(BF16)

    blk = pl.BlockSpec((tm, tn), lambda j, i: (i, j))
    out = jax.ShapeDtypeStruct((S, D_FF), BF16)
    return _pcall(
        body, name=name, grid=(nc, S // tm),
        in_specs=[pl.BlockSpec((tm, D), lambda j, i: (i, 0)), pl.BlockSpec((tn, D), lambda j, i: (j, 0)), blk, blk],
        out_specs=[blk] * 2, out_shape=[out] * 2, compiler_params=_params("parallel", "parallel"),
    )(dy, w_d, g, u)


def _conv_taps(xpad, T, cw, cb):
    u = cb + cw[3:4, :] * xpad[pl.ds(8, T), :]
    for tap in range(3):
        u = u + cw[tap:tap + 1, :] * xpad[pl.ds(5 + tap, T), :]
    return u


def _rg_gates(u, wa_ref, wx_ref, ba, bx, lam):
    ub = u.astype(BF16)
    r = _sigmoid(_dot(ub, wa_ref[...], NN) + ba)
    ig = _sigmoid(_dot(ub, wx_ref[...], NN) + bx)
    nlam = -lam
    clam = -RG_C * (jnp.maximum(nlam, 0.0) + jnp.log(1.0 + jnp.exp(-jnp.abs(nlam))))
    la = clam * r
    return r, ig, clam, la


def rglru_fwd(z, cw, cb, wa, wx, ba, bx, lam, name):
    S, D = z.shape[0], D_MODEL
    T = _pick(S, 256)

    def body(x_ref, g_ref, cw_ref, cb_ref, wa_ref, wx_ref, ba_ref, bx_ref, lam_ref, y_ref, h_ref, xpad, a_s, b_s, hst):
        @pl.when(pl.program_id(0) == 0)
        def _():
            xpad[pl.ds(0, 8), :] = jnp.zeros((8, D), F32)
            hst[...] = jnp.zeros_like(hst)

        xpad[pl.ds(8, T), :] = x_ref[...].astype(F32)
        u = _conv_taps(xpad, T, cw_ref[...], cb_ref[...])
        xpad[pl.ds(0, 8), :] = xpad[pl.ds(T, 8), :]
        r, ig, clam, la = _rg_gates(u, wa_ref, wx_ref, ba_ref[...], bx_ref[...], lam_ref[...])
        a_s[...] = jnp.exp(la)
        b_s[...] = jnp.sqrt(_neg_expm1(2.0 * la)) * (ig * u)

        def tile(j, h):
            r0 = pl.multiple_of(j * 8, 8)
            av = a_s[pl.ds(r0, 8), :]
            bv = b_s[pl.ds(r0, 8), :]
            rows = []
            for k in range(8):
                h = av[k:k + 1, :] * h + bv[k:k + 1, :]
                rows.append(h)
            h_ref[pl.ds(r0, 8), :] = jnp.concatenate(rows, axis=0)
            return h

        hst[...] = lax.fori_loop(0, T // 8, tile, hst[...])
        gel, _ = _gelu_and_grad(g_ref[...].astype(F32))
        y_ref[...] = (h_ref[...] * gel).astype(BF16)

    blk = lambda c: pl.BlockSpec((T, D), lambda i: (i, c))
    vec = pl.BlockSpec((1, D), lambda i: (0, 0))
    full = lambda r: pl.BlockSpec((r, D), lambda i: (0, 0))
    return _pcall(
        body, name=name, grid=(S // T,),
        in_specs=[blk(0), blk(1), full(4), vec, full(D), full(D), vec, vec, vec],
        out_specs=[blk(0), blk(0)],
        out_shape=[jax.ShapeDtypeStruct((S, D), BF16), jax.ShapeDtypeStruct((S, D), F32)],
        scratch_shapes=[pltpu.VMEM((T + 8, D), F32), pltpu.VMEM((T, D), F32), pltpu.VMEM((T, D), F32), pltpu.VMEM((1, D), F32)],
        compiler_params=_params("arbitrary"),
    )(z, z, cw, cb, wa, wx, ba, bx, lam)


def rglru_bwd(z, hs, dy, cw, cb, wa, wx, ba, bx, lam, name):
    S, D = z.shape[0], D_MODEL
    T = _pick(S, 256)
    nb = S // T
    t8 = T // 8

    def body(x_ref, xp_ref, g_ref, h_ref, hp_ref, dy_ref, cw_ref, cb_ref, wa_ref, wx_ref, ba_ref, bx_ref, lam_ref,
             dx_ref, dg_ref, dwa_ref, dwx_ref, dvec_ref, xpad, hpad, dupad, a_s, d_s, carry):
        i = pl.program_id(0)
        first_block = i == nb - 1

        @pl.when(i == 0)
        def _():
            dwa_ref[...] = jnp.zeros_like(dwa_ref)
            dwx_ref[...] = jnp.zeros_like(dwx_ref)
            dvec_ref[...] = jnp.zeros_like(dvec_ref)
            carry[...] = jnp.zeros_like(carry)
            dupad[pl.ds(T, 8), :] = jnp.zeros((8, D), F32)

        keep = jnp.where(first_block, 0.0, 1.0)
        xpad[pl.ds(0, 8), :] = xp_ref[...].astype(F32) * keep
        xpad[pl.ds(8, T), :] = x_ref[...].astype(F32)
        hpad[pl.ds(0, 8), :] = hp_ref[...] * keep
        hpad[pl.ds(8, T), :] = h_ref[...]
        cwv = cw_ref[...]
        u = _conv_taps(xpad, T, cwv, cb_ref[...])
        r, ig, clam, la = _rg_gates(u, wa_ref, wx_ref, ba_ref[...], bx_ref[...], lam_ref[...])
        a = jnp.exp(la)
        a_s[...] = a
        gv = g_ref[...].astype(F32)
        gel, dgel = _gelu_and_grad(gv)
        dyv = dy_ref[...].astype(F32)
        d_s[...] = dyv * gel
        dg_ref[...] = (dyv * h_ref[...] * dgel).astype(BF16)

        def tile(j, c):
            r0 = pl.multiple_of((t8 - 1 - j) * 8, 8)
            av = a_s[pl.ds(r0, 8), :]
            dv = d_s[pl.ds(r0, 8), :]
            rows = [None] * 8
            for k in range(7, -1, -1):
                d = dv[k:k + 1, :] + c
                rows[k] = d
                c = av[k:k + 1, :] * d
            d_s[pl.ds(r0, 8), :] = jnp.concatenate(rows, axis=0)
            return c

        carry[...] = lax.fori_loop(0, t8, tile, carry[...])
        dht = d_s[...]
        hprev = hpad[pl.ds(7, T), :]
        w = _neg_expm1(2.0 * la)
        s = jnp.sqrt(w)
        e2 = 1.0 - w
        d_iu = dht * s
        dla = dht * hprev * a - dht * (ig * u) * e2 / s
        dpr = (dla * clam * r * (1.0 - r))
        dpi = (d_iu * u * ig * (1.0 - ig))
        dprb, dpib, ub = dpr.astype(BF16), dpi.astype(BF16), u.astype(BF16)
        du = d_iu * ig + _dot(dprb, wa_ref[...], NT) + _dot(dpib, wx_ref[...], NT)
        dwa_ref[...] += _dot(ub, dprb, TN)
        dwx_ref[...] += _dot(ub, dpib, TN)
        dvec_ref[0:1, :] += jnp.sum(dpr, axis=0, keepdims=True)
        dvec_ref[1:2, :] += jnp.sum(dpi, axis=0, keepdims=True)
        dvec_ref[2:3, :] += jnp.sum(dla * r, axis=0, keepdims=True)
        dvec_ref[3:4, :] += jnp.sum(du, axis=0, keepdims=True)
        for tap in range(4):
            dvec_ref[4 + tap:5 + tap, :] += jnp.sum(du * xpad[pl.ds(5 + tap, T), :], axis=0, keepdims=True)
        dupad[pl.ds(0, T), :] = du
        dx = cwv[3:4, :] * du
        for tap in range(3):
            dx = dx + cwv[tap:tap + 1, :] * dupad[pl.ds(3 - tap, T), :]
        dx_ref[...] = dx.astype(BF16)
        dupad[pl.ds(T, 8), :] = dupad[pl.ds(0, 8), :]

        @pl.when(first_block)
        def _():
            dvec_ref[2:3, :] = dvec_ref[2:3, :] * (RG_C * _sigmoid(-lam_ref[...]))

    rev = lambda c: pl.BlockSpec((T, D), lambda i: (nb - 1 - i, c))
    prev = lambda c: pl.BlockSpec((8, D), lambda i: (jnp.maximum((nb - 1 - i) * t8 - 1, 0), c))
    vec = pl.BlockSpec((1, D), lambda i: (0, 0))
    full = lambda r: pl.BlockSpec((r, D), lambda i: (0, 0))
    return _pcall(
        body, name=name, grid=(nb,),
        in_specs=[rev(0), prev(0), rev(1), rev(0), prev(0), rev(0), full(4), vec, full(D), full(D), vec, vec, vec],
        out_specs=[rev(0), rev(0), full(D), full(D), full(8)],
        out_shape=[jax.ShapeDtypeStruct((S, D), BF16), jax.ShapeDtypeStruct((S, D), BF16),
                   jax.ShapeDtypeStruct((D, D), F32), jax.ShapeDtypeStruct((D, D), F32), jax.ShapeDtypeStruct((8, D), F32)],
        scratch_shapes=[pltpu.VMEM((T + 8, D), F32), pltpu.VMEM((T + 8, D), F32), pltpu.VMEM((T + 8, D), F32),
                        pltpu.VMEM((T, D), F32), pltpu.VMEM((T, D), F32), pltpu.VMEM((1, D), F32)],
        compiler_params=_params("arbitrary"),
    )(z, z, z, hs, hs, dy, cw, cb, wa, wx, ba, bx, lam)


def _tri(n, kind):
    j = lax.broadcasted_iota(jnp.int32, (n, n), 0)
    s = lax.broadcasted_iota(jnp.int32, (n, n), 1)
    m = {"gt": j > s, "le": j <= s, "lt": j < s}[kind]
    return jnp.where(m, 1.0, 0.0).astype(BF16)


def _dot2(x, tri):
    hi = x.astype(BF16)
    lo = (x - hi.astype(F32)).astype(BF16)
    return _dot(jnp.concatenate([hi, lo], axis=1), jnp.concatenate([tri, tri], axis=0), NN)


SB_TK = 128

def _sb_logits(q, kblk, q0, k0, tq, masked):
    z = _dot(q, kblk, NT) * SB_SCALE
    sp = jnp.maximum(z, 0.0) + jnp.log(1.0 + jnp.exp(-jnp.abs(z)))
    lkeep = -sp
    mask = None
    if masked:
        tpos = q0 + lax.broadcasted_iota(jnp.int32, (tq, SB_TK), 0)
        spos = k0 + lax.broadcasted_iota(jnp.int32, (tq, SB_TK), 1)
        mask = spos < tpos
        lkeep = jnp.where(mask, lkeep, 0.0)
    return mask, lkeep, z - sp


def sb_fwd(z, name):
    S = z.shape[0]
    tq, tk = _pick(S, 512), SB_TK
    nd = tq // tk
    U = min(4, nd)
    qc, kc, vc = Z_Q // HEAD, Z_K // HEAD, Z_V // HEAD

    def body(q_ref, k_ref, v_ref, o_ref, lt_ref, acc, run):
        qi = pl.program_id(1)
        q0 = qi * tq
        q = q_ref[...]
        tri = _tri(tk, "gt")
        acc[...] = jnp.zeros_like(acc)
        run[...] = jnp.zeros_like(run)

        def group(k0s, masked):
            parts = [(k0,) + _sb_logits(q, k_ref[pl.ds(k0, tk), :], q0, k0, tq, masked) for k0 in k0s]
            cums = [_dot2(p[2], tri) for p in parts]
            r, a = run[...], acc[...]
            for (k0, mask, lkeep, lbeta), cum in zip(parts, cums):
                w = jnp.exp(lbeta + cum + r)
                if masked:
                    w = jnp.where(mask, w, 0.0)
                a = a + _dot(w.astype(BF16), v_ref[pl.ds(k0, tk), :], NN)
                r = r + jnp.sum(lkeep, axis=1, keepdims=True)
            acc[...] = a
            run[...] = r

        for g in range(nd // U):
            group([pl.multiple_of(q0 + (nd - 1 - g * U - u) * tk, tk) for u in range(U)], True)

        def step(i, c):
            base = qi * nd - 1 - i * U
            group([pl.multiple_of((base - u) * tk, tk) for u in range(U)], False)
            return c

        lax.fori_loop(0, qi * nd // U, step, 0)
        o_ref[...] = acc[...].astype(BF16)
        lt_ref[0] = run[...]

    return _pcall(
        body, name=name, grid=(SB_HEADS, S // tq),
        in_specs=[pl.BlockSpec((tq, HEAD), lambda h, i: (i, qc + h)), pl.BlockSpec((S, HEAD), lambda h, i: (0, kc + h)),
                  pl.BlockSpec((S, HEAD), lambda h, i: (0, vc + h))],
        out_specs=[pl.BlockSpec((tq, HEAD), lambda h, i: (i, h)), pl.BlockSpec((1, tq, 1), lambda h, i: (h, i, 0))],
        out_shape=[jax.ShapeDtypeStruct((S, SB_HEADS * HEAD), BF16), jax.ShapeDtypeStruct((SB_HEADS, S, 1), F32)],
        scratch_shapes=[pltpu.VMEM((tq, HEAD), F32), pltpu.VMEM((tq, 1), F32)],
        compiler_params=_params("parallel", "parallel"),
    )(z, z, z)


def sb_bwd(z, ltot, dy, name):
    S = z.shape[0]
    tq, tk = _pick(S, 512), SB_TK
    nd = tq // tk
    U = min(4, nd)
    nkb = S // tk
    qc, kc, vc = Z_Q // HEAD, Z_K // HEAD, Z_V // HEAD

    def body(q_ref, k_ref, v_ref, lt_ref, do_ref, dq_ref, dk_ref, dv_ref, dq_s, run_l, run_g, dkT, dvT):
        qi = pl.program_id(1)
        q0 = qi * tq

        @pl.when(qi == 0)
        def _():
            dkT[...] = jnp.zeros_like(dkT)
            dvT[...] = jnp.zeros_like(dvT)

        q = q_ref[...]
        do = do_ref[...].astype(BF16)
        qT, doT = q.T, do.T
        ltv = lt_ref[0]
        tri_le, tri_lt = _tri(tk, "le"), _tri(tk, "lt")
        dq_s[...] = jnp.zeros_like(dq_s)
        run_l[...] = jnp.zeros_like(run_l)
        run_g[...] = jnp.zeros_like(run_g)

        def group(k0s, masked):
            parts = []
            for k0 in k0s:
                kblk = k_ref[pl.ds(k0, tk), :]
                mask, lkeep, lbeta = _sb_logits(q, kblk, q0, k0, tq, masked)
                parts.append((k0, kblk, mask, lkeep, lbeta, _dot(do, v_ref[pl.ds(k0, tk), :], NT)))
            pres = [_dot2(p[3], tri_le) for p in parts]
            rl = run_l[...]
            ws = []
            for (k0, kblk, mask, lkeep, lbeta, dw), pre in zip(parts, pres):
                w = jnp.exp(lbeta + (ltv - (pre + rl)))
                if masked:
                    w = jnp.where(mask, w, 0.0)
                ws.append((w, w * dw))
                rl = rl + jnp.sum(lkeep, axis=1, keepdims=True)
            run_l[...] = rl
            gpres = [_dot2(g, tri_lt) for _, g in ws]
            rg, dq = run_g[...], dq_s[...]
            for (k0, kblk, mask, lkeep, lbeta, dw), (w, g), gpre in zip(parts, ws, gpres):
                dz = (g * jnp.exp(lkeep) - jnp.exp(lbeta) * (gpre + rg)) * SB_SCALE
                if masked:
                    dz = jnp.where(mask, dz, 0.0)
                dz = dz.astype(BF16)
                dq = dq + _dot(dz, kblk, NN)
                kb = k0 // tk
                dkT[kb] += _dot(qT, dz, NN)
                dvT[kb] += _dot(doT, w.astype(BF16), NN)
                rg = rg + jnp.sum(g, axis=1, keepdims=True)
            run_g[...] = rg
            dq_s[...] = dq

        def step(i, c):
            group([pl.multiple_of((i * U + u) * tk, tk) for u in range(U)], False)
            return c

        lax.fori_loop(0, qi * nd // U, step, 0)
        for g in range(nd // U):
            group([pl.multiple_of(q0 + (g * U + u) * tk, tk) for u in range(U)], True)
        dq_ref[...] = dq_s[...].astype(BF16)

        @pl.when(qi == pl.num_programs(1) - 1)
        def _():
            def flush(kb, c):
                r0 = pl.multiple_of(kb * tk, tk)
                dk_ref[pl.ds(r0, tk), :] = dkT[kb].T.astype(BF16)
                dv_ref[pl.ds(r0, tk), :] = dvT[kb].T.astype(BF16)
                return c

            lax.fori_loop(0, nkb, flush, 0)

    qblk = lambda c: pl.BlockSpec((tq, HEAD), lambda h, i: (i, c + h))
    kfull = lambda c: pl.BlockSpec((S, HEAD), lambda h, i: (0, c + h))
    out = jax.ShapeDtypeStruct((S, SB_HEADS * HEAD), BF16)
    return _pcall(
        body, name=name, grid=(SB_HEADS, S // tq),
        in_specs=[qblk(qc), kfull(kc), kfull(vc), pl.BlockSpec((1, tq, 1), lambda h, i: (h, i, 0)), qblk(0)],
        out_specs=[qblk(0), kfull(0), kfull(0)], out_shape=[out, out, out],
        scratch_shapes=[pltpu.VMEM((tq, HEAD), F32), pltpu.VMEM((tq, 1), F32), pltpu.VMEM((tq, 1), F32),
                        pltpu.VMEM((nkb, HEAD, tk), F32), pltpu.VMEM((nkb, HEAD, tk), F32)],
        compiler_params=_params("arbitrary", "arbitrary"),
    )(z, z, z, ltot, dy)


def _rope(x, cs, sn, sign):
    lane = lax.broadcasted_iota(jnp.int32, x.shape, 1)
    swapped = jnp.where(lane < MLA_ROPE // 2, -pltpu.roll(x, LANE - MLA_ROPE // 2, 1), pltpu.roll(x, MLA_ROPE // 2, 1))
    return x * cs + sign * swapped * sn


def mla_prep_fwd(z, qn, kvn, cs, sn, name):
    S = z.shape[0]
    tr = _pick(S, 512)

    def body(cq_ref, ckv_ref, kr_ref, qn_ref, kvn_ref, cs_ref, sn_ref, oq_ref, okv_ref, okr_ref):
        for src, g, dst in ((cq_ref, qn_ref, oq_ref), (ckv_ref, kvn_ref, okv_ref)):
            xv = src[...].astype(F32)
            r = lax.rsqrt(jnp.mean(xv * xv, axis=-1, keepdims=True) + NORM_EPS)
            dst[...] = (xv * r * g[...]).astype(BF16)
        okr_ref[...] = _rope(kr_ref[...].astype(F32), cs_ref[...], sn_ref[...], 1.0).astype(BF16)

    lora = lambda c: pl.BlockSpec((tr, MLA_LORA), lambda i: (i, c))
    tile = lambda c: pl.BlockSpec((tr, LANE), lambda i: (i, c))
    vec = pl.BlockSpec((1, MLA_LORA), lambda i: (0, 0))
    return _pcall(
        body, name=name, grid=(S // tr,),
        in_specs=[lora(Z_CQ // MLA_LORA), lora(Z_CKV // MLA_LORA), tile(Z_KR // LANE), vec, vec, tile(0), tile(0)],
        out_specs=[lora(0), lora(0), tile(0)],
        out_shape=[jax.ShapeDtypeStruct((S, MLA_LORA), BF16), jax.ShapeDtypeStruct((S, MLA_LORA), BF16),
                   jax.ShapeDtypeStruct((S, LANE), BF16)],
        compiler_params=_params("parallel"),
    )(z, z, z, qn, kvn, cs, sn)


def mla_prep_bwd(z, qn, kvn, cs, sn, dcqn, dckvn, dkrope, name):
    S = z.shape[0]
    tr = _pick(S, 512)

    def body(cq_ref, ckv_ref, qn_ref, kvn_ref, cs_ref, sn_ref, dq_ref, dkv_ref, dkr_ref, oq_ref, okv_ref, okr_ref, gq_ref, gkv_ref):
        @pl.when(pl.program_id(0) == 0)
        def _():
            gq_ref[...] = jnp.zeros_like(gq_ref)
            gkv_ref[...] = jnp.zeros_like(gkv_ref)

        for src, g, dh, dst, gacc in ((cq_ref, qn_ref, dq_ref, oq_ref, gq_ref), (ckv_ref, kvn_ref, dkv_ref, okv_ref, gkv_ref)):
            dx, dg = _rms_bwd_math(src[...].astype(F32), g[...], dh[...])
            dst[...] = dx.astype(BF16)
            gacc[...] += dg
        okr_ref[...] = _rope(dkr_ref[...], cs_ref[...], sn_ref[...], -1.0).astype(BF16)

    lora = lambda c: pl.BlockSpec((tr, MLA_LORA), lambda i: (i, c))
    tile = lambda c: pl.BlockSpec((tr, LANE), lambda i: (i, c))
    vec = pl.BlockSpec((1, MLA_LORA), lambda i: (0, 0))
    return _pcall(
        body, name=name, grid=(S // tr,),
        in_specs=[lora(Z_CQ // MLA_LORA), lora(Z_CKV // MLA_LORA), vec, vec, tile(0), tile(0), lora(0), lora(0), tile(0)],
        out_specs=[lora(0), lora(0), tile(0), vec, vec],
        out_shape=[jax.ShapeDtypeStruct((S, MLA_LORA), BF16), jax.ShapeDtypeStruct((S, MLA_LORA), BF16),
                   jax.ShapeDtypeStruct((S, LANE), BF16), jax.ShapeDtypeStruct((1, MLA_LORA), F32), jax.ShapeDtypeStruct((1, MLA_LORA), F32)],
        compiler_params=_params("arbitrary"),
    )(z, z, qn, kvn, cs, sn, dcqn, dckvn, dkrope)


def q_rope(q, cs, sn, sign, name):
    S = q.shape[0]
    tr = _pick(S, 512)

    def body(q_ref, cs_ref, sn_ref, o_ref):
        o_ref[:, 0:LANE] = q_ref[:, 0:LANE].astype(BF16)
        o_ref[:, LANE:2 * LANE] = _rope(q_ref[:, LANE:2 * LANE], cs_ref[...], sn_ref[...], sign).astype(BF16)

    blk = pl.BlockSpec((tr, 2 * LANE), lambda i, h: (i, h))
    tile = pl.BlockSpec((tr, LANE), lambda i, h: (i, 0))
    return _pcall(
        body, name=name, grid=(S // tr, MLA_HEADS), in_specs=[blk, tile, tile], out_specs=blk,
        out_shape=jax.ShapeDtypeStruct(q.shape, BF16), compiler_params=_params("parallel", "parallel"),
    )(q, cs, sn)


def _chunk_mask(rows, cols):
    tch = lax.broadcasted_iota(jnp.int32, (rows, cols), 0) // CHUNK
    sch = lax.broadcasted_iota(jnp.int32, (rows, cols), 1) // CHUNK
    return sch <= tch


def _fill_kcat(kcat, kv_ref, kr_ref):
    kcat[:, 0:HEAD] = kv_ref[:, 0:HEAD]
    kcat[:, HEAD:2 * HEAD] = kr_ref[...]


def mla_fwd(q, kv, kr, name):
    S = q.shape[0]
    t = _pick(S, 512)

    def body(q_ref, kv_ref, kr_ref, o_ref, lse_ref, kcat, m_s, l_s, acc):
        qi = pl.program_id(1)

        @pl.when(qi == 0)
        def _():
            _fill_kcat(kcat, kv_ref, kr_ref)

        q = q_ref[...]
        m_s[...] = jnp.full_like(m_s, NEG)
        l_s[...] = jnp.zeros_like(l_s)
        acc[...] = jnp.zeros_like(acc)

        def block(k0, width, masked):
            s = _dot(q, kcat[pl.ds(k0, width), :], NT) * MLA_SCALE
            if masked:
                s = jnp.where(_chunk_mask(t, width), s, NEG)
            m = m_s[...]
            m2 = jnp.maximum(m, jnp.max(s, axis=1, keepdims=True))
            p = jnp.exp(s - m2)
            alpha = jnp.exp(m - m2)
            l_s[...] = alpha * l_s[...] + jnp.sum(p, axis=1, keepdims=True)
            acc[...] = alpha * acc[...] + _dot(p.astype(BF16), kv_ref[pl.ds(k0, width), HEAD:2 * HEAD], NN)
            m_s[...] = m2

        if S >= 2 * t:
            def step(i, c):
                block(pl.multiple_of(i * 2 * t, 2 * t), 2 * t, False)
                return c

            lax.fori_loop(0, qi // 2, step, 0)

            @pl.when(qi % 2 == 1)
            def _():
                block(pl.multiple_of((qi - 1) * t, t), t, False)

        block(pl.multiple_of(qi * t, t), t, True)
        o_ref[...] = (acc[...] / l_s[...]).astype(BF16)
        lse_ref[0] = m_s[...] + jnp.log(l_s[...])

    return _pcall(
        body, name=name, grid=(MLA_HEADS, S // t),
        in_specs=[pl.BlockSpec((t, 2 * HEAD), lambda h, i: (i, h)), pl.BlockSpec((S, 2 * HEAD), lambda h, i: (0, h)),
                  pl.BlockSpec((S, LANE), lambda h, i: (0, 0))],
        out_specs=[pl.BlockSpec((t, HEAD), lambda h, i: (i, h)), pl.BlockSpec((1, t, 1), lambda h, i: (h, i, 0))],
        out_shape=[jax.ShapeDtypeStruct((S, MLA_HEADS * HEAD), BF16), jax.ShapeDtypeStruct((MLA_HEADS, S, 1), F32)],
        scratch_shapes=[pltpu.VMEM((S, 2 * HEAD), BF16), pltpu.VMEM((t, 1), F32), pltpu.VMEM((t, 1), F32), pltpu.VMEM((t, HEAD), F32)],
        compiler_params=_params("arbitrary", "arbitrary"),
    )(q, kv, kr)


def mla_bwd(q, kv, kr, o, lse, do, cs, sn, name):
    S = q.shape[0]
    t = _pick(S, 512)
    nkb = S // t

    def body(q_ref, kv_ref, kr_ref, o_ref, lse_ref, do_ref, cs_ref, sn_ref, dq_ref, dkv_ref, dkr_ref, kcat, dq_s, dkT, dvT, dkrT):
        h, qi = pl.program_id(0), pl.program_id(1)

        @pl.when(qi == 0)
        def _():
            _fill_kcat(kcat, kv_ref, kr_ref)
            dkT[...] = jnp.zeros_like(dkT)
            dvT[...] = jnp.zeros_like(dvT)

        @pl.when((qi == 0) & (h == 0))
        def _():
            dkrT[...] = jnp.zeros_like(dkrT)

        q = q_ref[...]
        dov = do_ref[...].astype(F32)
        dob = dov.astype(BF16)
        qT, doT = q.T, dob.T
        delta = jnp.sum(dov * o_ref[...].astype(F32), axis=1, keepdims=True)
        lsev = lse_ref[0]
        dq_s[...] = jnp.zeros_like(dq_s)

        def block(kb, masked):
            k0 = pl.multiple_of(kb * t, t)
            kc = kcat[pl.ds(k0, t), :]
            p = jnp.exp(_dot(q, kc, NT) * MLA_SCALE - lsev)
            if masked:
                p = jnp.where(_chunk_mask(t, t), p, 0.0)
            ds = (p * (_dot(dob, kv_ref[pl.ds(k0, t), HEAD:2 * HEAD], NT) - delta) * MLA_SCALE).astype(BF16)
            dkT[kb] += _dot(qT, ds, NN)
            dvT[kb] += _dot(doT, p.astype(BF16), NN)
            dq_s[...] += _dot(ds, kc, NN)

        def step(kb, c):
            block(kb, False)
            return c

        lax.fori_loop(0, qi, step, 0)
        block(qi, True)
        dq_ref[:, 0:HEAD] = dq_s[:, 0:HEAD].astype(BF16)
        dq_ref[:, HEAD:2 * HEAD] = _rope(dq_s[:, HEAD:2 * HEAD], cs_ref[...], sn_ref[...], -1.0).astype(BF16)
        last_q = qi == pl.num_programs(1) - 1

        @pl.when(last_q)
        def _():
            def flush(kb, c):
                r0 = pl.multiple_of(kb * t, t)
                dkv_ref[pl.ds(r0, t), 0:HEAD] = dkT[kb, 0:HEAD, :].T
                dkv_ref[pl.ds(r0, t), HEAD:2 * HEAD] = dvT[kb].T
                dkrT[kb] += dkT[kb, HEAD:2 * HEAD, :]
                return c

            lax.fori_loop(0, nkb, flush, 0)

        @pl.when(last_q & (h == pl.num_programs(0) - 1))
        def _():
            def flush(kb, c):
                r0 = pl.multiple_of(kb * t, t)
                dkr_ref[pl.ds(r0, t), :] = dkrT[kb].T
                return c

            lax.fori_loop(0, nkb, flush, 0)

    qblk = pl.BlockSpec((t, 2 * HEAD), lambda h, i: (i, h))
    kvfull = pl.BlockSpec((S, 2 * HEAD), lambda h, i: (0, h))
    krfull = pl.BlockSpec((S, LANE), lambda h, i: (0, 0))
    oblk = pl.BlockSpec((t, HEAD), lambda h, i: (i, h))
    return _pcall(
        body, name=name, grid=(MLA_HEADS, S // t),
        in_specs=[qblk, kvfull, krfull, oblk, pl.BlockSpec((1, t, 1), lambda h, i: (h, i, 0)), oblk,
                  pl.BlockSpec((t, LANE), lambda h, i: (i, 0)), pl.BlockSpec((t, LANE), lambda h, i: (i, 0))],
        out_specs=[qblk, kvfull, krfull],
        out_shape=[jax.ShapeDtypeStruct(q.shape, BF16), jax.ShapeDtypeStruct(kv.shape, F32), jax.ShapeDtypeStruct((S, LANE), F32)],
        scratch_shapes=[pltpu.VMEM((S, 2 * HEAD), BF16), pltpu.VMEM((t, 2 * HEAD), F32), pltpu.VMEM((nkb, 2 * HEAD, t), F32),
                        pltpu.VMEM((nkb, HEAD, t), F32), pltpu.VMEM((nkb, LANE, t), F32)],
        compiler_params=_params("arbitrary", "arbitrary"),
    )(q, kv, kr, o, lse, do, cs, sn)


GATE_TC = 512


def merge_fwd(z, ya, yb, yc, name):
    S = z.shape[0]
    tr, tc = _pick(S, 512), GATE_TC
    g0 = Z_GATE // tc
    nc = D_MODEL // tc

    def body(ga_ref, gb_ref, gc_ref, ya_ref, yb_ref, yc_ref, o_ref):
        acc = None
        for g, y in ((ga_ref, ya_ref), (gb_ref, yb_ref), (gc_ref, yc_ref)):
            term = _sigmoid(g[...].astype(F32)) * y[...].astype(F32)
            acc = term if acc is None else acc + term
        o_ref[...] = acc.astype(BF16)

    gate = lambda b: pl.BlockSpec((tr, tc), lambda i, j: (i, g0 + b * nc + j))
    blk = pl.BlockSpec((tr, tc), lambda i, j: (i, j))
    return _pcall(
        body, name=name, grid=(S // tr, nc), in_specs=[gate(0), gate(1), gate(2), blk, blk, blk], out_specs=blk,
        out_shape=jax.ShapeDtypeStruct((S, D_MODEL), BF16), compiler_params=_params("parallel", "parallel"),
    )(z, z, z, ya, yb, yc)


def merge_bwd(z, ya, yb, yc, dm, name):
    S = z.shape[0]
    tr, tc = _pick(S, 512), GATE_TC
    g0 = Z_GATE // tc
    nc = D_MODEL // tc

    def body(ga_ref, gb_ref, gc_ref, ya_ref, yb_ref, yc_ref, dm_ref, da_ref, db_ref, dc_ref, dga_ref, dgb_ref, dgc_ref):
        dmv = dm_ref[...].astype(F32)
        for g, y, dy, dg in ((ga_ref, ya_ref, da_ref, dga_ref), (gb_ref, yb_ref, db_ref, dgb_ref), (gc_ref, yc_ref, dc_ref, dgc_ref)):
            sg = _sigmoid(g[...].astype(F32))
            dy[...] = (dmv * sg).astype(BF16)
            dg[...] = (dmv * y[...].astype(F32) * sg * (1.0 - sg)).astype(BF16)

    gate = lambda b: pl.BlockSpec((tr, tc), lambda i, j: (i, g0 + b * nc + j))
    blk = pl.BlockSpec((tr, tc), lambda i, j: (i, j))
    out = jax.ShapeDtypeStruct((S, D_MODEL), BF16)
    return _pcall(
        body, name=name, grid=(S // tr, nc), in_specs=[gate(0), gate(1), gate(2), blk, blk, blk, blk],
        out_specs=[blk] * 6, out_shape=[out] * 6, compiler_params=_params("parallel", "parallel"),
    )(z, z, z, ya, yb, yc, dm)


def loss_head(x, g, target, name):
    S, D = x.shape
    tr = _pick(S, 512)

    def body(x_ref, g_ref, t_ref, l_ref, dx_ref, dg_ref):
        @pl.when(pl.program_id(0) == 0)
        def _():
            l_ref[...] = jnp.zeros_like(l_ref)
            dg_ref[...] = jnp.zeros_like(dg_ref)

        xv, gv = x_ref[...], g_ref[...]
        r = lax.rsqrt(jnp.mean(xv * xv, axis=-1, keepdims=True) + NORM_EPS)
        diff = xv * r * gv - t_ref[...]
        l_ref[...] += 0.5 * jnp.sum(jnp.mean(diff * diff, axis=-1, keepdims=True), axis=0, keepdims=True)
        dx, dg = _rms_bwd_math(xv, gv, diff * (1.0 / D))
        dx_ref[...] = dx
        dg_ref[...] += dg

    row = pl.BlockSpec((tr, D), lambda i: (i, 0))
    vec = pl.BlockSpec((1, D), lambda i: (0, 0))
    return _pcall(
        body, name=name, grid=(S // tr,), in_specs=[row, vec, row],
        out_specs=[pl.BlockSpec((1, LANE), lambda i: (0, 0)), row, vec],
        out_shape=[jax.ShapeDtypeStruct((1, LANE), F32), jax.ShapeDtypeStruct((S, D), F32), jax.ShapeDtypeStruct((1, D), F32)],
        compiler_params=_params("arbitrary"),
    )(x, g, target)


def _peer(k, x, y, c):
    px = 1 - x if k & 4 else x
    py = 1 - y if k & 2 else y
    pc = 1 - c if k & 1 else c
    return (px, py, pc), 4 * px + 2 * py + pc


def exchange(arrs, gather, name):
    n = len(arrs)
    shapes = [((N_DEV,) + a.shape) if gather else a.shape for a in arrs]

    def body(*refs):
        ins, outs = refs[:n], refs[n:2 * n]
        send_sems, recv_sems, loc_sems = refs[2 * n:]
        x, y, c = lax.axis_index("x"), lax.axis_index("y"), lax.axis_index("c")
        me = 4 * x + 2 * y + c
        sends, recvs, locs = [], [], []
        for a in range(n):
            loc = pltpu.make_async_copy(ins[a] if gather else ins[a].at[me], outs[a].at[me], loc_sems.at[a])
            loc.start()
            locs.append(loc)
            for k in range(1, N_DEV):
                peer, pid = _peer(k, x, y, c)
                s = a * (N_DEV - 1) + k - 1
                src = ins[a] if gather else ins[a].at[pid]
                snd = pltpu.make_async_remote_copy(src_ref=src, dst_ref=outs[a].at[me], send_sem=send_sems.at[s],
                                                   recv_sem=recv_sems.at[s], device_id=peer, device_id_type=pl.DeviceIdType.MESH)
                snd.start()
                sends.append(snd)
                recvs.append(pltpu.make_async_remote_copy(src_ref=src, dst_ref=outs[a].at[pid], send_sem=send_sems.at[s],
                                                          recv_sem=recv_sems.at[s], device_id=peer, device_id_type=pl.DeviceIdType.MESH))
        for snd, rcv in zip(sends, recvs):
            snd.wait_send()
            rcv.wait_recv()
        for loc in locs:
            loc.wait()

    any_spec = pl.BlockSpec(memory_space=pl.ANY)
    outs = _pcall(
        body, name=name, in_specs=[any_spec] * n, out_specs=[any_spec] * n,
        out_shape=[jax.ShapeDtypeStruct(s, a.dtype) for s, a in zip(shapes, arrs)],
        scratch_shapes=[pltpu.SemaphoreType.DMA((n * (N_DEV - 1),)), pltpu.SemaphoreType.DMA((n * (N_DEV - 1),)),
                        pltpu.SemaphoreType.DMA((n,))],
        compiler_params=pltpu.CompilerParams(has_side_effects=True),
    )(*arrs)
    return list(outs)


_HBM = pl.BlockSpec(memory_space=pltpu.HBM)
_SEM = pl.BlockSpec(memory_space=pltpu.SEMAPHORE)
_EFFECT = pltpu.SideEffectType.DATAFLOW_SIDE_EFFECTING


def _peer_copies(srcs, lands, send_sems, recv_sems, gather):
    x, y, c = lax.axis_index("x"), lax.axis_index("y"), lax.axis_index("c")
    me = 4 * x + 2 * y + c
    out = []
    for a, (src, land) in enumerate(zip(srcs, lands)):
        for k in range(1, N_DEV):
            peer, pid = _peer(k, x, y, c)
            s = a * (N_DEV - 1) + k - 1
            mk = lambda dst: pltpu.make_async_remote_copy(
                src_ref=src if gather else src.at[pid], dst_ref=dst, send_sem=send_sems.at[s], recv_sem=recv_sems.at[s],
                device_id=peer, device_id_type=pl.DeviceIdType.MESH)
            out.append((mk(land.at[me]), mk(land.at[pid])))
    return out


def exchange_start(arrs, gather, after, name):
    n = len(arrs)
    nsem = n * (N_DEV - 1)
    lands = [lax.empty(((N_DEV,) + a.shape) if gather else a.shape, a.dtype) for a in arrs]

    def body(*refs):
        srcs, land_refs = refs[:n], refs[n:2 * n]
        send_sems, recv_sems = refs[2 * n + 1], refs[2 * n + 2]
        token = refs[-1]
        for snd, _ in _peer_copies(srcs, land_refs, send_sems, recv_sems, gather):
            snd.start()
        token[...] = jnp.zeros_like(token)

    hbm = lambda a: pltpu.HBM(a.shape, a.dtype)
    outs = _pcall(
        body, name=name, in_specs=[_HBM] * (2 * n) + [pl.BlockSpec(memory_space=pl.ANY)],
        out_specs=[_SEM, _SEM] + [_HBM] * (2 * n) + [pl.BlockSpec(memory_space=pltpu.VMEM)],
        out_shape=[pltpu.SemaphoreType.DMA((nsem,)), pltpu.SemaphoreType.DMA((nsem,))] + [hbm(a) for a in arrs]
        + [hbm(a) for a in lands] + [jax.ShapeDtypeStruct((SUBLANE, LANE), F32)],
        input_output_aliases={i: i + 2 for i in range(2 * n)},
        compiler_params=pltpu.CompilerParams(has_side_effects=_EFFECT),
    )(*[pltpu.with_memory_space_constraint(a, pltpu.HBM) for a in list(arrs) + lands], after)
    return (outs[0], outs[1], list(outs[2:2 + n]), list(outs[2 + n:2 + 2 * n])), outs[-1]


def exchange_wait(handle, gather, after, name):
    send_sems, recv_sems, srcs, lands = handle
    n = len(srcs)

    def body(*refs):
        src_refs, land_refs = refs[:n], refs[n:2 * n]
        for snd, rcv in _peer_copies(src_refs, land_refs, refs[2 * n], refs[2 * n + 1], gather):
            snd.wait_send()
            rcv.wait_recv()

    hbm = lambda a: pltpu.HBM(a.shape, a.dtype)
    outs = _pcall(
        body, name=name, in_specs=[_HBM] * (2 * n) + [_SEM, _SEM, pl.BlockSpec(memory_space=pl.ANY)],
        out_specs=[_HBM] * (2 * n), out_shape=[hbm(a) for a in srcs] + [hbm(a) for a in lands],
        input_output_aliases={i: i for i in range(2 * n)},
        compiler_params=pltpu.CompilerParams(has_side_effects=_EFFECT),
    )(*srcs, *lands, send_sems, recv_sems, after)
    return list(outs[n:])


def _my_index():
    return 4 * lax.axis_index("x") + 2 * lax.axis_index("y") + lax.axis_index("c")


def adamw_sum(parts, w, m, v, name):
    L, R, C = w.shape
    tr = R
    for cand in (512, 352, 256, 128, 64, 48, 32, 16, 8):
        if R % cand == 0 and cand * C * 4 <= 2 * 1024 * 1024:
            tr = cand
            break
    c1 = 1.0 - ADAM_B1 ** ADAM_STEP
    c2 = 1.0 - ADAM_B2 ** ADAM_STEP

    def body(p_ref, w_ref, m_ref, v_ref, g_ref, d_ref, nm_ref, nv_ref):
        g = p_ref[0, 0].astype(F32)
        for k in range(1, N_DEV):
            g = g + p_ref[k, 0].astype(F32)
        m2 = ADAM_B1 * m_ref[0] + (1.0 - ADAM_B1) * g
        v2 = ADAM_B2 * v_ref[0] + (1.0 - ADAM_B2) * (g * g)
        g_ref[0] = g
        nm_ref[0] = m2
        nv_ref[0] = v2
        d_ref[0] = -ADAM_LR * ((m2 / c1) / (jnp.sqrt(v2 / c2) + ADAM_EPS) + ADAM_WD * w_ref[0])

    blk = pl.BlockSpec((1, tr, C), lambda l, i: (l, i, 0))
    out = jax.ShapeDtypeStruct((L, R, C), F32)
    return _pcall(
        body, name=name, grid=(L, R // tr),
        in_specs=[pl.BlockSpec((N_DEV, 1, tr, C), lambda l, i: (0, l, i, 0)), blk, blk, blk],
        out_specs=[blk] * 4, out_shape=[out] * 4, compiler_params=_params("parallel", "parallel"),
    )(parts, w, m, v)


def adamw_layer(parts, w, m, v, layer, prev, name):
    L, R, C = w.shape
    tr = R
    for cand in (512, 352, 256, 128, 64, 48, 32, 16, 8):
        if R % cand == 0 and cand * C * 4 <= 2 * 1024 * 1024:
            tr = cand
            break
    c1 = 1.0 - ADAM_B1 ** ADAM_STEP
    c2 = 1.0 - ADAM_B2 ** ADAM_STEP
    n_prev = 0 if prev is None else 4

    def body(*refs):
        p_ref, w_ref, m_ref, v_ref = refs[:4]
        g_ref, d_ref, nm_ref, nv_ref = refs[4 + n_prev:]
        g = p_ref[0].astype(F32)
        for k in range(1, N_DEV):
            g = g + p_ref[k].astype(F32)
        m2 = ADAM_B1 * m_ref[0] + (1.0 - ADAM_B1) * g
        v2 = ADAM_B2 * v_ref[0] + (1.0 - ADAM_B2) * (g * g)
        g_ref[0] = g
        nm_ref[0] = m2
        nv_ref[0] = v2
        d_ref[0] = -ADAM_LR * ((m2 / c1) / (jnp.sqrt(v2 / c2) + ADAM_EPS) + ADAM_WD * w_ref[0])

    blk = pl.BlockSpec((1, tr, C), lambda i: (layer, i, 0))
    out = jax.ShapeDtypeStruct((L, R, C), F32)
    return _pcall(
        body, name=name, grid=(R // tr,),
        in_specs=[pl.BlockSpec((N_DEV, tr, C), lambda i: (0, i, 0)), blk, blk, blk] + [pl.BlockSpec(memory_space=pl.ANY)] * n_prev,
        out_specs=[blk] * 4, out_shape=[out] * 4, input_output_aliases={4 + j: j for j in range(n_prev)},
        compiler_params=_params("parallel"),
    )(parts, w, m, v, *(prev or ()))


def _cols_full(g):
    return jnp.transpose(g, (1, 0, 2)).reshape(g.shape[1], N_DEV * g.shape[2])


def _cols_shards(w):
    R = w.shape[0]
    return jnp.transpose(w.reshape(R, N_DEV, w.shape[1] // N_DEV), (1, 0, 2))


def _w_in_to_z(w):
    kr0 = Z_GATE
    gate0 = Z_GATE + MLA_ROPE
    pad = jnp.zeros((w.shape[0], Z_W - Z_KR - MLA_ROPE), w.dtype)
    return jnp.concatenate([w[:, :kr0], w[:, gate0:], w[:, kr0:gate0], pad], axis=1)


def _z_to_w_in(dw):
    return jnp.concatenate([dw[:, :Z_GATE], dw[:, Z_KR:Z_KR + MLA_ROPE], dw[:, Z_GATE:Z_KR]], axis=1)


def _block_diag(w):
    eye = jnp.eye(RG_BLOCKS, dtype=w.dtype)
    return (w[:, :, None, :] * eye[:, None, :, None]).reshape(D_MODEL, D_MODEL).astype(BF16)


def _diag_blocks(d):
    d4 = d.reshape(RG_BLOCKS, RG_BLOCK_W, RG_BLOCKS, RG_BLOCK_W)
    return jnp.stack([d4[n, :, n, :] for n in range(RG_BLOCKS)], axis=0)


def _uq_full(g):
    p = jnp.pad(g, ((0, 0), (0, 0), (0, 2 * HEAD - HEAD - MLA_ROPE)))
    return jnp.transpose(p, (1, 0, 2)).reshape(MLA_LORA, MLA_HEADS * 2 * HEAD)


def _uq_shards(dw):
    return jnp.transpose(dw.reshape(MLA_LORA, MLA_HEADS, 2 * HEAD), (1, 0, 2))[:, :, :HEAD + MLA_ROPE]


SMALL = ("ffn1_norm", "mix_norm", "conv_b", "rg_w_a", "rg_b_a", "rg_w_x", "rg_b_x", "rg_lambda", "mla_q_norm",
         "mla_kv_norm", "ffn2_norm", "final_norm")
BIG = ("ffn1_w_gate_up", "ffn1_w_down", "w_in", "conv_w", "mla_w_uq", "mla_w_ukv", "w_branch_a", "w_branch_b",
       "w_branch_c", "w_out", "ffn2_w_gate_up", "ffn2_w_down")
ROW_SHARDED = ("ffn1_w_down", "w_branch_a", "w_branch_b", "w_branch_c", "w_out", "ffn2_w_down")


FIRST = ("ffn1_w_gate_up", "ffn1_w_down")
REST = tuple(k for k in BIG if k not in FIRST)


def _full_weights(g):
    fw = {}
    for k, s in g.items():
        if k in ROW_SHARDED:
            fw[k] = s.reshape(-1, s.shape[-1])
        elif k == "w_in":
            fw[k] = _w_in_to_z(_cols_full(s))
        elif k == "mla_w_uq":
            fw[k] = _uq_full(s)
        else:
            fw[k] = _cols_full(s)
    return fw


def _grad_shards(dw):
    out = {}
    for k, g in dw.items():
        if k in ROW_SHARDED:
            out[k] = g.reshape(N_DEV, g.shape[0] // N_DEV, g.shape[1])
        elif k == "w_in":
            out[k] = _cols_shards(_z_to_w_in(g))
        elif k == "mla_w_uq":
            out[k] = _uq_shards(g)
        else:
            out[k] = _cols_shards(g)
    return out


def ffn_fwd(x, norm, w_gu, w_d, tag):
    h = rms_fwd(x, norm, f"{tag}_rms")
    g, u, a = ffn_up(h, w_gu, f"{tag}_up")
    y = matmul(a, w_d, "nn", F32, f"{tag}_down", scale=0.5, res=x, tk=1408)
    return y, (x, h, g, u, a)


def ffn_bwd(dy, saved, norm, w_gu, w_d, tag):
    x, h, g, u, a = saved
    dg, du = ffn_dact(dy, w_d, g, u, f"{tag}_dact")
    dw_d = matmul(a, dy, "tn", BF16, f"{tag}_dwd", scale=0.5, tm=1408)
    dw_gu = jnp.concatenate([matmul(h, dg, "tn", BF16, f"{tag}_dwg", tn=1408), matmul(h, du, "tn", BF16, f"{tag}_dwu", tn=1408)], axis=1)
    dh = matmul(dg, w_gu, "nt", F32, f"{tag}_dhg", tk=1408)
    dh = matmul(du, w_gu, "nt", F32, f"{tag}_dhu", tk=1408, res=dh, b_koff=D_FF // 1408)
    dx, dnorm = rms_bwd(x, norm, dh, dy, f"{tag}_drms")
    return dx, dw_gu, dw_d, dnorm


def mixer_fwd(x, sp, fw, cs, sn, tag):
    h = rms_fwd(x, sp["mix_norm"], f"{tag}_rms")
    z = matmul(h, fw["w_in"], "nn", BF16, f"{tag}_in", tn=1280)
    wa, wx = _block_diag(sp["rg_w_a"]), _block_diag(sp["rg_w_x"])
    ya, hs = rglru_fwd(z, fw["conv_w"], sp["conv_b"], wa, wx, sp["rg_b_a"], sp["rg_b_x"], sp["rg_lambda"], f"{tag}_rg")
    yb, ltot = sb_fwd(z, f"{tag}_sb")
    cqn, ckvn, krope = mla_prep_fwd(z, sp["mla_q_norm"], sp["mla_kv_norm"], cs, sn, f"{tag}_mprep")
    q = q_rope(matmul(cqn, fw["mla_w_uq"], "nn", F32, f"{tag}_uq"), cs, sn, 1.0, f"{tag}_qrope")
    kv = matmul(ckvn, fw["mla_w_ukv"], "nn", BF16, f"{tag}_ukv")
    yc, lse = mla_fwd(q, kv, krope, f"{tag}_mla")
    pa = matmul(ya, fw["w_branch_a"], "nn", BF16, f"{tag}_pa")
    pb = matmul(yb, fw["w_branch_b"], "nn", BF16, f"{tag}_pb")
    pc = matmul(yc, fw["w_branch_c"], "nn", BF16, f"{tag}_pc")
    merged = merge_fwd(z, pa, pb, pc, f"{tag}_merge")
    y = matmul(merged, fw["w_out"], "nn", F32, f"{tag}_out", res=x)
    return y, (x, h, z, wa, wx, ya, hs, yb, ltot, cqn, ckvn, krope, q, kv, yc, lse, pa, pb, pc, merged)


def mixer_bwd(dy, saved, sp, fw, cs, sn, tag):
    x, h, z, wa, wx, ya, hs, yb, ltot, cqn, ckvn, krope, q, kv, yc, lse, pa, pb, pc, merged = saved
    S = x.shape[0]
    dw, ds = {}, {}
    dmerged = matmul(dy, fw["w_out"], "nt", BF16, f"{tag}_dmerged")
    dw["w_out"] = matmul(merged, dy, "tn", BF16, f"{tag}_dwout")
    dpa, dpb, dpc, dga, dgb, dgc = merge_bwd(z, pa, pb, pc, dmerged, f"{tag}_dmerge")
    dya = matmul(dpa, fw["w_branch_a"], "nt", BF16, f"{tag}_dya")
    dyb = matmul(dpb, fw["w_branch_b"], "nt", BF16, f"{tag}_dyb")
    dyc = matmul(dpc, fw["w_branch_c"], "nt", BF16, f"{tag}_dyc")
    dw["w_branch_a"] = matmul(ya, dpa, "tn", BF16, f"{tag}_dwa")
    dw["w_branch_b"] = matmul(yb, dpb, "tn", BF16, f"{tag}_dwb")
    dw["w_branch_c"] = matmul(yc, dpc, "tn", BF16, f"{tag}_dwc")
    drgx, drgg, dwa, dwx, dvec = rglru_bwd(z, hs, dya, fw["conv_w"], sp["conv_b"], wa, wx, sp["rg_b_a"], sp["rg_b_x"],
                                           sp["rg_lambda"], f"{tag}_drg")
    ds["rg_w_a"], ds["rg_w_x"] = _diag_blocks(dwa), _diag_blocks(dwx)
    ds["rg_b_a"], ds["rg_b_x"], ds["rg_lambda"], ds["conv_b"] = dvec[0], dvec[1], dvec[2], dvec[3]
    dw["conv_w"] = dvec[4:8]
    dsq, dsk, dsv = sb_bwd(z, ltot, dyb, f"{tag}_dsb")
    dqp, dkv, dkr = mla_bwd(q, kv, krope, yc, lse, dyc, cs, sn, f"{tag}_dmla")
    dw["mla_w_uq"] = matmul(cqn, dqp, "tn", BF16, f"{tag}_dwuq")
    dw["mla_w_ukv"] = matmul(ckvn, dkv, "tn", BF16, f"{tag}_dwukv")
    dcqn = matmul(dqp, fw["mla_w_uq"], "nt", F32, f"{tag}_dcqn")
    dckvn = matmul(dkv, fw["mla_w_ukv"], "nt", F32, f"{tag}_dckvn")
    dcq, dckv, dkrr, dqn, dkvn = mla_prep_bwd(z, sp["mla_q_norm"], sp["mla_kv_norm"], cs, sn, dcqn, dckvn, dkr, f"{tag}_dmprep")
    ds["mla_q_norm"], ds["mla_kv_norm"] = dqn[0], dkvn[0]
    dz = jnp.concatenate([drgx, drgg, dsq.astype(BF16), dsk.astype(BF16), dsv.astype(BF16), dcq, dckv, dga, dgb, dgc, dkrr,
                          jnp.zeros((S, Z_W - Z_KR - LANE), BF16)], axis=1)
    dw["w_in"] = matmul(h, dz, "tn", BF16, f"{tag}_dwin", tn=1280)
    dh = matmul(dz, fw["w_in"], "nt", F32, f"{tag}_dh", tk=1280)
    dx, dnorm = rms_bwd(x, sp["mix_norm"], dh, dy, f"{tag}_drms")
    ds["mix_norm"] = dnorm[0]
    return dx, dw, ds


def _rope_tables(positions):
    inv = ROPE_THETA ** (-jnp.arange(0, MLA_ROPE, 2, dtype=F32) / MLA_ROPE)
    ang = positions.astype(F32)[:, None] * inv
    zeros = jnp.zeros((positions.shape[0], LANE - MLA_ROPE), F32)
    cs = jnp.concatenate([jnp.cos(ang), jnp.cos(ang), zeros], axis=1)
    sn = jnp.concatenate([jnp.sin(ang), jnp.sin(ang), zeros], axis=1)
    return cs, sn


def local_step(x, positions, target, small, fetch, emit):
    L = small["ffn1_norm"].shape[0]
    cs, sn = _rope_tables(positions)
    row = lambda v: v.reshape(1, -1)
    saved = []
    for l in range(L):
        sp = {k: small[k][l] for k in SMALL if k != "final_norm"}
        sp = {k: (v if v.ndim == 3 else row(v)) for k, v in sp.items()}
        g, token = fetch(l, "first", x)
        fw = _full_weights(g)
        if token is not None:
            sp["ffn1_norm"] = sp["ffn1_norm"] + token[0:1, 0:1]
        x, s1 = ffn_fwd(x, sp["ffn1_norm"], fw["ffn1_w_gate_up"], fw["ffn1_w_down"], f"l{l}_f1")
        g, token = fetch(l, "rest", x)
        fw.update(_full_weights(g))
        if token is not None:
            sp["mix_norm"] = sp["mix_norm"] + token[0:1, 0:1]
        x, s2 = mixer_fwd(x, sp, fw, cs, sn, f"l{l}_mx")
        x, s3 = ffn_fwd(x, sp["ffn2_norm"], fw["ffn2_w_gate_up"], fw["ffn2_w_down"], f"l{l}_f2")
        saved.append((fw, sp, s1, s2, s3))
    loss, dx, dfinal = loss_head(x, row(small["final_norm"]), target, "loss_head")
    small_grads = [None] * L
    for l in reversed(range(L)):
        fw, sp, s1, s2, s3 = saved[l]
        dx, dgu2, dd2, dn2 = ffn_bwd(dx, s3, sp["ffn2_norm"], fw["ffn2_w_gate_up"], fw["ffn2_w_down"], f"l{l}_f2")
        dx, dw, ds = mixer_bwd(dx, s2, sp, fw, cs, sn, f"l{l}_mx")
        dw.update(ffn2_w_gate_up=dgu2, ffn2_w_down=dd2)
        dx = emit(l, "rest", _grad_shards(dw), dx)
        dx, dgu1, dd1, dn1 = ffn_bwd(dx, s1, sp["ffn1_norm"], fw["ffn1_w_gate_up"], fw["ffn1_w_down"], f"l{l}_f1")
        ds.update(ffn1_norm=dn1[0], ffn2_norm=dn2[0])
        small_grads[l] = ds
        dx = emit(l, "first", _grad_shards(dict(ffn1_w_gate_up=dgu1, ffn1_w_down=dd1)), dx)
    return loss[0, 0], dx, small_grads, dfinal[0]


def _pack_small(tree):
    flat = jnp.concatenate([tree[k].reshape(-1).astype(F32) for k in SMALL])
    rows = -(-flat.shape[0] // (SUBLANE * D_MODEL)) * SUBLANE
    return jnp.pad(flat, (0, rows * D_MODEL - flat.shape[0])).reshape(rows, D_MODEL)


def _unpack_small(buf, like):
    flat = buf.reshape(-1)
    out, off = {}, 0
    for k in SMALL:
        n = like[k].size
        out[k] = flat[off:off + n].reshape(like[k].shape)
        off += n
    return out


NAMES = ("ffn1_norm", "ffn1_w_gate_up", "ffn1_w_down", "mix_norm", "w_in", "conv_w", "conv_b", "rg_w_a", "rg_b_a", "rg_w_x",
         "rg_b_x", "rg_lambda", "mla_q_norm", "mla_w_uq", "mla_kv_norm", "mla_w_ukv", "w_branch_a", "w_branch_b",
         "w_branch_c", "w_out", "ffn2_norm", "ffn2_w_gate_up", "ffn2_w_down", "final_norm")


def kernel(x, positions, *rest):
    n = len(NAMES)
    w = dict(zip(NAMES, rest[:n]))
    target = rest[n]
    m = dict(zip(NAMES, rest[n + 1:2 * n + 1]))
    v = dict(zip(NAMES, rest[2 * n + 1:3 * n + 1]))
    L = w["ffn1_norm"].shape[0]
    me = _my_index()

    stages = [(0, FIRST), (0, REST)] + [(l, BIG) for l in range(1, L)]
    shard = lambda l, k: w[k][l] if k == "conv_w" else w[k][l].astype(BF16)

    pending, got = {}, {}

    def gather_start(s, after):
        l, names = stages[s]
        mine = [shard(l, k) for k in names]
        handle, token = exchange_start(mine, True, after, f"gather_start_{s}")
        pending[(l, names[0])] = (s, handle, mine)
        return token

    def fetch(l, part, after):
        token = None
        key = (l, FIRST[0] if part == "first" else REST[0])
        if key in pending:
            s, handle, mine = pending.pop(key)
            landed = exchange_wait(handle, True, after, f"gather_wait_{s}")
            if s + 1 < len(stages):
                token = gather_start(s + 1, landed[0])
            filled = [lax.dynamic_update_slice_in_dim(g, a[None], me, 0) for g, a in zip(landed, mine)]
            got.update({(l, k): a for k, a in zip(stages[s][1], filled)})
        return {k: got.pop((l, k)) for k in (FIRST if part == "first" else REST)}, token

    gather_start(0, x)

    flying, stash, res = [], {}, {}

    def land(after):
        s, handle, own = flying.pop()
        l, names = stages[s]
        landed = exchange_wait(handle, False, after, f"scatter_wait_{s}")
        for k, g, o in zip(names, landed, own):
            parts = lax.dynamic_update_slice_in_dim(g, o, me, 0)
            res[k] = adamw_layer(parts, w[k], m[k], v[k], l, res.get(k), f"adamw_{k}_{l}")
        return landed[0]

    def emit(l, part, gshards, dx):
        stash.update(gshards)
        s = next(i for i, (sl, names) in enumerate(stages) if sl == l and (names[0] == FIRST[0]) == (part == "first" or l > 0))
        if l > 0 and part == "rest":
            return dx
        send = [stash.pop(k) for k in stages[s][1]]
        own = [lax.dynamic_slice_in_dim(a, me, 1, 0) for a in send]
        after = land(dx) if flying else dx
        handle, token = exchange_start(send, False, after, f"scatter_start_{s}")
        flying.append((s, handle, own))
        return dx + token[0:1, 0:1]

    small = {k: w[k] for k in SMALL}
    loss, dx, small_grads, dfinal = local_step(x[0], positions[0], target[0], small, fetch, emit)
    loss = lax.psum(loss, ("x", "y", "c"))

    sg = {k: jnp.stack([small_grads[l][k] for l in range(L)], axis=0) for k in SMALL if k != "final_norm"}
    sg["final_norm"] = dfinal
    small_parts = exchange([_pack_small(sg)], True, "gather_small_grads")[0]
    land(small_parts)
    packed = adamw_sum(small_parts[:, None], _pack_small(small)[None], _pack_small({k: m[k] for k in SMALL})[None],
                       _pack_small({k: v[k] for k in SMALL})[None], "adamw_small")
    unpacked = [_unpack_small(p[0], small) for p in packed]
    for k in SMALL:
        res[k] = tuple(u[k] for u in unpacked)

    outs = [loss, dx[None]]
    for i in range(4):
        outs += [res[k][i] for k in NAMES]
    return tuple(outs)
```

```python
import functools
import math

import jax
import jax.numpy as jnp
from jax import lax
from jax.experimental import pallas as pl
from jax.experimental.pallas import tpu as pltpu

F32 = jnp.float32
BF16 = jnp.bfloat16

N_DEV = 8
D_MODEL = 1024
D_FF = 2816
NORM_EPS = 1e-6
RG_BLOCKS = 16
RG_BLOCK_W = 64
RG_C = 8.0
SB_HEADS = 8
HEAD = 128
MLA_HEADS = 8
MLA_LORA = 256
MLA_ROPE = 64
ROPE_THETA = 10000.0
CHUNK = 64
SB_SCALE = HEAD ** -0.5
MLA_SCALE = (HEAD + MLA_ROPE) ** -0.5
N_IN = 8768

Z_RGX, Z_RGG, Z_Q, Z_K, Z_V, Z_CQ, Z_CKV, Z_GATE, Z_KR, Z_W = 0, 1024, 2048, 3072, 4096, 5120, 5376, 5632, 8704, 8960

ADAM_LR, ADAM_B1, ADAM_B2, ADAM_EPS, ADAM_WD, ADAM_STEP = 0.001, 0.9, 0.999, 1e-08, 0.01, 10

LANE = 128
SUBLANE = 8
VMEM_LIMIT = 48 * 1024 * 1024
NEG = -1e30


def _pcall(body, **kw):
    return pl.pallas_call(body, **kw)


def _params(*sem):
    return pltpu.CompilerParams(dimension_semantics=sem or None, vmem_limit_bytes=VMEM_LIMIT)


def _pick(dim, target):
    best = None
    t = LANE
    while t <= min(dim, target):
        if dim % t == 0:
            best = t
        t += LANE
    return best if best is not None else dim


def _sigmoid(x):
    return 1.0 / (1.0 + jnp.exp(-x))


def _gelu_and_grad(x):
    c = math.sqrt(2.0 / math.pi)
    inner = c * (x + 0.044715 * x * x * x)
    t = jnp.tanh(inner)
    val = 0.5 * x * (1.0 + t)
    grad = 0.5 * (1.0 + t) + 0.5 * x * (1.0 - t * t) * c * (1.0 + 3.0 * 0.044715 * x * x)
    return val, grad


def _neg_expm1(y):
    series = -y * (1.0 + y * (0.5 + y * (1.0 / 6.0 + y * (1.0 / 24.0))))
    return jnp.where(jnp.abs(y) < 0.02, series, 1.0 - jnp.exp(y))


def _dot(a, b, dims):
    return lax.dot_general(a, b, (dims, ((), ())), preferred_element_type=F32)


NN = ((1,), (0,))
NT = ((1,), (1,))
TN = ((0,), (0,))


def matmul(a, b, mode, out_dtype, name, scale=1.0, res=None, tm=1024, tn=1024, tk=1024, b_koff=0):
    if mode == "nn":
        (M, K), N = a.shape, b.shape[1]
    elif mode == "nt":
        (M, K), N = a.shape, b.shape[0]
    else:
        (K, M), N = a.shape, b.shape[1]
    tm, tn, tk = _pick(M, tm), _pick(N, tn), _pick(K, tk)
    nk = K // tk
    dims = {"nn": NN, "nt": NT, "tn": TN}[mode]

    def body(*refs):
        if res is None:
            a_ref, b_ref, o_ref, acc = refs
        else:
            a_ref, b_ref, r_ref, o_ref, acc = refs
        k = pl.program_id(2)

        @pl.when(k == 0)
        def _():
            acc[...] = jnp.zeros_like(acc)

        acc[...] += _dot(a_ref[...].astype(BF16), b_ref[...].astype(BF16), dims)

        @pl.when(k == nk - 1)
        def _():
            r = acc[...] * scale
            if res is not None:
                r = r + r_ref[...]
            o_ref[...] = r.astype(out_dtype)

    a_spec = pl.BlockSpec((tk, tm), lambda i, j, k: (k, i)) if mode == "tn" else pl.BlockSpec((tm, tk), lambda i, j, k: (i, k))
    b_spec = pl.BlockSpec((tn, tk), lambda i, j, k: (j, k + b_koff)) if mode == "nt" else pl.BlockSpec((tk, tn), lambda i, j, k: (k, j))
    o_spec = pl.BlockSpec((tm, tn), lambda i, j, k: (i, j))
    in_specs = [a_spec, b_spec] + ([o_spec] if res is not None else [])
    args = (a, b) + ((res,) if res is not None else ())
    return _pcall(
        body, name=name, grid=(M // tm, N // tn, nk), in_specs=in_specs, out_specs=o_spec,
        out_shape=jax.ShapeDtypeStruct((M, N), out_dtype), scratch_shapes=[pltpu.VMEM((tm, tn), F32)],
        compiler_params=_params("parallel", "parallel", "arbitrary"),
    )(*args)


def rms_fwd(x, g, name, col=0):
    S, D = x.shape[0], g.shape[1]
    tr = _pick(S, 512)

    def body(x_ref, g_ref, o_ref):
        xv = x_ref[...].astype(F32)
        r = lax.rsqrt(jnp.mean(xv * xv, axis=-1, keepdims=True) + NORM_EPS)
        o_ref[...] = (xv * r * g_ref[...]).astype(BF16)

    return _pcall(
        body, name=name, grid=(S // tr,),
        in_specs=[pl.BlockSpec((tr, D), lambda i: (i, col)), pl.BlockSpec((1, D), lambda i: (0, 0))],
        out_specs=pl.BlockSpec((tr, D), lambda i: (i, 0)),
        out_shape=jax.ShapeDtypeStruct((S, D), BF16), compiler_params=_params("parallel"),
    )(x, g)


def _rms_bwd_math(xv, g, dh):
    r = lax.rsqrt(jnp.mean(xv * xv, axis=-1, keepdims=True) + NORM_EPS)
    xhat = xv * r
    dxhat = dh * g
    dx = r * (dxhat - xhat * jnp.mean(dxhat * xhat, axis=-1, keepdims=True))
    dg = jnp.sum(dh * xhat, axis=0, keepdims=True)
    return dx, dg


def rms_bwd(x, g, dh, dres, name):
    S, D = x.shape
    tr = _pick(S, 512)

    def body(x_ref, g_ref, dh_ref, dr_ref, dx_ref, dg_ref):
        dx, dg = _rms_bwd_math(x_ref[...], g_ref[...], dh_ref[...])
        dx_ref[...] = dx + dr_ref[...]

        @pl.when(pl.program_id(0) == 0)
        def _():
            dg_ref[...] = jnp.zeros_like(dg_ref)

        dg_ref[...] += dg

    row = pl.BlockSpec((tr, D), lambda i: (i, 0))
    vec = pl.BlockSpec((1, D), lambda i: (0, 0))
    return _pcall(
        body, name=name, grid=(S // tr,), in_specs=[row, vec, row, row], out_specs=[row, vec],
        out_shape=[jax.ShapeDtypeStruct((S, D), F32), jax.ShapeDtypeStruct((1, D), F32)],
        compiler_params=_params("arbitrary"),
    )(x, g, dh, dres)


def ffn_up(h, w_gu, name):
    S, D = h.shape
    tm, tn = _pick(S, 512), D_FF // 2
    nc = D_FF // tn

    def body(h_ref, wg_ref, wu_ref, g_ref, u_ref, a_ref):
        hv = h_ref[...]
        g = _dot(hv, wg_ref[...], NN)
        u = _dot(hv, wu_ref[...], NN)
        g_ref[...] = g.astype(BF16)
        u_ref[...] = u.astype(BF16)
        a_ref[...] = (g * _sigmoid(g) * u).astype(BF16)

    blk = pl.BlockSpec((tm, tn), lambda j, i: (i, j))
    out = jax.ShapeDtypeStruct((S, D_FF), BF16)
    return _pcall(
        body, name=name, grid=(nc, S // tm),
        in_specs=[pl.BlockSpec((tm, D), lambda j, i: (i, 0)), pl.BlockSpec((D, tn), lambda j, i: (0, j)),
                  pl.BlockSpec((D, tn), lambda j, i: (0, j + nc))],
        out_specs=[blk] * 3, out_shape=[out] * 3, compiler_params=_params("parallel", "parallel"),
    )(h, w_gu, w_gu)


def ffn_dact(dy, w_d, g, u, name):
    S, D = dy.shape
    tm, tn = _pick(S, 512), D_FF // 2
    nc = D_FF // tn

    def body(dy_ref, wd_ref, g_ref, u_ref, dg_ref, du_ref):
        da = _dot(dy_ref[...].astype(BF16), wd_ref[...], NT) * 0.5
        gv = g_ref[...].astype(F32)
        sg = _sigmoid(gv)
        dg_ref[...] = (da * u_ref[...].astype(F32) * sg * (1.0 + gv * (1.0 - sg))).astype(BF16)
        du_ref[...] = (da * gv * sg).astype(BF16)

    blk = pl.BlockSpec((tm, tn), lambda j, i: (i, j))
    out = jax.ShapeDtypeStruct((S, D_FF), BF16)
    return _pcall(
        body, name=name, grid=(nc, S // tm),
        in_specs=[pl.BlockSpec((tm, D), lambda j, i: (i, 0)), pl.BlockSpec((tn, D), lambda j, i: (j, 0)), blk, blk],
        out_specs=[blk] * 2, out_shape=[out] * 2, compiler_params=_params("parallel", "parallel"),
    )(dy, w_d, g, u)


def _conv_taps(xpad, T, cw, cb):
    u = cb + cw[3:4, :] * xpad[pl.ds(8, T), :]
    for tap in range(3):
        u = u + cw[tap:tap + 1, :] * xpad[pl.ds(5 + tap, T), :]
    return u


def _rg_gates(u, wa_ref, wx_ref, ba, bx, lam):
    ub = u.astype(BF16)
    r = _sigmoid(_dot(ub, wa_ref[...], NN) + ba)
    ig = _sigmoid(_dot(ub, wx_ref[...], NN) + bx)
    nlam = -lam
    clam = -RG_C * (jnp.maximum(nlam, 0.0) + jnp.log(1.0 + jnp.exp(-jnp.abs(nlam))))
    la = clam * r
    return r, ig, clam, la


def rglru_fwd(z, cw, cb, wa, wx, ba, bx, lam, name):
    S, D = z.shape[0], D_MODEL
    T = _pick(S, 256)

    def body(x_ref, g_ref, cw_ref, cb_ref, wa_ref, wx_ref, ba_ref, bx_ref, lam_ref, y_ref, h_ref, xpad, a_s, b_s, hst):
        @pl.when(pl.program_id(0) == 0)
        def _():
            xpad[pl.ds(0, 8), :] = jnp.zeros((8, D), F32)
            hst[...] = jnp.zeros_like(hst)

        xpad[pl.ds(8, T), :] = x_ref[...].astype(F32)
        u = _conv_taps(xpad, T, cw_ref[...], cb_ref[...])
        xpad[pl.ds(0, 8), :] = xpad[pl.ds(T, 8), :]
        r, ig, clam, la = _rg_gates(u, wa_ref, wx_ref, ba_ref[...], bx_ref[...], lam_ref[...])
        a_s[...] = jnp.exp(la)
        b_s[...] = jnp.sqrt(_neg_expm1(2.0 * la)) * (ig * u)

        def tile(j, h):
            r0 = pl.multiple_of(j * 8, 8)
            av = a_s[pl.ds(r0, 8), :]
            bv = b_s[pl.ds(r0, 8), :]
            rows = []
            for k in range(8):
                h = av[k:k + 1, :] * h + bv[k:k + 1, :]
                rows.append(h)
            h_ref[pl.ds(r0, 8), :] = jnp.concatenate(rows, axis=0)
            return h

        hst[...] = lax.fori_loop(0, T // 8, tile, hst[...])
        gel, _ = _gelu_and_grad(g_ref[...].astype(F32))
        y_ref[...] = (h_ref[...] * gel).astype(BF16)

    blk = lambda c: pl.BlockSpec((T, D), lambda i: (i, c))
    vec = pl.BlockSpec((1, D), lambda i: (0, 0))
    full = lambda r: pl.BlockSpec((r, D), lambda i: (0, 0))
    return _pcall(
        body, name=name, grid=(S // T,),
        in_specs=[blk(0), blk(1), full(4), vec, full(D), full(D), vec, vec, vec],
        out_specs=[blk(0), blk(0)],
        out_shape=[jax.ShapeDtypeStruct((S, D), BF16), jax.ShapeDtypeStruct((S, D), F32)],
        scratch_shapes=[pltpu.VMEM((T + 8, D), F32), pltpu.VMEM((T, D), F32), pltpu.VMEM((T, D), F32), pltpu.VMEM((1, D), F32)],
        compiler_params=_params("arbitrary"),
    )(z, z, cw, cb, wa, wx, ba, bx, lam)


def rglru_bwd(z, hs, dy, cw, cb, wa, wx, ba, bx, lam, name):
    S, D = z.shape[0], D_MODEL
    T = _pick(S, 256)
    nb = S // T
    t8 = T // 8

    def body(x_ref, xp_ref, g_ref, h_ref, hp_ref, dy_ref, cw_ref, cb_ref, wa_ref, wx_ref, ba_ref, bx_ref, lam_ref,
             dx_ref, dg_ref, dwa_ref, dwx_ref, dvec_ref, xpad, hpad, dupad, a_s, d_s, carry):
        i = pl.program_id(0)
        first_block = i == nb - 1

        @pl.when(i == 0)
        def _():
            dwa_ref[...] = jnp.zeros_like(dwa_ref)
            dwx_ref[...] = jnp.zeros_like(dwx_ref)
            dvec_ref[...] = jnp.zeros_like(dvec_ref)
            carry[...] = jnp.zeros_like(carry)
            dupad[pl.ds(T, 8), :] = jnp.zeros((8, D), F32)

        keep = jnp.where(first_block, 0.0, 1.0)
        xpad[pl.ds(0, 8), :] = xp_ref[...].astype(F32) * keep
        xpad[pl.ds(8, T), :] = x_ref[...].astype(F32)
        hpad[pl.ds(0, 8), :] = hp_ref[...] * keep
        hpad[pl.ds(8, T), :] = h_ref[...]
        cwv = cw_ref[...]
        u = _conv_taps(xpad, T, cwv, cb_ref[...])
        r, ig, clam, la = _rg_gates(u, wa_ref, wx_ref, ba_ref[...], bx_ref[...], lam_ref[...])
        a = jnp.exp(la)
        a_s[...] = a
        gv = g_ref[...].astype(F32)
        gel, dgel = _gelu_and_grad(gv)
        dyv = dy_ref[...].astype(F32)
        d_s[...] = dyv * gel
        dg_ref[...] = (dyv * h_ref[...] * dgel).astype(BF16)

        def tile(j, c):
            r0 = pl.multiple_of((t8 - 1 - j) * 8, 8)
            av = a_s[pl.ds(r0, 8), :]
            dv = d_s[pl.ds(r0, 8), :]
            rows = [None] * 8
            for k in range(7, -1, -1):
                d = dv[k:k + 1, :] + c
                rows[k] = d
                c = av[k:k + 1, :] * d
            d_s[pl.ds(r0, 8), :] = jnp.concatenate(rows, axis=0)
            return c

        carry[...] = lax.fori_loop(0, t8, tile, carry[...])
        dht = d_s[...]
        hprev = hpad[pl.ds(7, T), :]
        w = _neg_expm1(2.0 * la)
        s = jnp.sqrt(w)
        e2 = 1.0 - w
        d_iu = dht * s
        dla = dht * hprev * a - dht * (ig * u) * e2 / s
        dpr = (dla * clam * r * (1.0 - r))
        dpi = (d_iu * u * ig * (1.0 - ig))
        dprb, dpib, ub = dpr.astype(BF16), dpi.astype(BF16), u.astype(BF16)
        du = d_iu * ig + _dot(dprb, wa_ref[...], NT) + _dot(dpib, wx_ref[...], NT)
        dwa_ref[...] += _dot(ub, dprb, TN)
        dwx_ref[...] += _dot(ub, dpib, TN)
        dvec_ref[0:1, :] += jnp.sum(dpr, axis=0, keepdims=True)
        dvec_ref[1:2, :] += jnp.sum(dpi, axis=0, keepdims=True)
        dvec_ref[2:3, :] += jnp.sum(dla * r, axis=0, keepdims=True)
        dvec_ref[3:4, :] += jnp.sum(du, axis=0, keepdims=True)
        for tap in range(4):
            dvec_ref[4 + tap:5 + tap, :] += jnp.sum(du * xpad[pl.ds(5 + tap, T), :], axis=0, keepdims=True)
        dupad[pl.ds(0, T), :] = du
        dx = cwv[3:4, :] * du
        for tap in range(3):
            dx = dx + cwv[tap:tap + 1, :] * dupad[pl.ds(3 - tap, T), :]
        dx_ref[...] = dx.astype(BF16)
        dupad[pl.ds(T, 8), :] = dupad[pl.ds(0, 8), :]

        @pl.when(first_block)
        def _():
            dvec_ref[2:3, :] = dvec_ref[2:3, :] * (RG_C * _sigmoid(-lam_ref[...]))

    rev = lambda c: pl.BlockSpec((T, D), lambda i: (nb - 1 - i, c))
    prev = lambda c: pl.BlockSpec((8, D), lambda i: (jnp.maximum((nb - 1 - i) * t8 - 1, 0), c))
    vec = pl.BlockSpec((1, D), lambda i: (0, 0))
    full = lambda r: pl.BlockSpec((r, D), lambda i: (0, 0))
    return _pcall(
        body, name=name, grid=(nb,),
        in_specs=[rev(0), prev(0), rev(1), rev(0), prev(0), rev(0), full(4), vec, full(D), full(D), vec, vec, vec],
        out_specs=[rev(0), rev(0), full(D), full(D), full(8)],
        out_shape=[jax.ShapeDtypeStruct((S, D), BF16), jax.ShapeDtypeStruct((S, D), BF16),
                   jax.ShapeDtypeStruct((D, D), F32), jax.ShapeDtypeStruct((D, D), F32), jax.ShapeDtypeStruct((8, D), F32)],
        scratch_shapes=[pltpu.VMEM((T + 8, D), F32), pltpu.VMEM((T + 8, D), F32), pltpu.VMEM((T + 8, D), F32),
                        pltpu.VMEM((T, D), F32), pltpu.VMEM((T, D), F32), pltpu.VMEM((1, D), F32)],
        compiler_params=_params("arbitrary"),
    )(z, z, z, hs, hs, dy, cw, cb, wa, wx, ba, bx, lam)


def _tri(n, kind):
    j = lax.broadcasted_iota(jnp.int32, (n, n), 0)
    s = lax.broadcasted_iota(jnp.int32, (n, n), 1)
    m = {"gt": j > s, "le": j <= s, "lt": j < s}[kind]
    return jnp.where(m, 1.0, 0.0).astype(BF16)


def _dot2(x, tri):
    hi = x.astype(BF16)
    lo = (x - hi.astype(F32)).astype(BF16)
    return _dot(jnp.concatenate([hi, lo], axis=1), jnp.concatenate([tri, tri], axis=0), NN)


SB_TK = 128

def _sb_logits(q, kblk, q0, k0, tq, masked):
    z = _dot(q, kblk, NT) * SB_SCALE
    sp = jnp.maximum(z, 0.0) + jnp.log(1.0 + jnp.exp(-jnp.abs(z)))
    lkeep = -sp
    mask = None
    if masked:
        tpos = q0 + lax.broadcasted_iota(jnp.int32, (tq, SB_TK), 0)
        spos = k0 + lax.broadcasted_iota(jnp.int32, (tq, SB_TK), 1)
        mask = spos < tpos
        lkeep = jnp.where(mask, lkeep, 0.0)
    return mask, lkeep, z - sp


def sb_fwd(z, name):
    S = z.shape[0]
    tq, tk = _pick(S, 512), SB_TK
    nd = tq // tk
    U = min(4, nd)
    qc, kc, vc = Z_Q // HEAD, Z_K // HEAD, Z_V // HEAD

    def body(q_ref, k_ref, v_ref, o_ref, lt_ref, acc, run):
        qi = pl.program_id(1)
        q0 = qi * tq
        q = q_ref[...]
        tri = _tri(tk, "gt")
        acc[...] = jnp.zeros_like(acc)
        run[...] = jnp.zeros_like(run)

        def group(k0s, masked):
            parts = [(k0,) + _sb_logits(q, k_ref[pl.ds(k0, tk), :], q0, k0, tq, masked) for k0 in k0s]
            cums = [_dot2(p[2], tri) for p in parts]
            r, a = run[...], acc[...]
            for (k0, mask, lkeep, lbeta), cum in zip(parts, cums):
                w = jnp.exp(lbeta + cum + r)
                if masked:
                    w = jnp.where(mask, w, 0.0)
                a = a + _dot(w.astype(BF16), v_ref[pl.ds(k0, tk), :], NN)
                r = r + jnp.sum(lkeep, axis=1, keepdims=True)
            acc[...] = a
            run[...] = r

        for g in range(nd // U):
            group([pl.multiple_of(q0 + (nd - 1 - g * U - u) * tk, tk) for u in range(U)], True)

        def step(i, c):
            base = qi * nd - 1 - i * U
            group([pl.multiple_of((base - u) * tk, tk) for u in range(U)], False)
            return c

        lax.fori_loop(0, qi * nd // U, step, 0)
        o_ref[...] = acc[...].astype(BF16)
        lt_ref[0] = run[...]

    return _pcall(
        body, name=name, grid=(SB_HEADS, S // tq),
        in_specs=[pl.BlockSpec((tq, HEAD), lambda h, i: (i, qc + h)), pl.BlockSpec((S, HEAD), lambda h, i: (0, kc + h)),
                  pl.BlockSpec((S, HEAD), lambda h, i: (0, vc + h))],
        out_specs=[pl.BlockSpec((tq, HEAD), lambda h, i: (i, h)), pl.BlockSpec((1, tq, 1), lambda h, i: (h, i, 0))],
        out_shape=[jax.ShapeDtypeStruct((S, SB_HEADS * HEAD), BF16), jax.ShapeDtypeStruct((SB_HEADS, S, 1), F32)],
        scratch_shapes=[pltpu.VMEM((tq, HEAD), F32), pltpu.VMEM((tq, 1), F32)],
        compiler_params=_params("parallel", "parallel"),
    )(z, z, z)


def sb_bwd(z, ltot, dy, name):
    S = z.shape[0]
    tq, tk = _pick(S, 512), SB_TK
    nd = tq // tk
    U = min(4, nd)
    nkb = S // tk
    qc, kc, vc = Z_Q // HEAD, Z_K // HEAD, Z_V // HEAD

    def body(q_ref, k_ref, v_ref, lt_ref, do_ref, dq_ref, dk_ref, dv_ref, dq_s, run_l, run_g, dkT, dvT):
        qi = pl.program_id(1)
        q0 = qi * tq

        @pl.when(qi == 0)
        def _():
            dkT[...] = jnp.zeros_like(dkT)
            dvT[...] = jnp.zeros_like(dvT)

        q = q_ref[...]
        do = do_ref[...].astype(BF16)
        qT, doT = q.T, do.T
        ltv = lt_ref[0]
        tri_le, tri_lt = _tri(tk, "le"), _tri(tk, "lt")
        dq_s[...] = jnp.zeros_like(dq_s)
        run_l[...] = jnp.zeros_like(run_l)
        run_g[...] = jnp.zeros_like(run_g)

        def group(k0s, masked):
            parts = []
            for k0 in k0s:
                kblk = k_ref[pl.ds(k0, tk), :]
                mask, lkeep, lbeta = _sb_logits(q, kblk, q0, k0, tq, masked)
                parts.append((k0, kblk, mask, lkeep, lbeta, _dot(do, v_ref[pl.ds(k0, tk), :], NT)))
            pres = [_dot2(p[3], tri_le) for p in parts]
            rl = run_l[...]
            ws = []
            for (k0, kblk, mask, lkeep, lbeta, dw), pre in zip(parts, pres):
                w = jnp.exp(lbeta + (ltv - (pre + rl)))
                if masked:
                    w = jnp.where(mask, w, 0.0)
                ws.append((w, w * dw))
                rl = rl + jnp.sum(lkeep, axis=1, keepdims=True)
            run_l[...] = rl
            gpres = [_dot2(g, tri_lt) for _, g in ws]
            rg, dq = run_g[...], dq_s[...]
            for (k0, kblk, mask, lkeep, lbeta, dw), (w, g), gpre in zip(parts, ws, gpres):
                dz = (g * jnp.exp(lkeep) - jnp.exp(lbeta) * (gpre + rg)) * SB_SCALE
                if masked:
                    dz = jnp.where(mask, dz, 0.0)
                dz = dz.astype(BF16)
                dq = dq + _dot(dz, kblk, NN)
                kb = k0 // tk
                dkT[kb] += _dot(qT, dz, NN)
                dvT[kb] += _dot(doT, w.astype(BF16), NN)
                rg = rg + jnp.sum(g, axis=1, keepdims=True)
            run_g[...] = rg
            dq_s[...] = dq

        def step(i, c):
            group([pl.multiple_of((i * U + u) * tk, tk) for u in range(U)], False)
            return c

        lax.fori_loop(0, qi * nd // U, step, 0)
        for g in range(nd // U):
            group([pl.multiple_of(q0 + (g * U + u) * tk, tk) for u in range(U)], True)
        dq_ref[...] = dq_s[...].astype(BF16)

        @pl.when(qi == pl.num_programs(1) - 1)
        def _():
            def flush(kb, c):
                r0 = pl.multiple_of(kb * tk, tk)
                dk_ref[pl.ds(r0, tk), :] = dkT[kb].T.astype(BF16)
                dv_ref[pl.ds(r0, tk), :] = dvT[kb].T.astype(BF16)
                return c

            lax.fori_loop(0, nkb, flush, 0)

    qblk = lambda c: pl.BlockSpec((tq, HEAD), lambda h, i: (i, c + h))
    kfull = lambda c: pl.BlockSpec((S, HEAD), lambda h, i: (0, c + h))
    out = jax.ShapeDtypeStruct((S, SB_HEADS * HEAD), BF16)
    return _pcall(
        body, name=name, grid=(SB_HEADS, S // tq),
        in_specs=[qblk(qc), kfull(kc), kfull(vc), pl.BlockSpec((1, tq, 1), lambda h, i: (h, i, 0)), qblk(0)],
        out_specs=[qblk(0), kfull(0), kfull(0)], out_shape=[out, out, out],
        scratch_shapes=[pltpu.VMEM((tq, HEAD), F32), pltpu.VMEM((tq, 1), F32), pltpu.VMEM((tq, 1), F32),
                        pltpu.VMEM((nkb, HEAD, tk), F32), pltpu.VMEM((nkb, HEAD, tk), F32)],
        compiler_params=_params("arbitrary", "arbitrary"),
    )(z, z, z, ltot, dy)


def _rope(x, cs, sn, sign):
    lane = lax.broadcasted_iota(jnp.int32, x.shape, 1)
    swapped = jnp.where(lane < MLA_ROPE // 2, -pltpu.roll(x, LANE - MLA_ROPE // 2, 1), pltpu.roll(x, MLA_ROPE // 2, 1))
    return x * cs + sign * swapped * sn


def mla_prep_fwd(z, qn, kvn, cs, sn, name):
    S = z.shape[0]
    tr = _pick(S, 512)

    def body(cq_ref, ckv_ref, kr_ref, qn_ref, kvn_ref, cs_ref, sn_ref, oq_ref, okv_ref, okr_ref):
        for src, g, dst in ((cq_ref, qn_ref, oq_ref), (ckv_ref, kvn_ref, okv_ref)):
            xv = src[...].astype(F32)
            r = lax.rsqrt(jnp.mean(xv * xv, axis=-1, keepdims=True) + NORM_EPS)
            dst[...] = (xv * r * g[...]).astype(BF16)
        okr_ref[...] = _rope(kr_ref[...].astype(F32), cs_ref[...], sn_ref[...], 1.0).astype(BF16)

    lora = lambda c: pl.BlockSpec((tr, MLA_LORA), lambda i: (i, c))
    tile = lambda c: pl.BlockSpec((tr, LANE), lambda i: (i, c))
    vec = pl.BlockSpec((1, MLA_LORA), lambda i: (0, 0))
    return _pcall(
        body, name=name, grid=(S // tr,),
        in_specs=[lora(Z_CQ // MLA_LORA), lora(Z_CKV // MLA_LORA), tile(Z_KR // LANE), vec, vec, tile(0), tile(0)],
        out_specs=[lora(0), lora(0), tile(0)],
        out_shape=[jax.ShapeDtypeStruct((S, MLA_LORA), BF16), jax.ShapeDtypeStruct((S, MLA_LORA), BF16),
                   jax.ShapeDtypeStruct((S, LANE), BF16)],
        compiler_params=_params("parallel"),
    )(z, z, z, qn, kvn, cs, sn)


def mla_prep_bwd(z, qn, kvn, cs, sn, dcqn, dckvn, dkrope, name):
    S = z.shape[0]
    tr = _pick(S, 512)

    def body(cq_ref, ckv_ref, qn_ref, kvn_ref, cs_ref, sn_ref, dq_ref, dkv_ref, dkr_ref, oq_ref, okv_ref, okr_ref, gq_ref, gkv_ref):
        @pl.when(pl.program_id(0) == 0)
        def _():
            gq_ref[...] = jnp.zeros_like(gq_ref)
            gkv_ref[...] = jnp.zeros_like(gkv_ref)

        for src, g, dh, dst, gacc in ((cq_ref, qn_ref, dq_ref, oq_ref, gq_ref), (ckv_ref, kvn_ref, dkv_ref, okv_ref, gkv_ref)):
            dx, dg = _rms_bwd_math(src[...].astype(F32), g[...], dh[...])
            dst[...] = dx.astype(BF16)
            gacc[...] += dg
        okr_ref[...] = _rope(dkr_ref[...], cs_ref[...], sn_ref[...], -1.0).astype(BF16)

    lora = lambda c: pl.BlockSpec((tr, MLA_LORA), lambda i: (i, c))
    tile = lambda c: pl.BlockSpec((tr, LANE), lambda i: (i, c))
    vec = pl.BlockSpec((1, MLA_LORA), lambda i: (0, 0))
    return _pcall(
        body, name=name, grid=(S // tr,),
        in_specs=[lora(Z_CQ // MLA_LORA), lora(Z_CKV // MLA_LORA), vec, vec, tile(0), tile(0), lora(0), lora(0), tile(0)],
        out_specs=[lora(0), lora(0), tile(0), vec, vec],
        out_shape=[jax.ShapeDtypeStruct((S, MLA_LORA), BF16), jax.ShapeDtypeStruct((S, MLA_LORA), BF16),
                   jax.ShapeDtypeStruct((S, LANE), BF16), jax.ShapeDtypeStruct((1, MLA_LORA), F32), jax.ShapeDtypeStruct((1, MLA_LORA), F32)],
        compiler_params=_params("arbitrary"),
    )(z, z, qn, kvn, cs, sn, dcqn, dckvn, dkrope)


def q_rope(q, cs, sn, sign, name):
    S = q.shape[0]
    tr = _pick(S, 512)

    def body(q_ref, cs_ref, sn_ref, o_ref):
        o_ref[:, 0:LANE] = q_ref[:, 0:LANE].astype(BF16)
        o_ref[:, LANE:2 * LANE] = _rope(q_ref[:, LANE:2 * LANE], cs_ref[...], sn_ref[...], sign).astype(BF16)

    blk = pl.BlockSpec((tr, 2 * LANE), lambda i, h: (i, h))
    tile = pl.BlockSpec((tr, LANE), lambda i, h: (i, 0))
    return _pcall(
        body, name=name, grid=(S // tr, MLA_HEADS), in_specs=[blk, tile, tile], out_specs=blk,
        out_shape=jax.ShapeDtypeStruct(q.shape, BF16), compiler_params=_params("parallel", "parallel"),
    )(q, cs, sn)


def _chunk_mask(rows, cols):
    tch = lax.broadcasted_iota(jnp.int32, (rows, cols), 0) // CHUNK
    sch = lax.broadcasted_iota(jnp.int32, (rows, cols), 1) // CHUNK
    return sch <= tch


def _fill_kcat(kcat, kv_ref, kr_ref):
    kcat[:, 0:HEAD] = kv_ref[:, 0:HEAD]
    kcat[:, HEAD:2 * HEAD] = kr_ref[...]


def mla_fwd(q, kv, kr, name):
    S = q.shape[0]
    t = _pick(S, 512)

    def body(q_ref, kv_ref, kr_ref, o_ref, lse_ref, kcat, m_s, l_s, acc):
        qi = pl.program_id(1)

        @pl.when(qi == 0)
        def _():
            _fill_kcat(kcat, kv_ref, kr_ref)

        q = q_ref[...]
        m_s[...] = jnp.full_like(m_s, NEG)
        l_s[...] = jnp.zeros_like(l_s)
        acc[...] = jnp.zeros_like(acc)

        def block(k0, width, masked):
            s = _dot(q, kcat[pl.ds(k0, width), :], NT) * MLA_SCALE
            if masked:
                s = jnp.where(_chunk_mask(t, width), s, NEG)
            m = m_s[...]
            m2 = jnp.maximum(m, jnp.max(s, axis=1, keepdims=True))
            p = jnp.exp(s - m2)
            alpha = jnp.exp(m - m2)
            l_s[...] = alpha * l_s[...] + jnp.sum(p, axis=1, keepdims=True)
            acc[...] = alpha * acc[...] + _dot(p.astype(BF16), kv_ref[pl.ds(k0, width), HEAD:2 * HEAD], NN)
            m_s[...] = m2

        if S >= 2 * t:
            def step(i, c):
                block(pl.multiple_of(i * 2 * t, 2 * t), 2 * t, False)
                return c

            lax.fori_loop(0, qi // 2, step, 0)

            @pl.when(qi % 2 == 1)
            def _():
                block(pl.multiple_of((qi - 1) * t, t), t, False)

        block(pl.multiple_of(qi * t, t), t, True)
        o_ref[...] = (acc[...] / l_s[...]).astype(BF16)
        lse_ref[0] = m_s[...] + jnp.log(l_s[...])

    return _pcall(
        body, name=name, grid=(MLA_HEADS, S // t),
        in_specs=[pl.BlockSpec((t, 2 * HEAD), lambda h, i: (i, h)), pl.BlockSpec((S, 2 * HEAD), lambda h, i: (0, h)),
                  pl.BlockSpec((S, LANE), lambda h, i: (0, 0))],
        out_specs=[pl.BlockSpec((t, HEAD), lambda h, i: (i, h)), pl.BlockSpec((1, t, 1), lambda h, i: (h, i, 0))],
        out_shape=[jax.ShapeDtypeStruct((S, MLA_HEADS * HEAD), BF16), jax.ShapeDtypeStruct((MLA_HEADS, S, 1), F32)],
        scratch_shapes=[pltpu.VMEM((S, 2 * HEAD), BF16), pltpu.VMEM((t, 1), F32), pltpu.VMEM((t, 1), F32), pltpu.VMEM((t, HEAD), F32)],
        compiler_params=_params("arbitrary", "arbitrary"),
    )(q, kv, kr)


def mla_bwd(q, kv, kr, o, lse, do, cs, sn, name):
    S = q.shape[0]
    t = _pick(S, 512)
    nkb = S // t

    def body(q_ref, kv_ref, kr_ref, o_ref, lse_ref, do_ref, cs_ref, sn_ref, dq_ref, dkv_ref, dkr_ref, kcat, dq_s, dkT, dvT, dkrT):
        h, qi = pl.program_id(0), pl.program_id(1)

        @pl.when(qi == 0)
        def _():
            _fill_kcat(kcat, kv_ref, kr_ref)
            dkT[...] = jnp.zeros_like(dkT)
            dvT[...] = jnp.zeros_like(dvT)

        @pl.when((qi == 0) & (h == 0))
        def _():
            dkrT[...] = jnp.zeros_like(dkrT)

        q = q_ref[...]
        dov = do_ref[...].astype(F32)
        dob = dov.astype(BF16)
        qT, doT = q.T, dob.T
        delta = jnp.sum(dov * o_ref[...].astype(F32), axis=1, keepdims=True)
        lsev = lse_ref[0]
        dq_s[...] = jnp.zeros_like(dq_s)

        def block(kb, masked):
            k0 = pl.multiple_of(kb * t, t)
            kc = kcat[pl.ds(k0, t), :]
            p = jnp.exp(_dot(q, kc, NT) * MLA_SCALE - lsev)
            if masked:
                p = jnp.where(_chunk_mask(t, t), p, 0.0)
            ds = (p * (_dot(dob, kv_ref[pl.ds(k0, t), HEAD:2 * HEAD], NT) - delta) * MLA_SCALE).astype(BF16)
            dkT[kb] += _dot(qT, ds, NN)
            dvT[kb] += _dot(doT, p.astype(BF16), NN)
            dq_s[...] += _dot(ds, kc, NN)

        def step(kb, c):
            block(kb, False)
            return c

        lax.fori_loop(0, qi, step, 0)
        block(qi, True)
        dq_ref[:, 0:HEAD] = dq_s[:, 0:HEAD].astype(BF16)
        dq_ref[:, HEAD:2 * HEAD] = _rope(dq_s[:, HEAD:2 * HEAD], cs_ref[...], sn_ref[...], -1.0).astype(BF16)
        last_q = qi == pl.num_programs(1) - 1

        @pl.when(last_q)
        def _():
            def flush(kb, c):
                r0 = pl.multiple_of(kb * t, t)
                dkv_ref[pl.ds(r0, t), 0:HEAD] = dkT[kb, 0:HEAD, :].T
                dkv_ref[pl.ds(r0, t), HEAD:2 * HEAD] = dvT[kb].T
                dkrT[kb] += dkT[kb, HEAD:2 * HEAD, :]
                return c

            lax.fori_loop(0, nkb, flush, 0)

        @pl.when(last_q & (h == pl.num_programs(0) - 1))
        def _():
            def flush(kb, c):
                r0 = pl.multiple_of(kb * t, t)
                dkr_ref[pl.ds(r0, t), :] = dkrT[kb].T
                return c

            lax.fori_loop(0, nkb, flush, 0)

    qblk = pl.BlockSpec((t, 2 * HEAD), lambda h, i: (i, h))
    kvfull = pl.BlockSpec((S, 2 * HEAD), lambda h, i: (0, h))
    krfull = pl.BlockSpec((S, LANE), lambda h, i: (0, 0))
    oblk = pl.BlockSpec((t, HEAD), lambda h, i: (i, h))
    return _pcall(
        body, name=name, grid=(MLA_HEADS, S // t),
        in_specs=[qblk, kvfull, krfull, oblk, pl.BlockSpec((1, t, 1), lambda h, i: (h, i, 0)), oblk,
                  pl.BlockSpec((t, LANE), lambda h, i: (i, 0)), pl.BlockSpec((t, LANE), lambda h, i: (i, 0))],
        out_specs=[qblk, kvfull, krfull],
        out_shape=[jax.ShapeDtypeStruct(q.shape, BF16), jax.ShapeDtypeStruct(kv.shape, F32), jax.ShapeDtypeStruct((S, LANE), F32)],
        scratch_shapes=[pltpu.VMEM((S, 2 * HEAD), BF16), pltpu.VMEM((t, 2 * HEAD), F32), pltpu.VMEM((nkb, 2 * HEAD, t), F32),
                        pltpu.VMEM((nkb, HEAD, t), F32), pltpu.VMEM((nkb, LANE, t), F32)],
        compiler_params=_params("arbitrary", "arbitrary"),
    )(q, kv, kr, o, lse, do, cs, sn)


GATE_TC = 512


def merge_fwd(z, ya, yb, yc, name):
    S = z.shape[0]
    tr, tc = _pick(S, 512), GATE_TC
    g0 = Z_GATE // tc
    nc = D_MODEL // tc

    def body(ga_ref, gb_ref, gc_ref, ya_ref, yb_ref, yc_ref, o_ref):
        acc = None
        for g, y in ((ga_ref, ya_ref), (gb_ref, yb_ref), (gc_ref, yc_ref)):
            term = _sigmoid(g[...].astype(F32)) * y[...].astype(F32)
            acc = term if acc is None else acc + term
        o_ref[...] = acc.astype(BF16)

    gate = lambda b: pl.BlockSpec((tr, tc), lambda i, j: (i, g0 + b * nc + j))
    blk = pl.BlockSpec((tr, tc), lambda i, j: (i, j))
    return _pcall(
        body, name=name, grid=(S // tr, nc), in_specs=[gate(0), gate(1), gate(2), blk, blk, blk], out_specs=blk,
        out_shape=jax.ShapeDtypeStruct((S, D_MODEL), BF16), compiler_params=_params("parallel", "parallel"),
    )(z, z, z, ya, yb, yc)


def merge_bwd(z, ya, yb, yc, dm, name):
    S = z.shape[0]
    tr, tc = _pick(S, 512), GATE_TC
    g0 = Z_GATE // tc
    nc = D_MODEL // tc

    def body(ga_ref, gb_ref, gc_ref, ya_ref, yb_ref, yc_ref, dm_ref, da_ref, db_ref, dc_ref, dga_ref, dgb_ref, dgc_ref):
        dmv = dm_ref[...].astype(F32)
        for g, y, dy, dg in ((ga_ref, ya_ref, da_ref, dga_ref), (gb_ref, yb_ref, db_ref, dgb_ref), (gc_ref, yc_ref, dc_ref, dgc_ref)):
            sg = _sigmoid(g[...].astype(F32))
            dy[...] = (dmv * sg).astype(BF16)
            dg[...] = (dmv * y[...].astype(F32) * sg * (1.0 - sg)).astype(BF16)

    gate = lambda b: pl.BlockSpec((tr, tc), lambda i, j: (i, g0 + b * nc + j))
    blk = pl.BlockSpec((tr, tc), lambda i, j: (i, j))
    out = jax.ShapeDtypeStruct((S, D_MODEL), BF16)
    return _pcall(
        body, name=name, grid=(S // tr, nc), in_specs=[gate(0), gate(1), gate(2), blk, blk, blk, blk],
        out_specs=[blk] * 6, out_shape=[out] * 6, compiler_params=_params("parallel", "parallel"),
    )(z, z, z, ya, yb, yc, dm)


def loss_head(x, g, target, name):
    S, D = x.shape
    tr = _pick(S, 512)

    def body(x_ref, g_ref, t_ref, l_ref, dx_ref, dg_ref):
        @pl.when(pl.program_id(0) == 0)
        def _():
            l_ref[...] = jnp.zeros_like(l_ref)
            dg_ref[...] = jnp.zeros_like(dg_ref)

        xv, gv = x_ref[...], g_ref[...]
        r = lax.rsqrt(jnp.mean(xv * xv, axis=-1, keepdims=True) + NORM_EPS)
        diff = xv * r * gv - t_ref[...]
        l_ref[...] += 0.5 * jnp.sum(jnp.mean(diff * diff, axis=-1, keepdims=True), axis=0, keepdims=True)
        dx, dg = _rms_bwd_math(xv, gv, diff * (1.0 / D))
        dx_ref[...] = dx
        dg_ref[...] += dg

    row = pl.BlockSpec((tr, D), lambda i: (i, 0))
    vec = pl.BlockSpec((1, D), lambda i: (0, 0))
    return _pcall(
        body, name=name, grid=(S // tr,), in_specs=[row, vec, row],
        out_specs=[pl.BlockSpec((1, LANE), lambda i: (0, 0)), row, vec],
        out_shape=[jax.ShapeDtypeStruct((1, LANE), F32), jax.ShapeDtypeStruct((S, D), F32), jax.ShapeDtypeStruct((1, D), F32)],
        compiler_params=_params("arbitrary"),
    )(x, g, target)


def _peer(k, x, y, c):
    px = 1 - x if k & 4 else x
    py = 1 - y if k & 2 else y
    pc = 1 - c if k & 1 else c
    return (px, py, pc), 4 * px + 2 * py + pc


def exchange(arrs, gather, name):
    n = len(arrs)
    shapes = [((N_DEV,) + a.shape) if gather else a.shape for a in arrs]

    def body(*refs):
        ins, outs = refs[:n], refs[n:2 * n]
        send_sems, recv_sems, loc_sems = refs[2 * n:]
        x, y, c = lax.axis_index("x"), lax.axis_index("y"), lax.axis_index("c")
        me = 4 * x + 2 * y + c
        sends, recvs, locs = [], [], []
        for a in range(n):
            loc = pltpu.make_async_copy(ins[a] if gather else ins[a].at[me], outs[a].at[me], loc_sems.at[a])
            loc.start()
            locs.append(loc)
            for k in range(1, N_DEV):
                peer, pid = _peer(k, x, y, c)
                s = a * (N_DEV - 1) + k - 1
                src = ins[a] if gather else ins[a].at[pid]
                snd = pltpu.make_async_remote_copy(src_ref=src, dst_ref=outs[a].at[me], send_sem=send_sems.at[s],
                                                   recv_sem=recv_sems.at[s], device_id=peer, device_id_type=pl.DeviceIdType.MESH)
                snd.start()
                sends.append(snd)
                recvs.append(pltpu.make_async_remote_copy(src_ref=src, dst_ref=outs[a].at[pid], send_sem=send_sems.at[s],
                                                          recv_sem=recv_sems.at[s], device_id=peer, device_id_type=pl.DeviceIdType.MESH))
        for snd, rcv in zip(sends, recvs):
            snd.wait_send()
            rcv.wait_recv()
        for loc in locs:
            loc.wait()

    any_spec = pl.BlockSpec(memory_space=pl.ANY)
    outs = _pcall(
        body, name=name, in_specs=[any_spec] * n, out_specs=[any_spec] * n,
        out_shape=[jax.ShapeDtypeStruct(s, a.dtype) for s, a in zip(shapes, arrs)],
        scratch_shapes=[pltpu.SemaphoreType.DMA((n * (N_DEV - 1),)), pltpu.SemaphoreType.DMA((n * (N_DEV - 1),)),
                        pltpu.SemaphoreType.DMA((n,))],
        compiler_params=pltpu.CompilerParams(has_side_effects=True),
    )(*arrs)
    return list(outs)


_HBM = pl.BlockSpec(memory_space=pltpu.HBM)
_SEM = pl.BlockSpec(memory_space=pltpu.SEMAPHORE)
_EFFECT = pltpu.SideEffectType.DATAFLOW_SIDE_EFFECTING


def _peer_copies(srcs, lands, send_sems, recv_sems, gather):
    x, y, c = lax.axis_index("x"), lax.axis_index("y"), lax.axis_index("c")
    me = 4 * x + 2 * y + c
    out = []
    for a, (src, land) in enumerate(zip(srcs, lands)):
        for k in range(1, N_DEV):
            peer, pid = _peer(k, x, y, c)
            s = a * (N_DEV - 1) + k - 1
            mk = lambda dst: pltpu.make_async_remote_copy(
                src_ref=src if gather else src.at[pid], dst_ref=dst, send_sem=send_sems.at[s], recv_sem=recv_sems.at[s],
                device_id=peer, device_id_type=pl.DeviceIdType.MESH)
            out.append((mk(land.at[me]), mk(land.at[pid])))
    return out


def exchange_start(arrs, gather, after, name):
    n = len(arrs)
    nsem = n * (N_DEV - 1)
    lands = [lax.empty(((N_DEV,) + a.shape) if gather else a.shape, a.dtype) for a in arrs]

    def body(*refs):
        srcs, land_refs = refs[:n], refs[n:2 * n]
        send_sems, recv_sems = refs[2 * n + 1], refs[2 * n + 2]
        token = refs[-1]
        for snd, _ in _peer_copies(srcs, land_refs, send_sems, recv_sems, gather):
            snd.start()
        token[...] = jnp.zeros_like(token)

    hbm = lambda a: pltpu.HBM(a.shape, a.dtype)
    outs = _pcall(
        body, name=name, in_specs=[_HBM] * (2 * n) + [pl.BlockSpec(memory_space=pl.ANY)],
        out_specs=[_SEM, _SEM] + [_HBM] * (2 * n) + [pl.BlockSpec(memory_space=pltpu.VMEM)],
        out_shape=[pltpu.SemaphoreType.DMA((nsem,)), pltpu.SemaphoreType.DMA((nsem,))] + [hbm(a) for a in arrs]
        + [hbm(a) for a in lands] + [jax.ShapeDtypeStruct((SUBLANE, LANE), F32)],
        input_output_aliases={i: i + 2 for i in range(2 * n)},
        compiler_params=pltpu.CompilerParams(has_side_effects=_EFFECT),
    )(*[pltpu.with_memory_space_constraint(a, pltpu.HBM) for a in list(arrs) + lands], after)
    return (outs[0], outs[1], list(outs[2:2 + n]), list(outs[2 + n:2 + 2 * n])), outs[-1]


def exchange_wait(handle, gather, after, name):
    send_sems, recv_sems, srcs, lands = handle
    n = len(srcs)

    def body(*refs):
        src_refs, land_refs = refs[:n], refs[n:2 * n]
        for snd, rcv in _peer_copies(src_refs, land_refs, refs[2 * n], refs[2 * n + 1], gather):
            snd.wait_send()
            rcv.wait_recv()

    hbm = lambda a: pltpu.HBM(a.shape, a.dtype)
    outs = _pcall(
        body, name=name, in_specs=[_HBM] * (2 * n) + [_SEM, _SEM, pl.BlockSpec(memory_space=pl.ANY)],
        out_specs=[_HBM] * (2 * n), out_shape=[hbm(a) for a in srcs] + [hbm(a) for a in lands],
        input_output_aliases={i: i for i in range(2 * n)},
        compiler_params=pltpu.CompilerParams(has_side_effects=_EFFECT),
    )(*srcs, *lands, send_sems, recv_sems, after)
    return list(outs[n:])


def _my_index():
    return 4 * lax.axis_index("x") + 2 * lax.axis_index("y") + lax.axis_index("c")


def adamw_sum(parts, w, m, v, name):
    L, R, C = w.shape
    tr = R
    for cand in (512, 352, 256, 128, 64, 48, 32, 16, 8):
        if R % cand == 0 and cand * C * 4 <= 2 * 1024 * 1024:
            tr = cand
            break
    c1 = 1.0 - ADAM_B1 ** ADAM_STEP
    c2 = 1.0 - ADAM_B2 ** ADAM_STEP

    def body(p_ref, w_ref, m_ref, v_ref, g_ref, d_ref, nm_ref, nv_ref):
        g = p_ref[0, 0].astype(F32)
        for k in range(1, N_DEV):
            g = g + p_ref[k, 0].astype(F32)
        m2 = ADAM_B1 * m_ref[0] + (1.0 - ADAM_B1) * g
        v2 = ADAM_B2 * v_ref[0] + (1.0 - ADAM_B2) * (g * g)
        g_ref[0] = g
        nm_ref[0] = m2
        nv_ref[0] = v2
        d_ref[0] = -ADAM_LR * ((m2 / c1) / (jnp.sqrt(v2 / c2) + ADAM_EPS) + ADAM_WD * w_ref[0])

    blk = pl.BlockSpec((1, tr, C), lambda l, i: (l, i, 0))
    out = jax.ShapeDtypeStruct((L, R, C), F32)
    return _pcall(
        body, name=name, grid=(L, R // tr),
        in_specs=[pl.BlockSpec((N_DEV, 1, tr, C), lambda l, i: (0, l, i, 0)), blk, blk, blk],
        out_specs=[blk] * 4, out_shape=[out] * 4, compiler_params=_params("parallel", "parallel"),
    )(parts, w, m, v)


def adamw_layer(parts, w, m, v, layer, prev, name):
    L, R, C = w.shape
    tr = R
    for cand in (512, 352, 256, 128, 64, 48, 32, 16, 8):
        if R % cand == 0 and cand * C * 4 <= 2 * 1024 * 1024:
            tr = cand
            break
    c1 = 1.0 - ADAM_B1 ** ADAM_STEP
    c2 = 1.0 - ADAM_B2 ** ADAM_STEP
    n_prev = 0 if prev is None else 4

    def body(*refs):
        p_ref, w_ref, m_ref, v_ref = refs[:4]
        g_ref, d_ref, nm_ref, nv_ref = refs[4 + n_prev:]
        g = p_ref[0].astype(F32)
        for k in range(1, N_DEV):
            g = g + p_ref[k].astype(F32)
        m2 = ADAM_B1 * m_ref[0] + (1.0 - ADAM_B1) * g
        v2 = ADAM_B2 * v_ref[0] + (1.0 - ADAM_B2) * (g * g)
        g_ref[0] = g
        nm_ref[0] = m2
        nv_ref[0] = v2
        d_ref[0] = -ADAM_LR * ((m2 / c1) / (jnp.sqrt(v2 / c2) + ADAM_EPS) + ADAM_WD * w_ref[0])

    blk = pl.BlockSpec((1, tr, C), lambda i: (layer, i, 0))
    out = jax.ShapeDtypeStruct((L, R, C), F32)
    return _pcall(
        body, name=name, grid=(R // tr,),
        in_specs=[pl.BlockSpec((N_DEV, tr, C), lambda i: (0, i, 0)), blk, blk, blk] + [pl.BlockSpec(memory_space=pl.ANY)] * n_prev,
        out_specs=[blk] * 4, out_shape=[out] * 4, input_output_aliases={4 + j: j for j in range(n_prev)},
        compiler_params=_params("parallel"),
    )(parts, w, m, v, *(prev or ()))


def _cols_full(g):
    return jnp.transpose(g, (1, 0, 2)).reshape(g.shape[1], N_DEV * g.shape[2])


def _cols_shards(w):
    R = w.shape[0]
    return jnp.transpose(w.reshape(R, N_DEV, w.shape[1] // N_DEV), (1, 0, 2))


def _w_in_to_z(w):
    kr0 = Z_GATE
    gate0 = Z_GATE + MLA_ROPE
    pad = jnp.zeros((w.shape[0], Z_W - Z_KR - MLA_ROPE), w.dtype)
    return jnp.concatenate([w[:, :kr0], w[:, gate0:], w[:, kr0:gate0], pad], axis=1)


def _z_to_w_in(dw):
    return jnp.concatenate([dw[:, :Z_GATE], dw[:, Z_KR:Z_KR + MLA_ROPE], dw[:, Z_GATE:Z_KR]], axis=1)


def _block_diag(w):
    eye = jnp.eye(RG_BLOCKS, dtype=w.dtype)
    return (w[:, :, None, :] * eye[:, None, :, None]).reshape(D_MODEL, D_MODEL).astype(BF16)


def _diag_blocks(d):
    d4 = d.reshape(RG_BLOCKS, RG_BLOCK_W, RG_BLOCKS, RG_BLOCK_W)
    return jnp.stack([d4[n, :, n, :] for n in range(RG_BLOCKS)], axis=0)


def _uq_full(g):
    p = jnp.pad(g, ((0, 0), (0, 0), (0, 2 * HEAD - HEAD - MLA_ROPE)))
    return jnp.transpose(p, (1, 0, 2)).reshape(MLA_LORA, MLA_HEADS * 2 * HEAD)


def _uq_shards(dw):
    return jnp.transpose(dw.reshape(MLA_LORA, MLA_HEADS, 2 * HEAD), (1, 0, 2))[:, :, :HEAD + MLA_ROPE]


SMALL = ("ffn1_norm", "mix_norm", "conv_b", "rg_w_a", "rg_b_a", "rg_w_x", "rg_b_x", "rg_lambda", "mla_q_norm",
         "mla_kv_norm", "ffn2_norm", "final_norm")
BIG = ("ffn1_w_gate_up", "ffn1_w_down", "w_in", "conv_w", "mla_w_uq", "mla_w_ukv", "w_branch_a", "w_branch_b",
       "w_branch_c", "w_out", "ffn2_w_gate_up", "ffn2_w_down")
ROW_SHARDED = ("ffn1_w_down", "w_branch_a", "w_branch_b", "w_branch_c", "w_out", "ffn2_w_down")


FIRST = ("ffn1_w_gate_up", "ffn1_w_down")
REST = tuple(k for k in BIG if k not in FIRST)


def _full_weights(g):
    fw = {}
    for k, s in g.items():
        if k in ROW_SHARDED:
            fw[k] = s.reshape(-1, s.shape[-1])
        elif k == "w_in":
            fw[k] = _w_in_to_z(_cols_full(s))
        elif k == "mla_w_uq":
            fw[k] = _uq_full(s)
        else:
            fw[k] = _cols_full(s)
    return fw


def _grad_shards(dw):
    out = {}
    for k, g in dw.items():
        if k in ROW_SHARDED:
            out[k] = g.reshape(N_DEV, g.shape[0] // N_DEV, g.shape[1])
        elif k == "w_in":
            out[k] = _cols_shards(_z_to_w_in(g))
        elif k == "mla_w_uq":
            out[k] = _uq_shards(g)
        else:
            out[k] = _cols_shards(g)
    return out


def ffn_fwd(x, norm, w_gu, w_d, tag):
    h = rms_fwd(x, norm, f"{tag}_rms")
    g, u, a = ffn_up(h, w_gu, f"{tag}_up")
    y = matmul(a, w_d, "nn", F32, f"{tag}_down", scale=0.5, res=x, tk=1408)
    return y, (x, h, g, u, a)


def ffn_bwd(dy, saved, norm, w_gu, w_d, tag):
    x, h, g, u, a = saved
    dg, du = ffn_dact(dy, w_d, g, u, f"{tag}_dact")
    dw_d = matmul(a, dy, "tn", BF16, f"{tag}_dwd", scale=0.5, tm=1408)
    dw_gu = jnp.concatenate([matmul(h, dg, "tn", BF16, f"{tag}_dwg", tn=1408), matmul(h, du, "tn", BF16, f"{tag}_dwu", tn=1408)], axis=1)
    dh = matmul(dg, w_gu, "nt", F32, f"{tag}_dhg", tk=1408)
    dh = matmul(du, w_gu, "nt", F32, f"{tag}_dhu", tk=1408, res=dh, b_koff=D_FF // 1408)
    dx, dnorm = rms_bwd(x, norm, dh, dy, f"{tag}_drms")
    return dx, dw_gu, dw_d, dnorm


def mixer_fwd(x, sp, fw, cs, sn, tag):
    h = rms_fwd(x, sp["mix_norm"], f"{tag}_rms")
    z = matmul(h, fw["w_in"], "nn", BF16, f"{tag}_in", tn=1280)
    wa, wx = _block_diag(sp["rg_w_a"]), _block_diag(sp["rg_w_x"])
    ya, hs = rglru_fwd(z, fw["conv_w"], sp["conv_b"], wa, wx, sp["rg_b_a"], sp["rg_b_x"], sp["rg_lambda"], f"{tag}_rg")
    yb, ltot = sb_fwd(z, f"{tag}_sb")
    cqn, ckvn, krope = mla_prep_fwd(z, sp["mla_q_norm"], sp["mla_kv_norm"], cs, sn, f"{tag}_mprep")
    q = q_rope(matmul(cqn, fw["mla_w_uq"], "nn", F32, f"{tag}_uq"), cs, sn, 1.0, f"{tag}_qrope")
    kv = matmul(ckvn, fw["mla_w_ukv"], "nn", BF16, f"{tag}_ukv")
    yc, lse = mla_fwd(q, kv, krope, f"{tag}_mla")
    pa = matmul(ya, fw["w_branch_a"], "nn", BF16, f"{tag}_pa")
    pb = matmul(yb, fw["w_branch_b"], "nn", BF16, f"{tag}_pb")
    pc = matmul(yc, fw["w_branch_c"], "nn", BF16, f"{tag}_pc")
    merged = merge_fwd(z, pa, pb, pc, f"{tag}_merge")
    y = matmul(merged, fw["w_out"], "nn", F32, f"{tag}_out", res=x)
    return y, (x, h, z, wa, wx, ya, hs, yb, ltot, cqn, ckvn, krope, q, kv, yc, lse, pa, pb, pc, merged)


def mixer_bwd(dy, saved, sp, fw, cs, sn, tag):
    x, h, z, wa, wx, ya, hs, yb, ltot, cqn, ckvn, krope, q, kv, yc, lse, pa, pb, pc, merged = saved
    S = x.shape[0]
    dw, ds = {}, {}
    dmerged = matmul(dy, fw["w_out"], "nt", BF16, f"{tag}_dmerged")
    dw["w_out"] = matmul(merged, dy, "tn", BF16, f"{tag}_dwout")
    dpa, dpb, dpc, dga, dgb, dgc = merge_bwd(z, pa, pb, pc, dmerged, f"{tag}_dmerge")
    dya = matmul(dpa, fw["w_branch_a"], "nt", BF16, f"{tag}_dya")
    dyb = matmul(dpb, fw["w_branch_b"], "nt", BF16, f"{tag}_dyb")
    dyc = matmul(dpc, fw["w_branch_c"], "nt", BF16, f"{tag}_dyc")
    dw["w_branch_a"] = matmul(ya, dpa, "tn", BF16, f"{tag}_dwa")
    dw["w_branch_b"] = matmul(yb, dpb, "tn", BF16, f"{tag}_dwb")
    dw["w_branch_c"] = matmul(yc, dpc, "tn", BF16, f"{tag}_dwc")
    drgx, drgg, dwa, dwx, dvec = rglru_bwd(z, hs, dya, fw["conv_w"], sp["conv_b"], wa, wx, sp["rg_b_a"], sp["rg_b_x"],
                                           sp["rg_lambda"], f"{tag}_drg")
    ds["rg_w_a"], ds["rg_w_x"] = _diag_blocks(dwa), _diag_blocks(dwx)
    ds["rg_b_a"], ds["rg_b_x"], ds["rg_lambda"], ds["conv_b"] = dvec[0], dvec[1], dvec[2], dvec[3]
    dw["conv_w"] = dvec[4:8]
    dsq, dsk, dsv = sb_bwd(z, ltot, dyb, f"{tag}_dsb")
    dqp, dkv, dkr = mla_bwd(q, kv, krope, yc, lse, dyc, cs, sn, f"{tag}_dmla")
    dw["mla_w_uq"] = matmul(cqn, dqp, "tn", BF16, f"{tag}_dwuq")
    dw["mla_w_ukv"] = matmul(ckvn, dkv, "tn", BF16, f"{tag}_dwukv")
    dcqn = matmul(dqp, fw["mla_w_uq"], "nt", F32, f"{tag}_dcqn")
    dckvn = matmul(dkv, fw["mla_w_ukv"], "nt", F32, f"{tag}_dckvn")
    dcq, dckv, dkrr, dqn, dkvn = mla_prep_bwd(z, sp["mla_q_norm"], sp["mla_kv_norm"], cs, sn, dcqn, dckvn, dkr, f"{tag}_dmprep")
    ds["mla_q_norm"], ds["mla_kv_norm"] = dqn[0], dkvn[0]
    dz = jnp.concatenate([drgx, drgg, dsq.astype(BF16), dsk.astype(BF16), dsv.astype(BF16), dcq, dckv, dga, dgb, dgc, dkrr,
                          jnp.zeros((S, Z_W - Z_KR - LANE), BF16)], axis=1)
    dw["w_in"] = matmul(h, dz, "tn", BF16, f"{tag}_dwin", tn=1280)
    dh = matmul(dz, fw["w_in"], "nt", F32, f"{tag}_dh", tk=1280)
    dx, dnorm = rms_bwd(x, sp["mix_norm"], dh, dy, f"{tag}_drms")
    ds["mix_norm"] = dnorm[0]
    return dx, dw, ds


def _rope_tables(positions):
    inv = ROPE_THETA ** (-jnp.arange(0, MLA_ROPE, 2, dtype=F32) / MLA_ROPE)
    ang = positions.astype(F32)[:, None] * inv
    zeros = jnp.zeros((positions.shape[0], LANE - MLA_ROPE), F32)
    cs = jnp.concatenate([jnp.cos(ang), jnp.cos(ang), zeros], axis=1)
    sn = jnp.concatenate([jnp.sin(ang), jnp.sin(ang), zeros], axis=1)
    return cs, sn


def local_step(x, positions, target, small, fetch, emit):
    L = small["ffn1_norm"].shape[0]
    cs, sn = _rope_tables(positions)
    row = lambda v: v.reshape(1, -1)
    saved = []
    for l in range(L):
        sp = {k: small[k][l] for k in SMALL if k != "final_norm"}
        sp = {k: (v if v.ndim == 3 else row(v)) for k, v in sp.items()}
        g, token = fetch(l, "first", x)
        fw = _full_weights(g)
        if token is not None:
            sp["ffn1_norm"] = sp["ffn1_norm"] + token[0:1, 0:1]
        x, s1 = ffn_fwd(x, sp["ffn1_norm"], fw["ffn1_w_gate_up"], fw["ffn1_w_down"], f"l{l}_f1")
        g, token = fetch(l, "rest", x)
        fw.update(_full_weights(g))
        if token is not None:
            sp["mix_norm"] = sp["mix_norm"] + token[0:1, 0:1]
        x, s2 = mixer_fwd(x, sp, fw, cs, sn, f"l{l}_mx")
        x, s3 = ffn_fwd(x, sp["ffn2_norm"], fw["ffn2_w_gate_up"], fw["ffn2_w_down"], f"l{l}_f2")
        saved.append((fw, sp, s1, s2, s3))
    loss, dx, dfinal = loss_head(x, row(small["final_norm"]), target, "loss_head")
    for l in reversed(range(L)):
        fw, sp, s1, s2, s3 = saved[l]
        dx, dgu2, dd2, dn2 = ffn_bwd(dx, s3, sp["ffn2_norm"], fw["ffn2_w_gate_up"], fw["ffn2_w_down"], f"l{l}_f2")
        dx, dw, ds = mixer_bwd(dx, s2, sp, fw, cs, sn, f"l{l}_mx")
        dw.update(ffn2_w_gate_up=dgu2, ffn2_w_down=dd2)
        dx = emit(l, "rest", _grad_shards(dw), dx)
        dx, dgu1, dd1, dn1 = ffn_bwd(dx, s1, sp["ffn1_norm"], fw["ffn1_w_gate_up"], fw["ffn1_w_down"], f"l{l}_f1")
        ds.update(ffn1_norm=dn1[0], ffn2_norm=dn2[0])
        if l == L - 1:
            ds["final_norm"] = dfinal[0]
        dx = emit(l, "first", _grad_shards(dict(ffn1_w_gate_up=dgu1, ffn1_w_down=dd1)), dx, ds)
    return loss[0, 0], dx


def _pack_small(tree):
    flat = jnp.concatenate([tree[k].reshape(-1).astype(F32) for k in SMALL])
    rows = -(-flat.shape[0] // (SUBLANE * D_MODEL)) * SUBLANE
    return jnp.pad(flat, (0, rows * D_MODEL - flat.shape[0])).reshape(rows, D_MODEL)


SMALL_LAYER = tuple(k for k in SMALL if k != "final_norm")


def _pack_rows(flat):
    rows = -(-flat.shape[-1] // (SUBLANE * D_MODEL)) * SUBLANE
    pad = [(0, 0)] * (flat.ndim - 1) + [(0, rows * D_MODEL - flat.shape[-1])]
    return jnp.pad(flat, pad).reshape(flat.shape[:-1] + (rows, D_MODEL))


def _pack_layer_small(ds):
    tail = ds.get("final_norm", jnp.zeros((D_MODEL,), F32))
    return _pack_rows(jnp.concatenate([ds[k].reshape(-1).astype(F32) for k in SMALL_LAYER] + [tail]))


def _small_parts(landed, like):
    L = len(landed)
    flats = [p.reshape(N_DEV, -1) for p in landed]
    pieces, off = {}, 0
    for k in SMALL_LAYER:
        n = like[k][0].size
        pieces[k] = jnp.concatenate([f[:, off:off + n] for f in flats], axis=1)
        off += n
    pieces["final_norm"] = flats[L - 1][:, off:off + D_MODEL]
    return _pack_rows(jnp.concatenate([pieces[k] for k in SMALL], axis=1))


def _unpack_small(buf, like):
    flat = buf.reshape(-1)
    out, off = {}, 0
    for k in SMALL:
        n = like[k].size
        out[k] = flat[off:off + n].reshape(like[k].shape)
        off += n
    return out


NAMES = ("ffn1_norm", "ffn1_w_gate_up", "ffn1_w_down", "mix_norm", "w_in", "conv_w", "conv_b", "rg_w_a", "rg_b_a", "rg_w_x",
         "rg_b_x", "rg_lambda", "mla_q_norm", "mla_w_uq", "mla_kv_norm", "mla_w_ukv", "w_branch_a", "w_branch_b",
         "w_branch_c", "w_out", "ffn2_norm", "ffn2_w_gate_up", "ffn2_w_down", "final_norm")


def kernel(x, positions, *rest):
    n = len(NAMES)
    w = dict(zip(NAMES, rest[:n]))
    target = rest[n]
    m = dict(zip(NAMES, rest[n + 1:2 * n + 1]))
    v = dict(zip(NAMES, rest[2 * n + 1:3 * n + 1]))
    L = w["ffn1_norm"].shape[0]
    me = _my_index()

    stages = [(0, FIRST), (0, REST)] + [(l, BIG) for l in range(1, L)]
    shard = lambda l, k: w[k][l] if k == "conv_w" else w[k][l].astype(BF16)

    pending, got = {}, {}

    def gather_start(s, after):
        l, names = stages[s]
        mine = [shard(l, k) for k in names]
        handle, token = exchange_start(mine, True, after, f"gather_start_{s}")
        pending[(l, names[0])] = (s, handle, mine)
        return token

    def fetch(l, part, after):
        token = None
        key = (l, FIRST[0] if part == "first" else REST[0])
        if key in pending:
            s, handle, mine = pending.pop(key)
            landed = exchange_wait(handle, True, after, f"gather_wait_{s}")
            if s + 1 < len(stages):
                token = gather_start(s + 1, landed[0])
            filled = [lax.dynamic_update_slice_in_dim(g, a[None], me, 0) for g, a in zip(landed, mine)]
            got.update({(l, k): a for k, a in zip(stages[s][1], filled)})
        return {k: got.pop((l, k)) for k in (FIRST if part == "first" else REST)}, token

    gather_start(0, x)

    flying, stash, res, small_landed = [], {}, {}, [None] * L

    def land(after):
        s, handle, own = flying.pop()
        l, names = stages[s]
        landed = exchange_wait(handle, False, after, f"scatter_wait_{s}")
        for k, g, o in zip(names, landed, own):
            parts = lax.dynamic_update_slice_in_dim(g, o, me, 0)
            res[k] = adamw_layer(parts, w[k], m[k], v[k], l, res.get(k), f"adamw_{k}_{l}")
        if len(landed) > len(names):
            small_landed[l] = lax.dynamic_update_slice_in_dim(landed[-1], own[-1], me, 0)
        return landed[0]

    def emit(l, part, gshards, dx, small_grads=None):
        stash.update(gshards)
        s = next(i for i, (sl, names) in enumerate(stages) if sl == l and (names[0] == FIRST[0]) == (part == "first" or l > 0))
        if l > 0 and part == "rest":
            return dx
        send = [stash.pop(k) for k in stages[s][1]]
        own = [lax.dynamic_slice_in_dim(a, me, 1, 0) for a in send]
        if small_grads is not None:
            pack = _pack_layer_small(small_grads)
            send.append(jnp.broadcast_to(pack[None], (N_DEV,) + pack.shape))
            own.append(pack[None])
        after = land(dx) if flying else dx
        handle, token = exchange_start(send, False, after, f"scatter_start_{s}")
        flying.append((s, handle, own))
        return dx + token[0:1, 0:1]

    small = {k: w[k] for k in SMALL}
    loss, dx = local_step(x[0], positions[0], target[0], small, fetch, emit)
    loss = lax.psum(loss, ("x", "y", "c"))
    land(dx)
    small_parts = _small_parts(small_landed, small)
    packed = adamw_sum(small_parts[:, None], _pack_small(small)[None], _pack_small({k: m[k] for k in SMALL})[None],
                       _pack_small({k: v[k] for k in SMALL})[None], "adamw_small")
    unpacked = [_unpack_small(p[0], small) for p in packed]
    for k in SMALL:
        res[k] = tuple(u[k] for u in unpacked)

    outs = [loss, dx[None]]
    for i in range(4):
        outs += [res[k][i] for k in NAMES]
    return tuple(outs)
```

```python
import functools
import math

import jax
import jax.numpy as jnp
from jax import lax
from jax.experimental import pallas as pl
from jax.experimental.pallas import tpu as pltpu

F32 = jnp.float32
BF16 = jnp.bfloat16

N_DEV = 8
D_MODEL = 1024
D_FF = 2816
NORM_EPS = 1e-6
RG_BLOCKS = 16
RG_BLOCK_W = 64
RG_C = 8.0
SB_HEADS = 8
HEAD = 128
MLA_HEADS = 8
MLA_LORA = 256
MLA_ROPE = 64
ROPE_THETA = 10000.0
CHUNK = 64
SB_SCALE = HEAD ** -0.5
MLA_SCALE = (HEAD + MLA_ROPE) ** -0.5
N_IN = 8768

Z_RGX, Z_RGG, Z_Q, Z_K, Z_V, Z_CQ, Z_CKV, Z_GATE, Z_KR, Z_W = 0, 1024, 2048, 3072, 4096, 5120, 5376, 5632, 8704, 8960

ADAM_LR, ADAM_B1, ADAM_B2, ADAM_EPS, ADAM_WD, ADAM_STEP = 0.001, 0.9, 0.999, 1e-08, 0.01, 10

LANE = 128
SUBLANE = 8
VMEM_LIMIT = 48 * 1024 * 1024
NEG = -1e30


def _pcall(body, **kw):
    return pl.pallas_call(body, **kw)


def _params(*sem):
    return pltpu.CompilerParams(dimension_semantics=sem or None, vmem_limit_bytes=VMEM_LIMIT)


def _pick(dim, target):
    best = None
    t = LANE
    while t <= min(dim, target):
        if dim % t == 0:
            best = t
        t += LANE
    return best if best is not None else dim


def _sigmoid(x):
    return 1.0 / (1.0 + jnp.exp(-x))


def _gelu_and_grad(x):
    c = math.sqrt(2.0 / math.pi)
    inner = c * (x + 0.044715 * x * x * x)
    t = jnp.tanh(inner)
    val = 0.5 * x * (1.0 + t)
    grad = 0.5 * (1.0 + t) + 0.5 * x * (1.0 - t * t) * c * (1.0 + 3.0 * 0.044715 * x * x)
    return val, grad


def _neg_expm1(y):
    series = -y * (1.0 + y * (0.5 + y * (1.0 / 6.0 + y * (1.0 / 24.0))))
    return jnp.where(jnp.abs(y) < 0.02, series, 1.0 - jnp.exp(y))


def _dot(a, b, dims):
    return lax.dot_general(a, b, (dims, ((), ())), preferred_element_type=F32)


NN = ((1,), (0,))
NT = ((1,), (1,))
TN = ((0,), (0,))


def matmul(a, b, mode, out_dtype, name, scale=1.0, res=None, tm=1024, tn=1024, tk=1024, b_koff=0):
    if mode == "nn":
        (M, K), N = a.shape, b.shape[1]
    elif mode == "nt":
        (M, K), N = a.shape, b.shape[0]
    else:
        (K, M), N = a.shape, b.shape[1]
    tm, tn, tk = _pick(M, tm), _pick(N, tn), _pick(K, tk)
    nk = K // tk
    dims = {"nn": NN, "nt": NT, "tn": TN}[mode]

    def body(*refs):
        if res is None:
            a_ref, b_ref, o_ref, acc = refs
        else:
            a_ref, b_ref, r_ref, o_ref, acc = refs
        k = pl.program_id(2)

        @pl.when(k == 0)
        def _():
            acc[...] = jnp.zeros_like(acc)

        acc[...] += _dot(a_ref[...].astype(BF16), b_ref[...].astype(BF16), dims)

        @pl.when(k == nk - 1)
        def _():
            r = acc[...] * scale
            if res is not None:
                r = r + r_ref[...]
            o_ref[...] = r.astype(out_dtype)

    a_spec = pl.BlockSpec((tk, tm), lambda i, j, k: (k, i)) if mode == "tn" else pl.BlockSpec((tm, tk), lambda i, j, k: (i, k))
    b_spec = pl.BlockSpec((tn, tk), lambda i, j, k: (j, k + b_koff)) if mode == "nt" else pl.BlockSpec((tk, tn), lambda i, j, k: (k, j))
    o_spec = pl.BlockSpec((tm, tn), lambda i, j, k: (i, j))
    in_specs = [a_spec, b_spec] + ([o_spec] if res is not None else [])
    args = (a, b) + ((res,) if res is not None else ())
    return _pcall(
        body, name=name, grid=(M // tm, N // tn, nk), in_specs=in_specs, out_specs=o_spec,
        out_shape=jax.ShapeDtypeStruct((M, N), out_dtype), scratch_shapes=[pltpu.VMEM((tm, tn), F32)],
        compiler_params=_params("parallel", "parallel", "arbitrary"),
    )(*args)


def rms_fwd(x, g, name, col=0):
    S, D = x.shape[0], g.shape[1]
    tr = _pick(S, 512)

    def body(x_ref, g_ref, o_ref):
        xv = x_ref[...].astype(F32)
        r = lax.rsqrt(jnp.mean(xv * xv, axis=-1, keepdims=True) + NORM_EPS)
        o_ref[...] = (xv * r * g_ref[...]).astype(BF16)

    return _pcall(
        body, name=name, grid=(S // tr,),
        in_specs=[pl.BlockSpec((tr, D), lambda i: (i, col)), pl.BlockSpec((1, D), lambda i: (0, 0))],
        out_specs=pl.BlockSpec((tr, D), lambda i: (i, 0)),
        out_shape=jax.ShapeDtypeStruct((S, D), BF16), compiler_params=_params("parallel"),
    )(x, g)


def _rms_bwd_math(xv, g, dh):
    r = lax.rsqrt(jnp.mean(xv * xv, axis=-1, keepdims=True) + NORM_EPS)
    xhat = xv * r
    dxhat = dh * g
    dx = r * (dxhat - xhat * jnp.mean(dxhat * xhat, axis=-1, keepdims=True))
    dg = jnp.sum(dh * xhat, axis=0, keepdims=True)
    return dx, dg


def rms_bwd(x, g, dh, dres, name):
    S, D = x.shape
    tr = _pick(S, 512)

    def body(x_ref, g_ref, dh_ref, dr_ref, dx_ref, dg_ref):
        dx, dg = _rms_bwd_math(x_ref[...], g_ref[...], dh_ref[...])
        dx_ref[...] = dx + dr_ref[...]

        @pl.when(pl.program_id(0) == 0)
        def _():
            dg_ref[...] = jnp.zeros_like(dg_ref)

        dg_ref[...] += dg

    row = pl.BlockSpec((tr, D), lambda i: (i, 0))
    vec = pl.BlockSpec((1, D), lambda i: (0, 0))
    return _pcall(
        body, name=name, grid=(S // tr,), in_specs=[row, vec, row, row], out_specs=[row, vec],
        out_shape=[jax.ShapeDtypeStruct((S, D), F32), jax.ShapeDtypeStruct((1, D), F32)],
        compiler_params=_params("arbitrary"),
    )(x, g, dh, dres)


def ffn_up(h, w_gu, name):
    S, D = h.shape
    tm, tn = _pick(S, 512), D_FF // 2
    nc = D_FF // tn

    def body(h_ref, wg_ref, wu_ref, g_ref, u_ref, a_ref):
        hv = h_ref[...]
        g = _dot(hv, wg_ref[...], NN)
        u = _dot(hv, wu_ref[...], NN)
        g_ref[...] = g.astype(BF16)
        u_ref[...] = u.astype(BF16)
        a_ref[...] = (g * _sigmoid(g) * u).astype(BF16)

    blk = pl.BlockSpec((tm, tn), lambda j, i: (i, j))
    out = jax.ShapeDtypeStruct((S, D_FF), BF16)
    return _pcall(
        body, name=name, grid=(nc, S // tm),
        in_specs=[pl.BlockSpec((tm, D), lambda j, i: (i, 0)), pl.BlockSpec((D, tn), lambda j, i: (0, j)),
                  pl.BlockSpec((D, tn), lambda j, i: (0, j + nc))],
        out_specs=[blk] * 3, out_shape=[out] * 3, compiler_params=_params("parallel", "parallel"),
    )(h, w_gu, w_gu)


def ffn_dact(dy, w_d, g, u, name):
    S, D = dy.shape
    tm, tn = _pick(S, 512), D_FF // 2
    nc = D_FF // tn

    def body(dy_ref, wd_ref, g_ref, u_ref, dg_ref, du_ref):
        da = _dot(dy_ref[...].astype(BF16), wd_ref[...], NT) * 0.5
        gv = g_ref[...].astype(F32)
        sg = _sigmoid(gv)
        dg_ref[...] = (da * u_ref[...].astype(F32) * sg * (1.0 + gv * (1.0 - sg))).astype(BF16)
        du_ref[...] = (da * gv * sg).astype(BF16)

    blk = pl.BlockSpec((tm, tn), lambda j, i: (i, j))
    out = jax.ShapeDtypeStruct((S, D_FF), BF16)
    return _pcall(
        body, name=name, grid=(nc, S // tm),
        in_specs=[pl.BlockSpec((tm, D), lambda j, i: (i, 0)), pl.BlockSpec((tn, D), lambda j, i: (j, 0)), blk, blk],
        out_specs=[blk] * 2, out_shape=[out] * 2, compiler_params=_params("parallel", "parallel"),
    )(dy, w_d, g, u)


def _conv_taps(xpad, T, cw, cb):
    u = cb + cw[3:4, :] * xpad[pl.ds(8, T), :]
    for tap in range(3):
        u = u + cw[tap:tap + 1, :] * xpad[pl.ds(5 + tap, T), :]
    return u


def _rg_gates(u, wa_ref, wx_ref, ba, bx, lam):
    ub = u.astype(BF16)
    r = _sigmoid(_dot(ub, wa_ref[...], NN) + ba)
    ig = _sigmoid(_dot(ub, wx_ref[...], NN) + bx)
    nlam = -lam
    clam = -RG_C * (jnp.maximum(nlam, 0.0) + jnp.log(1.0 + jnp.exp(-jnp.abs(nlam))))
    la = clam * r
    return r, ig, clam, la


def rglru_fwd(z, cw, cb, wa, wx, ba, bx, lam, name):
    S, D = z.shape[0], D_MODEL
    T = _pick(S, 256)

    def body(x_ref, g_ref, cw_ref, cb_ref, wa_ref, wx_ref, ba_ref, bx_ref, lam_ref, y_ref, h_ref, xpad, a_s, b_s, hst):
        @pl.when(pl.program_id(0) == 0)
        def _():
            xpad[pl.ds(0, 8), :] = jnp.zeros((8, D), F32)
            hst[...] = jnp.zeros_like(hst)

        xpad[pl.ds(8, T), :] = x_ref[...].astype(F32)
        u = _conv_taps(xpad, T, cw_ref[...], cb_ref[...])
        xpad[pl.ds(0, 8), :] = xpad[pl.ds(T, 8), :]
        r, ig, clam, la = _rg_gates(u, wa_ref, wx_ref, ba_ref[...], bx_ref[...], lam_ref[...])
        a_s[...] = jnp.exp(la)
        b_s[...] = jnp.sqrt(_neg_expm1(2.0 * la)) * (ig * u)

        def tile(j, h):
            r0 = pl.multiple_of(j * 8, 8)
            av = a_s[pl.ds(r0, 8), :]
            bv = b_s[pl.ds(r0, 8), :]
            rows = []
            for k in range(8):
                h = av[k:k + 1, :] * h + bv[k:k + 1, :]
                rows.append(h)
            h_ref[pl.ds(r0, 8), :] = jnp.concatenate(rows, axis=0)
            return h

        hst[...] = lax.fori_loop(0, T // 8, tile, hst[...])
        gel, _ = _gelu_and_grad(g_ref[...].astype(F32))
        y_ref[...] = (h_ref[...] * gel).astype(BF16)

    blk = lambda c: pl.BlockSpec((T, D), lambda i: (i, c))
    vec = pl.BlockSpec((1, D), lambda i: (0, 0))
    full = lambda r: pl.BlockSpec((r, D), lambda i: (0, 0))
    return _pcall(
        body, name=name, grid=(S // T,),
        in_specs=[blk(0), blk(1), full(4), vec, full(D), full(D), vec, vec, vec],
        out_specs=[blk(0), blk(0)],
        out_shape=[jax.ShapeDtypeStruct((S, D), BF16), jax.ShapeDtypeStruct((S, D), F32)],
        scratch_shapes=[pltpu.VMEM((T + 8, D), F32), pltpu.VMEM((T, D), F32), pltpu.VMEM((T, D), F32), pltpu.VMEM((1, D), F32)],
        compiler_params=_params("arbitrary"),
    )(z, z, cw, cb, wa, wx, ba, bx, lam)


def rglru_bwd(z, hs, dy, cw, cb, wa, wx, ba, bx, lam, name):
    S, D = z.shape[0], D_MODEL
    T = _pick(S, 256)
    nb = S // T
    t8 = T // 8

    def body(x_ref, xp_ref, g_ref, h_ref, hp_ref, dy_ref, cw_ref, cb_ref, wa_ref, wx_ref, ba_ref, bx_ref, lam_ref,
             dx_ref, dg_ref, dwa_ref, dwx_ref, dvec_ref, xpad, hpad, dupad, a_s, d_s, carry):
        i = pl.program_id(0)
        first_block = i == nb - 1

        @pl.when(i == 0)
        def _():
            dwa_ref[...] = jnp.zeros_like(dwa_ref)
            dwx_ref[...] = jnp.zeros_like(dwx_ref)
            dvec_ref[...] = jnp.zeros_like(dvec_ref)
            carry[...] = jnp.zeros_like(carry)
            dupad[pl.ds(T, 8), :] = jnp.zeros((8, D), F32)

        keep = jnp.where(first_block, 0.0, 1.0)
        xpad[pl.ds(0, 8), :] = xp_ref[...].astype(F32) * keep
        xpad[pl.ds(8, T), :] = x_ref[...].astype(F32)
        hpad[pl.ds(0, 8), :] = hp_ref[...] * keep
        hpad[pl.ds(8, T), :] = h_ref[...]
        cwv = cw_ref[...]
        u = _conv_taps(xpad, T, cwv, cb_ref[...])
        r, ig, clam, la = _rg_gates(u, wa_ref, wx_ref, ba_ref[...], bx_ref[...], lam_ref[...])
        a = jnp.exp(la)
        a_s[...] = a
        gv = g_ref[...].astype(F32)
        gel, dgel = _gelu_and_grad(gv)
        dyv = dy_ref[...].astype(F32)
        d_s[...] = dyv * gel
        dg_ref[...] = (dyv * h_ref[...] * dgel).astype(BF16)

        def tile(j, c):
            r0 = pl.multiple_of((t8 - 1 - j) * 8, 8)
            av = a_s[pl.ds(r0, 8), :]
            dv = d_s[pl.ds(r0, 8), :]
            rows = [None] * 8
            for k in range(7, -1, -1):
                d = dv[k:k + 1, :] + c
                rows[k] = d
                c = av[k:k + 1, :] * d
            d_s[pl.ds(r0, 8), :] = jnp.concatenate(rows, axis=0)
            return c

        carry[...] = lax.fori_loop(0, t8, tile, carry[...])
        dht = d_s[...]
        hprev = hpad[pl.ds(7, T), :]
        w = _neg_expm1(2.0 * la)
        s = jnp.sqrt(w)
        e2 = 1.0 - w
        d_iu = dht * s
        dla = dht * hprev * a - dht * (ig * u) * e2 / s
        dpr = (dla * clam * r * (1.0 - r))
        dpi = (d_iu * u * ig * (1.0 - ig))
        dprb, dpib, ub = dpr.astype(BF16), dpi.astype(BF16), u.astype(BF16)
        du = d_iu * ig + _dot(dprb, wa_ref[...], NT) + _dot(dpib, wx_ref[...], NT)
        dwa_ref[...] += _dot(ub, dprb, TN)
        dwx_ref[...] += _dot(ub, dpib, TN)
        dvec_ref[0:1, :] += jnp.sum(dpr, axis=0, keepdims=True)
        dvec_ref[1:2, :] += jnp.sum(dpi, axis=0, keepdims=True)
        dvec_ref[2:3, :] += jnp.sum(dla * r, axis=0, keepdims=True)
        dvec_ref[3:4, :] += jnp.sum(du, axis=0, keepdims=True)
        for tap in range(4):
            dvec_ref[4 + tap:5 + tap, :] += jnp.sum(du * xpad[pl.ds(5 + tap, T), :], axis=0, keepdims=True)
        dupad[pl.ds(0, T), :] = du
        dx = cwv[3:4, :] * du
        for tap in range(3):
            dx = dx + cwv[tap:tap + 1, :] * dupad[pl.ds(3 - tap, T), :]
        dx_ref[...] = dx.astype(BF16)
        dupad[pl.ds(T, 8), :] = dupad[pl.ds(0, 8), :]

        @pl.when(first_block)
        def _():
            dvec_ref[2:3, :] = dvec_ref[2:3, :] * (RG_C * _sigmoid(-lam_ref[...]))

    rev = lambda c: pl.BlockSpec((T, D), lambda i: (nb - 1 - i, c))
    prev = lambda c: pl.BlockSpec((8, D), lambda i: (jnp.maximum((nb - 1 - i) * t8 - 1, 0), c))
    vec = pl.BlockSpec((1, D), lambda i: (0, 0))
    full = lambda r: pl.BlockSpec((r, D), lambda i: (0, 0))
    return _pcall(
        body, name=name, grid=(nb,),
        in_specs=[rev(0), prev(0), rev(1), rev(0), prev(0), rev(0), full(4), vec, full(D), full(D), vec, vec, vec],
        out_specs=[rev(0), rev(0), full(D), full(D), full(8)],
        out_shape=[jax.ShapeDtypeStruct((S, D), BF16), jax.ShapeDtypeStruct((S, D), BF16),
                   jax.ShapeDtypeStruct((D, D), F32), jax.ShapeDtypeStruct((D, D), F32), jax.ShapeDtypeStruct((8, D), F32)],
        scratch_shapes=[pltpu.VMEM((T + 8, D), F32), pltpu.VMEM((T + 8, D), F32), pltpu.VMEM((T + 8, D), F32),
                        pltpu.VMEM((T, D), F32), pltpu.VMEM((T, D), F32), pltpu.VMEM((1, D), F32)],
        compiler_params=_params("arbitrary"),
    )(z, z, z, hs, hs, dy, cw, cb, wa, wx, ba, bx, lam)


def _tri(n, kind):
    j = lax.broadcasted_iota(jnp.int32, (n, n), 0)
    s = lax.broadcasted_iota(jnp.int32, (n, n), 1)
    m = {"gt": j > s, "le": j <= s, "lt": j < s}[kind]
    return jnp.where(m, 1.0, 0.0).astype(BF16)


def _dot2(x, tri):
    hi = x.astype(BF16)
    lo = (x - hi.astype(F32)).astype(BF16)
    return _dot(jnp.concatenate([hi, lo], axis=1), jnp.concatenate([tri, tri], axis=0), NN)


SB_TK = 128

def _sb_logits(q, kblk, q0, k0, tq, masked):
    z = _dot(q, kblk, NT) * SB_SCALE
    sp = jnp.maximum(z, 0.0) + jnp.log(1.0 + jnp.exp(-jnp.abs(z)))
    lkeep = -sp
    mask = None
    if masked:
        tpos = q0 + lax.broadcasted_iota(jnp.int32, (tq, SB_TK), 0)
        spos = k0 + lax.broadcasted_iota(jnp.int32, (tq, SB_TK), 1)
        mask = spos < tpos
        lkeep = jnp.where(mask, lkeep, 0.0)
    return mask, lkeep, z - sp


def sb_fwd(z, name):
    S = z.shape[0]
    tq, tk = _pick(S, 512), SB_TK
    nd = tq // tk
    U = min(4, nd)
    qc, kc, vc = Z_Q // HEAD, Z_K // HEAD, Z_V // HEAD

    def body(q_ref, k_ref, v_ref, o_ref, lt_ref, acc, run):
        qi = pl.program_id(1)
        q0 = qi * tq
        q = q_ref[...]
        tri = _tri(tk, "gt")
        acc[...] = jnp.zeros_like(acc)
        run[...] = jnp.zeros_like(run)

        def group(k0s, masked):
            parts = [(k0,) + _sb_logits(q, k_ref[pl.ds(k0, tk), :], q0, k0, tq, masked) for k0 in k0s]
            cums = [_dot2(p[2], tri) for p in parts]
            r, a = run[...], acc[...]
            for (k0, mask, lkeep, lbeta), cum in zip(parts, cums):
                w = jnp.exp(lbeta + cum + r)
                if masked:
                    w = jnp.where(mask, w, 0.0)
                a = a + _dot(w.astype(BF16), v_ref[pl.ds(k0, tk), :], NN)
                r = r + jnp.sum(lkeep, axis=1, keepdims=True)
            acc[...] = a
            run[...] = r

        for g in range(nd // U):
            group([pl.multiple_of(q0 + (nd - 1 - g * U - u) * tk, tk) for u in range(U)], True)

        def step(i, c):
            base = qi * nd - 1 - i * U
            group([pl.multiple_of((base - u) * tk, tk) for u in range(U)], False)
            return c

        lax.fori_loop(0, qi * nd // U, step, 0)
        o_ref[...] = acc[...].astype(BF16)
        lt_ref[0] = run[...]

    return _pcall(
        body, name=name, grid=(SB_HEADS, S // tq),
        in_specs=[pl.BlockSpec((tq, HEAD), lambda h, i: (i, qc + h)), pl.BlockSpec((S, HEAD), lambda h, i: (0, kc + h)),
                  pl.BlockSpec((S, HEAD), lambda h, i: (0, vc + h))],
        out_specs=[pl.BlockSpec((tq, HEAD), lambda h, i: (i, h)), pl.BlockSpec((1, tq, 1), lambda h, i: (h, i, 0))],
        out_shape=[jax.ShapeDtypeStruct((S, SB_HEADS * HEAD), BF16), jax.ShapeDtypeStruct((SB_HEADS, S, 1), F32)],
        scratch_shapes=[pltpu.VMEM((tq, HEAD), F32), pltpu.VMEM((tq, 1), F32)],
        compiler_params=_params("parallel", "parallel"),
    )(z, z, z)


def sb_bwd(z, ltot, dy, name):
    S = z.shape[0]
    tq, tk = _pick(S, 512), SB_TK
    nd = tq // tk
    U = min(4, nd)
    nkb = S // tk
    qc, kc, vc = Z_Q // HEAD, Z_K // HEAD, Z_V // HEAD

    def body(q_ref, k_ref, v_ref, lt_ref, do_ref, dq_ref, dk_ref, dv_ref, dq_s, run_l, run_g, dkT, dvT):
        qi = pl.program_id(1)
        q0 = qi * tq

        @pl.when(qi == 0)
        def _():
            dkT[...] = jnp.zeros_like(dkT)
            dvT[...] = jnp.zeros_like(dvT)

        q = q_ref[...]
        do = do_ref[...].astype(BF16)
        qT, doT = q.T, do.T
        ltv = lt_ref[0]
        tri_le, tri_lt = _tri(tk, "le"), _tri(tk, "lt")
        dq_s[...] = jnp.zeros_like(dq_s)
        run_l[...] = jnp.zeros_like(run_l)
        run_g[...] = jnp.zeros_like(run_g)

        def group(k0s, masked):
            parts = []
            for k0 in k0s:
                kblk = k_ref[pl.ds(k0, tk), :]
                mask, lkeep, lbeta = _sb_logits(q, kblk, q0, k0, tq, masked)
                parts.append((k0, kblk, mask, lkeep, lbeta, _dot(do, v_ref[pl.ds(k0, tk), :], NT)))
            pres = [_dot2(p[3], tri_le) for p in parts]
            rl = run_l[...]
            ws = []
            for (k0, kblk, mask, lkeep, lbeta, dw), pre in zip(parts, pres):
                w = jnp.exp(lbeta + (ltv - (pre + rl)))
                if masked:
                    w = jnp.where(mask, w, 0.0)
                ws.append((w, w * dw))
                rl = rl + jnp.sum(lkeep, axis=1, keepdims=True)
            run_l[...] = rl
            gpres = [_dot2(g, tri_lt) for _, g in ws]
            rg, dq = run_g[...], dq_s[...]
            for (k0, kblk, mask, lkeep, lbeta, dw), (w, g), gpre in zip(parts, ws, gpres):
                dz = (g * jnp.exp(lkeep) - jnp.exp(lbeta) * (gpre + rg)) * SB_SCALE
                if masked:
                    dz = jnp.where(mask, dz, 0.0)
                dz = dz.astype(BF16)
                dq = dq + _dot(dz, kblk, NN)
                kb = k0 // tk
                dkT[kb] += _dot(qT, dz, NN)
                dvT[kb] += _dot(doT, w.astype(BF16), NN)
                rg = rg + jnp.sum(g, axis=1, keepdims=True)
            run_g[...] = rg
            dq_s[...] = dq

        def step(i, c):
            group([pl.multiple_of((i * U + u) * tk, tk) for u in range(U)], False)
            return c

        lax.fori_loop(0, qi * nd // U, step, 0)
        for g in range(nd // U):
            group([pl.multiple_of(q0 + (g * U + u) * tk, tk) for u in range(U)], True)
        dq_ref[...] = dq_s[...].astype(BF16)

        @pl.when(qi == pl.num_programs(1) - 1)
        def _():
            def flush(kb, c):
                r0 = pl.multiple_of(kb * tk, tk)
                dk_ref[pl.ds(r0, tk), :] = dkT[kb].T.astype(BF16)
                dv_ref[pl.ds(r0, tk), :] = dvT[kb].T.astype(BF16)
                return c

            lax.fori_loop(0, nkb, flush, 0)

    qblk = lambda c: pl.BlockSpec((tq, HEAD), lambda h, i: (i, c + h))
    kfull = lambda c: pl.BlockSpec((S, HEAD), lambda h, i: (0, c + h))
    out = jax.ShapeDtypeStruct((S, SB_HEADS * HEAD), BF16)
    return _pcall(
        body, name=name, grid=(SB_HEADS, S // tq),
        in_specs=[qblk(qc), kfull(kc), kfull(vc), pl.BlockSpec((1, tq, 1), lambda h, i: (h, i, 0)), qblk(0)],
        out_specs=[qblk(0), kfull(0), kfull(0)], out_shape=[out, out, out],
        scratch_shapes=[pltpu.VMEM((tq, HEAD), F32), pltpu.VMEM((tq, 1), F32), pltpu.VMEM((tq, 1), F32),
                        pltpu.VMEM((nkb, HEAD, tk), F32), pltpu.VMEM((nkb, HEAD, tk), F32)],
        compiler_params=_params("arbitrary", "arbitrary"),
    )(z, z, z, ltot, dy)


def _rope(x, cs, sn, sign):
    lane = lax.broadcasted_iota(jnp.int32, x.shape, 1)
    swapped = jnp.where(lane < MLA_ROPE // 2, -pltpu.roll(x, LANE - MLA_ROPE // 2, 1), pltpu.roll(x, MLA_ROPE // 2, 1))
    return x * cs + sign * swapped * sn


def mla_prep_fwd(z, qn, kvn, cs, sn, name):
    S = z.shape[0]
    tr = _pick(S, 512)

    def body(cq_ref, ckv_ref, kr_ref, qn_ref, kvn_ref, cs_ref, sn_ref, oq_ref, okv_ref, okr_ref):
        for src, g, dst in ((cq_ref, qn_ref, oq_ref), (ckv_ref, kvn_ref, okv_ref)):
            xv = src[...].astype(F32)
            r = lax.rsqrt(jnp.mean(xv * xv, axis=-1, keepdims=True) + NORM_EPS)
            dst[...] = (xv * r * g[...]).astype(BF16)
        okr_ref[...] = _rope(kr_ref[...].astype(F32), cs_ref[...], sn_ref[...], 1.0).astype(BF16)

    lora = lambda c: pl.BlockSpec((tr, MLA_LORA), lambda i: (i, c))
    tile = lambda c: pl.BlockSpec((tr, LANE), lambda i: (i, c))
    vec = pl.BlockSpec((1, MLA_LORA), lambda i: (0, 0))
    return _pcall(
        body, name=name, grid=(S // tr,),
        in_specs=[lora(Z_CQ // MLA_LORA), lora(Z_CKV // MLA_LORA), tile(Z_KR // LANE), vec, vec, tile(0), tile(0)],
        out_specs=[lora(0), lora(0), tile(0)],
        out_shape=[jax.ShapeDtypeStruct((S, MLA_LORA), BF16), jax.ShapeDtypeStruct((S, MLA_LORA), BF16),
                   jax.ShapeDtypeStruct((S, LANE), BF16)],
        compiler_params=_params("parallel"),
    )(z, z, z, qn, kvn, cs, sn)


def mla_prep_bwd(z, qn, kvn, cs, sn, dcqn, dckvn, dkrope, name):
    S = z.shape[0]
    tr = _pick(S, 512)

    def body(cq_ref, ckv_ref, qn_ref, kvn_ref, cs_ref, sn_ref, dq_ref, dkv_ref, dkr_ref, oq_ref, okv_ref, okr_ref, gq_ref, gkv_ref):
        @pl.when(pl.program_id(0) == 0)
        def _():
            gq_ref[...] = jnp.zeros_like(gq_ref)
            gkv_ref[...] = jnp.zeros_like(gkv_ref)

        for src, g, dh, dst, gacc in ((cq_ref, qn_ref, dq_ref, oq_ref, gq_ref), (ckv_ref, kvn_ref, dkv_ref, okv_ref, gkv_ref)):
            dx, dg = _rms_bwd_math(src[...].astype(F32), g[...], dh[...])
            dst[...] = dx.astype(BF16)
            gacc[...] += dg
        okr_ref[...] = _rope(dkr_ref[...], cs_ref[...], sn_ref[...], -1.0).astype(BF16)

    lora = lambda c: pl.BlockSpec((tr, MLA_LORA), lambda i: (i, c))
    tile = lambda c: pl.BlockSpec((tr, LANE), lambda i: (i, c))
    vec = pl.BlockSpec((1, MLA_LORA), lambda i: (0, 0))
    return _pcall(
        body, name=name, grid=(S // tr,),
        in_specs=[lora(Z_CQ // MLA_LORA), lora(Z_CKV // MLA_LORA), vec, vec, tile(0), tile(0), lora(0), lora(0), tile(0)],
        out_specs=[lora(0), lora(0), tile(0), vec, vec],
        out_shape=[jax.ShapeDtypeStruct((S, MLA_LORA), BF16), jax.ShapeDtypeStruct((S, MLA_LORA), BF16),
                   jax.ShapeDtypeStruct((S, LANE), BF16), jax.ShapeDtypeStruct((1, MLA_LORA), F32), jax.ShapeDtypeStruct((1, MLA_LORA), F32)],
        compiler_params=_params("arbitrary"),
    )(z, z, qn, kvn, cs, sn, dcqn, dckvn, dkrope)


def q_rope(q, cs, sn, sign, name):
    S = q.shape[0]
    tr = _pick(S, 512)

    def body(q_ref, cs_ref, sn_ref, o_ref):
        o_ref[:, 0:LANE] = q_ref[:, 0:LANE].astype(BF16)
        o_ref[:, LANE:2 * LANE] = _rope(q_ref[:, LANE:2 * LANE], cs_ref[...], sn_ref[...], sign).astype(BF16)

    blk = pl.BlockSpec((tr, 2 * LANE), lambda i, h: (i, h))
    tile = pl.BlockSpec((tr, LANE), lambda i, h: (i, 0))
    return _pcall(
        body, name=name, grid=(S // tr, MLA_HEADS), in_specs=[blk, tile, tile], out_specs=blk,
        out_shape=jax.ShapeDtypeStruct(q.shape, BF16), compiler_params=_params("parallel", "parallel"),
    )(q, cs, sn)


def _chunk_mask(rows, cols):
    tch = lax.broadcasted_iota(jnp.int32, (rows, cols), 0) // CHUNK
    sch = lax.broadcasted_iota(jnp.int32, (rows, cols), 1) // CHUNK
    return sch <= tch


def _fill_kcat(kcat, kv_ref, kr_ref):
    kcat[:, 0:HEAD] = kv_ref[:, 0:HEAD]
    kcat[:, HEAD:2 * HEAD] = kr_ref[...]


def mla_fwd(q, kv, kr, name):
    S = q.shape[0]
    t = _pick(S, 512)

    def body(q_ref, kv_ref, kr_ref, o_ref, lse_ref, kcat, m_s, l_s, acc):
        qi = pl.program_id(1)

        @pl.when(qi == 0)
        def _():
            _fill_kcat(kcat, kv_ref, kr_ref)

        q = q_ref[...]
        m_s[...] = jnp.full_like(m_s, NEG)
        l_s[...] = jnp.zeros_like(l_s)
        acc[...] = jnp.zeros_like(acc)

        def block(k0, width, masked):
            s = _dot(q, kcat[pl.ds(k0, width), :], NT) * MLA_SCALE
            if masked:
                s = jnp.where(_chunk_mask(t, width), s, NEG)
            m = m_s[...]
            m2 = jnp.maximum(m, jnp.max(s, axis=1, keepdims=True))
            p = jnp.exp(s - m2)
            alpha = jnp.exp(m - m2)
            l_s[...] = alpha * l_s[...] + jnp.sum(p, axis=1, keepdims=True)
            acc[...] = alpha * acc[...] + _dot(p.astype(BF16), kv_ref[pl.ds(k0, width), HEAD:2 * HEAD], NN)
            m_s[...] = m2

        if S >= 2 * t:
            def step(i, c):
                block(pl.multiple_of(i * 2 * t, 2 * t), 2 * t, False)
                return c

            lax.fori_loop(0, qi // 2, step, 0)

            @pl.when(qi % 2 == 1)
            def _():
                block(pl.multiple_of((qi - 1) * t, t), t, False)

        block(pl.multiple_of(qi * t, t), t, True)
        o_ref[...] = (acc[...] / l_s[...]).astype(BF16)
        lse_ref[0] = m_s[...] + jnp.log(l_s[...])

    return _pcall(
        body, name=name, grid=(MLA_HEADS, S // t),
        in_specs=[pl.BlockSpec((t, 2 * HEAD), lambda h, i: (i, h)), pl.BlockSpec((S, 2 * HEAD), lambda h, i: (0, h)),
                  pl.BlockSpec((S, LANE), lambda h, i: (0, 0))],
        out_specs=[pl.BlockSpec((t, HEAD), lambda h, i: (i, h)), pl.BlockSpec((1, t, 1), lambda h, i: (h, i, 0))],
        out_shape=[jax.ShapeDtypeStruct((S, MLA_HEADS * HEAD), BF16), jax.ShapeDtypeStruct((MLA_HEADS, S, 1), F32)],
        scratch_shapes=[pltpu.VMEM((S, 2 * HEAD), BF16), pltpu.VMEM((t, 1), F32), pltpu.VMEM((t, 1), F32), pltpu.VMEM((t, HEAD), F32)],
        compiler_params=_params("arbitrary", "arbitrary"),
    )(q, kv, kr)


def mla_bwd(q, kv, kr, o, lse, do, cs, sn, name):
    S = q.shape[0]
    t = _pick(S, 512)
    nkb = S // t

    def body(q_ref, kv_ref, kr_ref, o_ref, lse_ref, do_ref, cs_ref, sn_ref, dq_ref, dkv_ref, dkr_ref, kcat, dq_s, dkT, dvT, dkrT):
        h, qi = pl.program_id(0), pl.program_id(1)

        @pl.when(qi == 0)
        def _():
            _fill_kcat(kcat, kv_ref, kr_ref)
            dkT[...] = jnp.zeros_like(dkT)
            dvT[...] = jnp.zeros_like(dvT)

        @pl.when((qi == 0) & (h == 0))
        def _():
            dkrT[...] = jnp.zeros_like(dkrT)

        q = q_ref[...]
        dov = do_ref[...].astype(F32)
        dob = dov.astype(BF16)
        qT, doT = q.T, dob.T
        delta = jnp.sum(dov * o_ref[...].astype(F32), axis=1, keepdims=True)
        lsev = lse_ref[0]
        dq_s[...] = jnp.zeros_like(dq_s)

        def block(kb, masked):
            k0 = pl.multiple_of(kb * t, t)
            kc = kcat[pl.ds(k0, t), :]
            p = jnp.exp(_dot(q, kc, NT) * MLA_SCALE - lsev)
            if masked:
                p = jnp.where(_chunk_mask(t, t), p, 0.0)
            ds = (p * (_dot(dob, kv_ref[pl.ds(k0, t), HEAD:2 * HEAD], NT) - delta) * MLA_SCALE).astype(BF16)
            dkT[kb] += _dot(qT, ds, NN)
            dvT[kb] += _dot(doT, p.astype(BF16), NN)
            dq_s[...] += _dot(ds, kc, NN)

        def step(kb, c):
            block(kb, False)
            return c

        lax.fori_loop(0, qi, step, 0)
        block(qi, True)
        dq_ref[:, 0:HEAD] = dq_s[:, 0:HEAD].astype(BF16)
        dq_ref[:, HEAD:2 * HEAD] = _rope(dq_s[:, HEAD:2 * HEAD], cs_ref[...], sn_ref[...], -1.0).astype(BF16)
        last_q = qi == pl.num_programs(1) - 1

        @pl.when(last_q)
        def _():
            def flush(kb, c):
                r0 = pl.multiple_of(kb * t, t)
                dkv_ref[pl.ds(r0, t), 0:HEAD] = dkT[kb, 0:HEAD, :].T
                dkv_ref[pl.ds(r0, t), HEAD:2 * HEAD] = dvT[kb].T
                dkrT[kb] += dkT[kb, HEAD:2 * HEAD, :]
                return c

            lax.fori_loop(0, nkb, flush, 0)

        @pl.when(last_q & (h == pl.num_programs(0) - 1))
        def _():
            def flush(kb, c):
                r0 = pl.multiple_of(kb * t, t)
                dkr_ref[pl.ds(r0, t), :] = dkrT[kb].T
                return c

            lax.fori_loop(0, nkb, flush, 0)

    qblk = pl.BlockSpec((t, 2 * HEAD), lambda h, i: (i, h))
    kvfull = pl.BlockSpec((S, 2 * HEAD), lambda h, i: (0, h))
    krfull = pl.BlockSpec((S, LANE), lambda h, i: (0, 0))
    oblk = pl.BlockSpec((t, HEAD), lambda h, i: (i, h))
    return _pcall(
        body, name=name, grid=(MLA_HEADS, S // t),
        in_specs=[qblk, kvfull, krfull, oblk, pl.BlockSpec((1, t, 1), lambda h, i: (h, i, 0)), oblk,
                  pl.BlockSpec((t, LANE), lambda h, i: (i, 0)), pl.BlockSpec((t, LANE), lambda h, i: (i, 0))],
        out_specs=[qblk, kvfull, krfull],
        out_shape=[jax.ShapeDtypeStruct(q.shape, BF16), jax.ShapeDtypeStruct(kv.shape, F32), jax.ShapeDtypeStruct((S, LANE), F32)],
        scratch_shapes=[pltpu.VMEM((S, 2 * HEAD), BF16), pltpu.VMEM((t, 2 * HEAD), F32), pltpu.VMEM((nkb, 2 * HEAD, t), F32),
                        pltpu.VMEM((nkb, HEAD, t), F32), pltpu.VMEM((nkb, LANE, t), F32)],
        compiler_params=_params("arbitrary", "arbitrary"),
    )(q, kv, kr, o, lse, do, cs, sn)


GATE_TC = 512


def merge_fwd(z, ya, yb, yc, name):
    S = z.shape[0]
    tr, tc = _pick(S, 512), GATE_TC
    g0 = Z_GATE // tc
    nc = D_MODEL // tc

    def body(ga_ref, gb_ref, gc_ref, ya_ref, yb_ref, yc_ref, o_ref):
        acc = None
        for g, y in ((ga_ref, ya_ref), (gb_ref, yb_ref), (gc_ref, yc_ref)):
            term = _sigmoid(g[...].astype(F32)) * y[...].astype(F32)
            acc = term if acc is None else acc + term
        o_ref[...] = acc.astype(BF16)

    gate = lambda b: pl.BlockSpec((tr, tc), lambda i, j: (i, g0 + b * nc + j))
    blk = pl.BlockSpec((tr, tc), lambda i, j: (i, j))
    return _pcall(
        body, name=name, grid=(S // tr, nc), in_specs=[gate(0), gate(1), gate(2), blk, blk, blk], out_specs=blk,
        out_shape=jax.ShapeDtypeStruct((S, D_MODEL), BF16), compiler_params=_params("parallel", "parallel"),
    )(z, z, z, ya, yb, yc)


def merge_bwd(z, ya, yb, yc, dm, name):
    S = z.shape[0]
    tr, tc = _pick(S, 512), GATE_TC
    g0 = Z_GATE // tc
    nc = D_MODEL // tc

    def body(ga_ref, gb_ref, gc_ref, ya_ref, yb_ref, yc_ref, dm_ref, da_ref, db_ref, dc_ref, dga_ref, dgb_ref, dgc_ref):
        dmv = dm_ref[...].astype(F32)
        for g, y, dy, dg in ((ga_ref, ya_ref, da_ref, dga_ref), (gb_ref, yb_ref, db_ref, dgb_ref), (gc_ref, yc_ref, dc_ref, dgc_ref)):
            sg = _sigmoid(g[...].astype(F32))
            dy[...] = (dmv * sg).astype(BF16)
            dg[...] = (dmv * y[...].astype(F32) * sg * (1.0 - sg)).astype(BF16)

    gate = lambda b: pl.BlockSpec((tr, tc), lambda i, j: (i, g0 + b * nc + j))
    blk = pl.BlockSpec((tr, tc), lambda i, j: (i, j))
    out = jax.ShapeDtypeStruct((S, D_MODEL), BF16)
    return _pcall(
        body, name=name, grid=(S // tr, nc), in_specs=[gate(0), gate(1), gate(2), blk, blk, blk, blk],
        out_specs=[blk] * 6, out_shape=[out] * 6, compiler_params=_params("parallel", "parallel"),
    )(z, z, z, ya, yb, yc, dm)


def loss_head(x, g, target, name):
    S, D = x.shape
    tr = _pick(S, 512)

    def body(x_ref, g_ref, t_ref, l_ref, dx_ref, dg_ref):
        @pl.when(pl.program_id(0) == 0)
        def _():
            l_ref[...] = jnp.zeros_like(l_ref)
            dg_ref[...] = jnp.zeros_like(dg_ref)

        xv, gv = x_ref[...], g_ref[...]
        r = lax.rsqrt(jnp.mean(xv * xv, axis=-1, keepdims=True) + NORM_EPS)
        diff = xv * r * gv - t_ref[...]
        l_ref[...] += 0.5 * jnp.sum(jnp.mean(diff * diff, axis=-1, keepdims=True), axis=0, keepdims=True)
        dx, dg = _rms_bwd_math(xv, gv, diff * (1.0 / D))
        dx_ref[...] = dx
        dg_ref[...] += dg

    row = pl.BlockSpec((tr, D), lambda i: (i, 0))
    vec = pl.BlockSpec((1, D), lambda i: (0, 0))
    return _pcall(
        body, name=name, grid=(S // tr,), in_specs=[row, vec, row],
        out_specs=[pl.BlockSpec((1, LANE), lambda i: (0, 0)), row, vec],
        out_shape=[jax.ShapeDtypeStruct((1, LANE), F32), jax.ShapeDtypeStruct((S, D), F32), jax.ShapeDtypeStruct((1, D), F32)],
        compiler_params=_params("arbitrary"),
    )(x, g, target)


def _peer(k, x, y, c):
    px = 1 - x if k & 4 else x
    py = 1 - y if k & 2 else y
    pc = 1 - c if k & 1 else c
    return (px, py, pc), 4 * px + 2 * py + pc


def exchange(arrs, gather, name):
    n = len(arrs)
    shapes = [((N_DEV,) + a.shape) if gather else a.shape for a in arrs]

    def body(*refs):
        ins, outs = refs[:n], refs[n:2 * n]
        send_sems, recv_sems, loc_sems = refs[2 * n:]
        x, y, c = lax.axis_index("x"), lax.axis_index("y"), lax.axis_index("c")
        me = 4 * x + 2 * y + c
        sends, recvs, locs = [], [], []
        for a in range(n):
            loc = pltpu.make_async_copy(ins[a] if gather else ins[a].at[me], outs[a].at[me], loc_sems.at[a])
            loc.start()
            locs.append(loc)
            for k in range(1, N_DEV):
                peer, pid = _peer(k, x, y, c)
                s = a * (N_DEV - 1) + k - 1
                src = ins[a] if gather else ins[a].at[pid]
                snd = pltpu.make_async_remote_copy(src_ref=src, dst_ref=outs[a].at[me], send_sem=send_sems.at[s],
                                                   recv_sem=recv_sems.at[s], device_id=peer, device_id_type=pl.DeviceIdType.MESH)
                snd.start()
                sends.append(snd)
                recvs.append(pltpu.make_async_remote_copy(src_ref=src, dst_ref=outs[a].at[pid], send_sem=send_sems.at[s],
                                                          recv_sem=recv_sems.at[s], device_id=peer, device_id_type=pl.DeviceIdType.MESH))
        for snd, rcv in zip(sends, recvs):
            snd.wait_send()
            rcv.wait_recv()
        for loc in locs:
            loc.wait()

    any_spec = pl.BlockSpec(memory_space=pl.ANY)
    outs = _pcall(
        body, name=name, in_specs=[any_spec] * n, out_specs=[any_spec] * n,
        out_shape=[jax.ShapeDtypeStruct(s, a.dtype) for s, a in zip(shapes, arrs)],
        scratch_shapes=[pltpu.SemaphoreType.DMA((n * (N_DEV - 1),)), pltpu.SemaphoreType.DMA((n * (N_DEV - 1),)),
                        pltpu.SemaphoreType.DMA((n,))],
        compiler_params=pltpu.CompilerParams(has_side_effects=True),
    )(*arrs)
    return list(outs)


_HBM = pl.BlockSpec(memory_space=pltpu.HBM)
_SEM = pl.BlockSpec(memory_space=pltpu.SEMAPHORE)
_EFFECT = pltpu.SideEffectType.DATAFLOW_SIDE_EFFECTING


def _peer_copies(srcs, lands, send_sems, recv_sems, gather):
    x, y, c = lax.axis_index("x"), lax.axis_index("y"), lax.axis_index("c")
    me = 4 * x + 2 * y + c
    out = []
    for a, (src, land) in enumerate(zip(srcs, lands)):
        for k in range(1, N_DEV):
            peer, pid = _peer(k, x, y, c)
            s = a * (N_DEV - 1) + k - 1
            mk = lambda dst: pltpu.make_async_remote_copy(
                src_ref=src if gather else src.at[pid], dst_ref=dst, send_sem=send_sems.at[s], recv_sem=recv_sems.at[s],
                device_id=peer, device_id_type=pl.DeviceIdType.MESH)
            out.append((mk(land.at[me]), mk(land.at[pid])))
    return out


def exchange_start(arrs, gather, after, name):
    n = len(arrs)
    nsem = n * (N_DEV - 1)
    lands = [lax.empty(((N_DEV,) + a.shape) if gather else a.shape, a.dtype) for a in arrs]

    def body(*refs):
        srcs, land_refs = refs[:n], refs[n:2 * n]
        send_sems, recv_sems = refs[2 * n + 1], refs[2 * n + 2]
        token = refs[-1]
        for snd, _ in _peer_copies(srcs, land_refs, send_sems, recv_sems, gather):
            snd.start()
        token[...] = jnp.zeros_like(token)

    hbm = lambda a: pltpu.HBM(a.shape, a.dtype)
    outs = _pcall(
        body, name=name, in_specs=[_HBM] * (2 * n) + [pl.BlockSpec(memory_space=pl.ANY)],
        out_specs=[_SEM, _SEM] + [_HBM] * (2 * n) + [pl.BlockSpec(memory_space=pltpu.VMEM)],
        out_shape=[pltpu.SemaphoreType.DMA((nsem,)), pltpu.SemaphoreType.DMA((nsem,))] + [hbm(a) for a in arrs]
        + [hbm(a) for a in lands] + [jax.ShapeDtypeStruct((SUBLANE, LANE), F32)],
        input_output_aliases={i: i + 2 for i in range(2 * n)},
        compiler_params=pltpu.CompilerParams(has_side_effects=_EFFECT),
    )(*[pltpu.with_memory_space_constraint(a, pltpu.HBM) for a in list(arrs) + lands], after)
    return (outs[0], outs[1], list(outs[2:2 + n]), list(outs[2 + n:2 + 2 * n])), outs[-1]


def exchange_wait(handle, gather, after, name):
    send_sems, recv_sems, srcs, lands = handle
    n = len(srcs)

    def body(*refs):
        src_refs, land_refs = refs[:n], refs[n:2 * n]
        for snd, rcv in _peer_copies(src_refs, land_refs, refs[2 * n], refs[2 * n + 1], gather):
            snd.wait_send()
            rcv.wait_recv()

    hbm = lambda a: pltpu.HBM(a.shape, a.dtype)
    outs = _pcall(
        body, name=name, in_specs=[_HBM] * (2 * n) + [_SEM, _SEM, pl.BlockSpec(memory_space=pl.ANY)],
        out_specs=[_HBM] * (2 * n), out_shape=[hbm(a) for a in srcs] + [hbm(a) for a in lands],
        input_output_aliases={i: i for i in range(2 * n)},
        compiler_params=pltpu.CompilerParams(has_side_effects=_EFFECT),
    )(*srcs, *lands, send_sems, recv_sems, after)
    return list(outs[n:])


def _my_index():
    return 4 * lax.axis_index("x") + 2 * lax.axis_index("y") + lax.axis_index("c")


def adamw_sum(parts, w, m, v, name):
    L, R, C = w.shape
    tr = R
    for cand in (512, 352, 256, 128, 64, 48, 32, 16, 8):
        if R % cand == 0 and cand * C * 4 <= 2 * 1024 * 1024:
            tr = cand
            break
    c1 = 1.0 - ADAM_B1 ** ADAM_STEP
    c2 = 1.0 - ADAM_B2 ** ADAM_STEP

    def body(p_ref, w_ref, m_ref, v_ref, g_ref, d_ref, nm_ref, nv_ref):
        g = p_ref[0, 0].astype(F32)
        for k in range(1, N_DEV):
            g = g + p_ref[k, 0].astype(F32)
        m2 = ADAM_B1 * m_ref[0] + (1.0 - ADAM_B1) * g
        v2 = ADAM_B2 * v_ref[0] + (1.0 - ADAM_B2) * (g * g)
        g_ref[0] = g
        nm_ref[0] = m2
        nv_ref[0] = v2
        d_ref[0] = -ADAM_LR * ((m2 / c1) / (jnp.sqrt(v2 / c2) + ADAM_EPS) + ADAM_WD * w_ref[0])

    blk = pl.BlockSpec((1, tr, C), lambda l, i: (l, i, 0))
    out = jax.ShapeDtypeStruct((L, R, C), F32)
    return _pcall(
        body, name=name, grid=(L, R // tr),
        in_specs=[pl.BlockSpec((N_DEV, 1, tr, C), lambda l, i: (0, l, i, 0)), blk, blk, blk],
        out_specs=[blk] * 4, out_shape=[out] * 4, compiler_params=_params("parallel", "parallel"),
    )(parts, w, m, v)


def adamw_layer(parts, w, m, v, layer, prev, name):
    L, R, C = w.shape
    tr = R
    for cand in (512, 352, 256, 128, 64, 48, 32, 16, 8):
        if R % cand == 0 and cand * C * 4 <= 2 * 1024 * 1024:
            tr = cand
            break
    c1 = 1.0 - ADAM_B1 ** ADAM_STEP
    c2 = 1.0 - ADAM_B2 ** ADAM_STEP
    n_prev = 0 if prev is None else 4

    def body(*refs):
        p_ref, w_ref, m_ref, v_ref = refs[:4]
        g_ref, d_ref, nm_ref, nv_ref = refs[4 + n_prev:]
        g = p_ref[0].astype(F32)
        for k in range(1, N_DEV):
            g = g + p_ref[k].astype(F32)
        m2 = ADAM_B1 * m_ref[0] + (1.0 - ADAM_B1) * g
        v2 = ADAM_B2 * v_ref[0] + (1.0 - ADAM_B2) * (g * g)
        g_ref[0] = g
        nm_ref[0] = m2
        nv_ref[0] = v2
        d_ref[0] = -ADAM_LR * ((m2 / c1) / (jnp.sqrt(v2 / c2) + ADAM_EPS) + ADAM_WD * w_ref[0])

    blk = pl.BlockSpec((1, tr, C), lambda i: (layer, i, 0))
    out = jax.ShapeDtypeStruct((L, R, C), F32)
    return _pcall(
        body, name=name, grid=(R // tr,),
        in_specs=[pl.BlockSpec((N_DEV, tr, C), lambda i: (0, i, 0)), blk, blk, blk] + [pl.BlockSpec(memory_space=pl.ANY)] * n_prev,
        out_specs=[blk] * 4, out_shape=[out] * 4, input_output_aliases={4 + j: j for j in range(n_prev)},
        compiler_params=_params("parallel"),
    )(parts, w, m, v, *(prev or ()))


def _cols_full(g):
    return jnp.transpose(g, (1, 0, 2)).reshape(g.shape[1], N_DEV * g.shape[2])


def _cols_shards(w):
    R = w.shape[0]
    return jnp.transpose(w.reshape(R, N_DEV, w.shape[1] // N_DEV), (1, 0, 2))


def _w_in_to_z(w):
    kr0 = Z_GATE
    gate0 = Z_GATE + MLA_ROPE
    pad = jnp.zeros((w.shape[0], Z_W - Z_KR - MLA_ROPE), w.dtype)
    return jnp.concatenate([w[:, :kr0], w[:, gate0:], w[:, kr0:gate0], pad], axis=1)


def _z_to_w_in(dw):
    return jnp.concatenate([dw[:, :Z_GATE], dw[:, Z_KR:Z_KR + MLA_ROPE], dw[:, Z_GATE:Z_KR]], axis=1)


def _block_diag(w):
    eye = jnp.eye(RG_BLOCKS, dtype=w.dtype)
    return (w[:, :, None, :] * eye[:, None, :, None]).reshape(D_MODEL, D_MODEL).astype(BF16)


def _diag_blocks(d):
    d4 = d.reshape(RG_BLOCKS, RG_BLOCK_W, RG_BLOCKS, RG_BLOCK_W)
    return jnp.stack([d4[n, :, n, :] for n in range(RG_BLOCKS)], axis=0)


def _uq_full(g):
    p = jnp.pad(g, ((0, 0), (0, 0), (0, 2 * HEAD - HEAD - MLA_ROPE)))
    return jnp.transpose(p, (1, 0, 2)).reshape(MLA_LORA, MLA_HEADS * 2 * HEAD)


def _uq_shards(dw):
    return jnp.transpose(dw.reshape(MLA_LORA, MLA_HEADS, 2 * HEAD), (1, 0, 2))[:, :, :HEAD + MLA_ROPE]


SMALL = ("ffn1_norm", "mix_norm", "conv_b", "rg_w_a", "rg_b_a", "rg_w_x", "rg_b_x", "rg_lambda", "mla_q_norm",
         "mla_kv_norm", "ffn2_norm", "final_norm")
BIG = ("ffn1_w_gate_up", "ffn1_w_down", "w_in", "conv_w", "mla_w_uq", "mla_w_ukv", "w_branch_a", "w_branch_b",
       "w_branch_c", "w_out", "ffn2_w_gate_up", "ffn2_w_down")
ROW_SHARDED = ("ffn1_w_down", "w_branch_a", "w_branch_b", "w_branch_c", "w_out", "ffn2_w_down")


GROUPS = {"first": ("ffn1_w_gate_up", "ffn1_w_down"), "last": ("ffn2_w_gate_up", "ffn2_w_down")}
GROUPS["mid"] = tuple(k for k in BIG if k not in GROUPS["first"] + GROUPS["last"])


def _full_weights(g):
    fw = {}
    for k, s in g.items():
        if k in ROW_SHARDED:
            fw[k] = s.reshape(-1, s.shape[-1])
        elif k == "w_in":
            fw[k] = _w_in_to_z(_cols_full(s))
        elif k == "mla_w_uq":
            fw[k] = _uq_full(s)
        else:
            fw[k] = _cols_full(s)
    return fw


def _grad_shards(dw):
    out = {}
    for k, g in dw.items():
        if k in ROW_SHARDED:
            out[k] = g.reshape(N_DEV, g.shape[0] // N_DEV, g.shape[1])
        elif k == "w_in":
            out[k] = _cols_shards(_z_to_w_in(g))
        elif k == "mla_w_uq":
            out[k] = _uq_shards(g)
        else:
            out[k] = _cols_shards(g)
    return out


def ffn_fwd(x, norm, w_gu, w_d, tag):
    h = rms_fwd(x, norm, f"{tag}_rms")
    g, u, a = ffn_up(h, w_gu, f"{tag}_up")
    y = matmul(a, w_d, "nn", F32, f"{tag}_down", scale=0.5, res=x, tk=1408)
    return y, (x, h, g, u, a)


def ffn_bwd(dy, saved, norm, w_gu, w_d, tag):
    x, h, g, u, a = saved
    dg, du = ffn_dact(dy, w_d, g, u, f"{tag}_dact")
    dw_d = matmul(a, dy, "tn", BF16, f"{tag}_dwd", scale=0.5, tm=1408)
    dw_gu = jnp.concatenate([matmul(h, dg, "tn", BF16, f"{tag}_dwg", tn=1408), matmul(h, du, "tn", BF16, f"{tag}_dwu", tn=1408)], axis=1)
    dh = matmul(dg, w_gu, "nt", F32, f"{tag}_dhg", tk=1408)
    dh = matmul(du, w_gu, "nt", F32, f"{tag}_dhu", tk=1408, res=dh, b_koff=D_FF // 1408)
    dx, dnorm = rms_bwd(x, norm, dh, dy, f"{tag}_drms")
    return dx, dw_gu, dw_d, dnorm


def mixer_fwd(x, sp, fw, cs, sn, tag):
    h = rms_fwd(x, sp["mix_norm"], f"{tag}_rms")
    z = matmul(h, fw["w_in"], "nn", BF16, f"{tag}_in", tn=1280)
    wa, wx = _block_diag(sp["rg_w_a"]), _block_diag(sp["rg_w_x"])
    ya, hs = rglru_fwd(z, fw["conv_w"], sp["conv_b"], wa, wx, sp["rg_b_a"], sp["rg_b_x"], sp["rg_lambda"], f"{tag}_rg")
    yb, ltot = sb_fwd(z, f"{tag}_sb")
    cqn, ckvn, krope = mla_prep_fwd(z, sp["mla_q_norm"], sp["mla_kv_norm"], cs, sn, f"{tag}_mprep")
    q = q_rope(matmul(cqn, fw["mla_w_uq"], "nn", F32, f"{tag}_uq"), cs, sn, 1.0, f"{tag}_qrope")
    kv = matmul(ckvn, fw["mla_w_ukv"], "nn", BF16, f"{tag}_ukv")
    yc, lse = mla_fwd(q, kv, krope, f"{tag}_mla")
    pa = matmul(ya, fw["w_branch_a"], "nn", BF16, f"{tag}_pa")
    pb = matmul(yb, fw["w_branch_b"], "nn", BF16, f"{tag}_pb")
    pc = matmul(yc, fw["w_branch_c"], "nn", BF16, f"{tag}_pc")
    merged = merge_fwd(z, pa, pb, pc, f"{tag}_merge")
    y = matmul(merged, fw["w_out"], "nn", F32, f"{tag}_out", res=x)
    return y, (x, h, z, wa, wx, ya, hs, yb, ltot, cqn, ckvn, krope, q, kv, yc, lse, pa, pb, pc, merged)


def mixer_bwd(dy, saved, sp, fw, cs, sn, tag):
    x, h, z, wa, wx, ya, hs, yb, ltot, cqn, ckvn, krope, q, kv, yc, lse, pa, pb, pc, merged = saved
    S = x.shape[0]
    dw, ds = {}, {}
    dmerged = matmul(dy, fw["w_out"], "nt", BF16, f"{tag}_dmerged")
    dw["w_out"] = matmul(merged, dy, "tn", BF16, f"{tag}_dwout")
    dpa, dpb, dpc, dga, dgb, dgc = merge_bwd(z, pa, pb, pc, dmerged, f"{tag}_dmerge")
    dya = matmul(dpa, fw["w_branch_a"], "nt", BF16, f"{tag}_dya")
    dyb = matmul(dpb, fw["w_branch_b"], "nt", BF16, f"{tag}_dyb")
    dyc = matmul(dpc, fw["w_branch_c"], "nt", BF16, f"{tag}_dyc")
    dw["w_branch_a"] = matmul(ya, dpa, "tn", BF16, f"{tag}_dwa")
    dw["w_branch_b"] = matmul(yb, dpb, "tn", BF16, f"{tag}_dwb")
    dw["w_branch_c"] = matmul(yc, dpc, "tn", BF16, f"{tag}_dwc")
    drgx, drgg, dwa, dwx, dvec = rglru_bwd(z, hs, dya, fw["conv_w"], sp["conv_b"], wa, wx, sp["rg_b_a"], sp["rg_b_x"],
                                           sp["rg_lambda"], f"{tag}_drg")
    ds["rg_w_a"], ds["rg_w_x"] = _diag_blocks(dwa), _diag_blocks(dwx)
    ds["rg_b_a"], ds["rg_b_x"], ds["rg_lambda"], ds["conv_b"] = dvec[0], dvec[1], dvec[2], dvec[3]
    dw["conv_w"] = dvec[4:8]
    dsq, dsk, dsv = sb_bwd(z, ltot, dyb, f"{tag}_dsb")
    dqp, dkv, dkr = mla_bwd(q, kv, krope, yc, lse, dyc, cs, sn, f"{tag}_dmla")
    dw["mla_w_uq"] = matmul(cqn, dqp, "tn", BF16, f"{tag}_dwuq")
    dw["mla_w_ukv"] = matmul(ckvn, dkv, "tn", BF16, f"{tag}_dwukv")
    dcqn = matmul(dqp, fw["mla_w_uq"], "nt", F32, f"{tag}_dcqn")
    dckvn = matmul(dkv, fw["mla_w_ukv"], "nt", F32, f"{tag}_dckvn")
    dcq, dckv, dkrr, dqn, dkvn = mla_prep_bwd(z, sp["mla_q_norm"], sp["mla_kv_norm"], cs, sn, dcqn, dckvn, dkr, f"{tag}_dmprep")
    ds["mla_q_norm"], ds["mla_kv_norm"] = dqn[0], dkvn[0]
    dz = jnp.concatenate([drgx, drgg, dsq.astype(BF16), dsk.astype(BF16), dsv.astype(BF16), dcq, dckv, dga, dgb, dgc, dkrr,
                          jnp.zeros((S, Z_W - Z_KR - LANE), BF16)], axis=1)
    dw["w_in"] = matmul(h, dz, "tn", BF16, f"{tag}_dwin", tn=1280)
    dh = matmul(dz, fw["w_in"], "nt", F32, f"{tag}_dh", tk=1280)
    dx, dnorm = rms_bwd(x, sp["mix_norm"], dh, dy, f"{tag}_drms")
    ds["mix_norm"] = dnorm[0]
    return dx, dw, ds


def _rope_tables(positions):
    inv = ROPE_THETA ** (-jnp.arange(0, MLA_ROPE, 2, dtype=F32) / MLA_ROPE)
    ang = positions.astype(F32)[:, None] * inv
    zeros = jnp.zeros((positions.shape[0], LANE - MLA_ROPE), F32)
    cs = jnp.concatenate([jnp.cos(ang), jnp.cos(ang), zeros], axis=1)
    sn = jnp.concatenate([jnp.sin(ang), jnp.sin(ang), zeros], axis=1)
    return cs, sn


def local_step(x, positions, target, small, fetch, emit):
    L = small["ffn1_norm"].shape[0]
    cs, sn = _rope_tables(positions)
    row = lambda v: v.reshape(1, -1)
    saved = []
    for l in range(L):
        sp = {k: small[k][l] for k in SMALL if k != "final_norm"}
        sp = {k: (v if v.ndim == 3 else row(v)) for k, v in sp.items()}
        g, token = fetch(l, "first", x)
        fw = _full_weights(g)
        if token is not None:
            sp["ffn1_norm"] = sp["ffn1_norm"] + token[0:1, 0:1]
        x, s1 = ffn_fwd(x, sp["ffn1_norm"], fw["ffn1_w_gate_up"], fw["ffn1_w_down"], f"l{l}_f1")
        g, token = fetch(l, "mid", x)
        fw.update(_full_weights(g))
        if token is not None:
            sp["mix_norm"] = sp["mix_norm"] + token[0:1, 0:1]
        x, s2 = mixer_fwd(x, sp, fw, cs, sn, f"l{l}_mx")
        g, token = fetch(l, "last", x)
        fw.update(_full_weights(g))
        if token is not None:
            sp["ffn2_norm"] = sp["ffn2_norm"] + token[0:1, 0:1]
        x, s3 = ffn_fwd(x, sp["ffn2_norm"], fw["ffn2_w_gate_up"], fw["ffn2_w_down"], f"l{l}_f2")
        saved.append((fw, sp, s1, s2, s3))
    loss, dx, dfinal = loss_head(x, row(small["final_norm"]), target, "loss_head")
    for l in reversed(range(L)):
        fw, sp, s1, s2, s3 = saved[l]
        dx, dgu2, dd2, dn2 = ffn_bwd(dx, s3, sp["ffn2_norm"], fw["ffn2_w_gate_up"], fw["ffn2_w_down"], f"l{l}_f2")
        dx = emit(l, "last", _grad_shards(dict(ffn2_w_gate_up=dgu2, ffn2_w_down=dd2)), dx)
        dx, dw, ds = mixer_bwd(dx, s2, sp, fw, cs, sn, f"l{l}_mx")
        dx = emit(l, "mid", _grad_shards(dw), dx)
        dx, dgu1, dd1, dn1 = ffn_bwd(dx, s1, sp["ffn1_norm"], fw["ffn1_w_gate_up"], fw["ffn1_w_down"], f"l{l}_f1")
        ds.update(ffn1_norm=dn1[0], ffn2_norm=dn2[0])
        if l == L - 1:
            ds["final_norm"] = dfinal[0]
        dx = emit(l, "first", _grad_shards(dict(ffn1_w_gate_up=dgu1, ffn1_w_down=dd1)), dx, ds)
    return loss[0, 0], dx


def _pack_small(tree):
    flat = jnp.concatenate([tree[k].reshape(-1).astype(F32) for k in SMALL])
    rows = -(-flat.shape[0] // (SUBLANE * D_MODEL)) * SUBLANE
    return jnp.pad(flat, (0, rows * D_MODEL - flat.shape[0])).reshape(rows, D_MODEL)


SMALL_LAYER = tuple(k for k in SMALL if k != "final_norm")


def _pack_rows(flat):
    rows = -(-flat.shape[-1] // (SUBLANE * D_MODEL)) * SUBLANE
    pad = [(0, 0)] * (flat.ndim - 1) + [(0, rows * D_MODEL - flat.shape[-1])]
    return jnp.pad(flat, pad).reshape(flat.shape[:-1] + (rows, D_MODEL))


def _pack_layer_small(ds):
    tail = ds.get("final_norm", jnp.zeros((D_MODEL,), F32))
    return _pack_rows(jnp.concatenate([ds[k].reshape(-1).astype(F32) for k in SMALL_LAYER] + [tail]))


def _small_parts(landed, like):
    L = len(landed)
    flats = [p.reshape(N_DEV, -1) for p in landed]
    pieces, off = {}, 0
    for k in SMALL_LAYER:
        n = like[k][0].size
        pieces[k] = jnp.concatenate([f[:, off:off + n] for f in flats], axis=1)
        off += n
    pieces["final_norm"] = flats[L - 1][:, off:off + D_MODEL]
    return _pack_rows(jnp.concatenate([pieces[k] for k in SMALL], axis=1))


def _unpack_small(buf, like):
    flat = buf.reshape(-1)
    out, off = {}, 0
    for k in SMALL:
        n = like[k].size
        out[k] = flat[off:off + n].reshape(like[k].shape)
        off += n
    return out


NAMES = ("ffn1_norm", "ffn1_w_gate_up", "ffn1_w_down", "mix_norm", "w_in", "conv_w", "conv_b", "rg_w_a", "rg_b_a", "rg_w_x",
         "rg_b_x", "rg_lambda", "mla_q_norm", "mla_w_uq", "mla_kv_norm", "mla_w_ukv", "w_branch_a", "w_branch_b",
         "w_branch_c", "w_out", "ffn2_norm", "ffn2_w_gate_up", "ffn2_w_down", "final_norm")


def kernel(x, positions, *rest):
    n = len(NAMES)
    w = dict(zip(NAMES, rest[:n]))
    target = rest[n]
    m = dict(zip(NAMES, rest[n + 1:2 * n + 1]))
    v = dict(zip(NAMES, rest[2 * n + 1:3 * n + 1]))
    L = w["ffn1_norm"].shape[0]
    me = _my_index()

    stages = [(0, GROUPS[p]) for p in ("first", "mid", "last")] + [(l, BIG) for l in range(1, L)]
    shard = lambda l, k: w[k][l] if k == "conv_w" else w[k][l].astype(BF16)

    pending, got = {}, {}

    def gather_start(s, after):
        l, names = stages[s]
        mine = [shard(l, k) for k in names]
        handle, token = exchange_start(mine, True, after, f"gather_start_{s}")
        pending[(l, names[0])] = (s, handle, mine)
        return token

    def fetch(l, part, after):
        token = None
        key = (l, GROUPS[part][0])
        if key in pending:
            s, handle, mine = pending.pop(key)
            landed = exchange_wait(handle, True, after, f"gather_wait_{s}")
            if s + 1 < len(stages):
                token = gather_start(s + 1, landed[0])
            filled = [lax.dynamic_update_slice_in_dim(g, a[None], me, 0) for g, a in zip(landed, mine)]
            got.update({(l, k): a for k, a in zip(stages[s][1], filled)})
        return {k: got.pop((l, k)) for k in GROUPS[part]}, token

    gather_start(0, x)

    flying, stash, res, small_landed = [], {}, {}, [None] * L

    def land(after):
        s, handle, own = flying.pop()
        l, names = stages[s]
        landed = exchange_wait(handle, False, after, f"scatter_wait_{s}")
        for k, g, o in zip(names, landed, own):
            parts = lax.dynamic_update_slice_in_dim(g, o, me, 0)
            res[k] = adamw_layer(parts, w[k], m[k], v[k], l, res.get(k), f"adamw_{k}_{l}")
        if len(landed) > len(names):
            small_landed[l] = lax.dynamic_update_slice_in_dim(landed[-1], own[-1], me, 0)
        return landed[0]

    def emit(l, part, gshards, dx, small_grads=None):
        stash.update(gshards)
        if l > 0 and part != "first":
            return dx
        s = next(i for i, (sl, names) in enumerate(stages) if sl == l and (l > 0 or names[0] == GROUPS[part][0]))
        send = [stash.pop(k) for k in stages[s][1]]
        own = [lax.dynamic_slice_in_dim(a, me, 1, 0) for a in send]
        if small_grads is not None:
            pack = _pack_layer_small(small_grads)
            send.append(jnp.broadcast_to(pack[None], (N_DEV,) + pack.shape))
            own.append(pack[None])
        after = land(dx) if flying else dx
        handle, token = exchange_start(send, False, after, f"scatter_start_{s}")
        flying.append((s, handle, own))
        return dx + token[0:1, 0:1]

    small = {k: w[k] for k in SMALL}
    loss, dx = local_step(x[0], positions[0], target[0], small, fetch, emit)
    loss = lax.psum(loss, ("x", "y", "c"))
    land(dx)
    small_parts = _small_parts(small_landed, small)
    packed = adamw_sum(small_parts[:, None], _pack_small(small)[None], _pack_small({k: m[k] for k in SMALL})[None],
                       _pack_small({k: v[k] for k in SMALL})[None], "adamw_small")
    unpacked = [_unpack_small(p[0], small) for p in packed]
    for k in SMALL:
        res[k] = tuple(u[k] for u in unpacked)

    outs = [loss, dx[None]]
    for i in range(4):
        outs += [res[k][i] for k in NAMES]
    return tuple(outs)
```

```python
import functools
import math

import jax
import jax.numpy as jnp
from jax import lax
from jax.experimental import pallas as pl
from jax.experimental.pallas import tpu as pltpu

F32 = jnp.float32
BF16 = jnp.bfloat16

N_DEV = 8
D_MODEL = 1024
D_FF = 2816
NORM_EPS = 1e-6
RG_BLOCKS = 16
RG_BLOCK_W = 64
RG_C = 8.0
SB_HEADS = 8
HEAD = 128
MLA_HEADS = 8
MLA_LORA = 256
MLA_ROPE = 64
ROPE_THETA = 10000.0
CHUNK = 64
SB_SCALE = HEAD ** -0.5
MLA_SCALE = (HEAD + MLA_ROPE) ** -0.5
N_IN = 8768

Z_RGX, Z_RGG, Z_Q, Z_K, Z_V, Z_CQ, Z_CKV, Z_GATE, Z_KR, Z_W = 0, 1024, 2048, 3072, 4096, 5120, 5376, 5632, 8704, 8960

ADAM_LR, ADAM_B1, ADAM_B2, ADAM_EPS, ADAM_WD, ADAM_STEP = 0.001, 0.9, 0.999, 1e-08, 0.01, 10

LANE = 128
SUBLANE = 8
VMEM_LIMIT = 48 * 1024 * 1024
NEG = -1e30


def _pcall(body, **kw):
    return pl.pallas_call(body, **kw)


def _params(*sem):
    return pltpu.CompilerParams(dimension_semantics=sem or None, vmem_limit_bytes=VMEM_LIMIT)


def _pick(dim, target):
    best = None
    t = LANE
    while t <= min(dim, target):
        if dim % t == 0:
            best = t
        t += LANE
    return best if best is not None else dim


def _sigmoid(x):
    return 1.0 / (1.0 + jnp.exp(-x))


def _gelu_and_grad(x):
    c = math.sqrt(2.0 / math.pi)
    inner = c * (x + 0.044715 * x * x * x)
    t = jnp.tanh(inner)
    val = 0.5 * x * (1.0 + t)
    grad = 0.5 * (1.0 + t) + 0.5 * x * (1.0 - t * t) * c * (1.0 + 3.0 * 0.044715 * x * x)
    return val, grad


def _neg_expm1(y):
    series = -y * (1.0 + y * (0.5 + y * (1.0 / 6.0 + y * (1.0 / 24.0))))
    return jnp.where(jnp.abs(y) < 0.02, series, 1.0 - jnp.exp(y))


def _dot(a, b, dims):
    return lax.dot_general(a, b, (dims, ((), ())), preferred_element_type=F32)


NN = ((1,), (0,))
NT = ((1,), (1,))
TN = ((0,), (0,))


def matmul(a, b, mode, out_dtype, name, scale=1.0, res=None, tm=1024, tn=1024, tk=1024, b_koff=0):
    if mode == "nn":
        (M, K), N = a.shape, b.shape[1]
    elif mode == "nt":
        (M, K), N = a.shape, b.shape[0]
    else:
        (K, M), N = a.shape, b.shape[1]
    tm, tn, tk = _pick(M, tm), _pick(N, tn), _pick(K, tk)
    nk = K // tk
    dims = {"nn": NN, "nt": NT, "tn": TN}[mode]

    def body(*refs):
        if res is None:
            a_ref, b_ref, o_ref, acc = refs
        else:
            a_ref, b_ref, r_ref, o_ref, acc = refs
        k = pl.program_id(2)

        @pl.when(k == 0)
        def _():
            acc[...] = jnp.zeros_like(acc)

        acc[...] += _dot(a_ref[...].astype(BF16), b_ref[...].astype(BF16), dims)

        @pl.when(k == nk - 1)
        def _():
            r = acc[...] * scale
            if res is not None:
                r = r + r_ref[...]
            o_ref[...] = r.astype(out_dtype)

    a_spec = pl.BlockSpec((tk, tm), lambda i, j, k: (k, i)) if mode == "tn" else pl.BlockSpec((tm, tk), lambda i, j, k: (i, k))
    b_spec = pl.BlockSpec((tn, tk), lambda i, j, k: (j, k + b_koff)) if mode == "nt" else pl.BlockSpec((tk, tn), lambda i, j, k: (k, j))
    o_spec = pl.BlockSpec((tm, tn), lambda i, j, k: (i, j))
    in_specs = [a_spec, b_spec] + ([o_spec] if res is not None else [])
    args = (a, b) + ((res,) if res is not None else ())
    return _pcall(
        body, name=name, grid=(M // tm, N // tn, nk), in_specs=in_specs, out_specs=o_spec,
        out_shape=jax.ShapeDtypeStruct((M, N), out_dtype), scratch_shapes=[pltpu.VMEM((tm, tn), F32)],
        compiler_params=_params("parallel", "parallel", "arbitrary"),
    )(*args)


def rms_fwd(x, g, name, col=0):
    S, D = x.shape[0], g.shape[1]
    tr = _pick(S, 512)

    def body(x_ref, g_ref, o_ref):
        xv = x_ref[...].astype(F32)
        r = lax.rsqrt(jnp.mean(xv * xv, axis=-1, keepdims=True) + NORM_EPS)
        o_ref[...] = (xv * r * g_ref[...]).astype(BF16)

    return _pcall(
        body, name=name, grid=(S // tr,),
        in_specs=[pl.BlockSpec((tr, D), lambda i: (i, col)), pl.BlockSpec((1, D), lambda i: (0, 0))],
        out_specs=pl.BlockSpec((tr, D), lambda i: (i, 0)),
        out_shape=jax.ShapeDtypeStruct((S, D), BF16), compiler_params=_params("parallel"),
    )(x, g)


def _rms_bwd_math(xv, g, dh):
    r = lax.rsqrt(jnp.mean(xv * xv, axis=-1, keepdims=True) + NORM_EPS)
    xhat = xv * r
    dxhat = dh * g
    dx = r * (dxhat - xhat * jnp.mean(dxhat * xhat, axis=-1, keepdims=True))
    dg = jnp.sum(dh * xhat, axis=0, keepdims=True)
    return dx, dg


def rms_bwd(x, g, dh, dres, name):
    S, D = x.shape
    tr = _pick(S, 512)

    def body(x_ref, g_ref, dh_ref, dr_ref, dx_ref, dg_ref):
        dx, dg = _rms_bwd_math(x_ref[...], g_ref[...], dh_ref[...])
        dx_ref[...] = dx + dr_ref[...]

        @pl.when(pl.program_id(0) == 0)
        def _():
            dg_ref[...] = jnp.zeros_like(dg_ref)

        dg_ref[...] += dg

    row = pl.BlockSpec((tr, D), lambda i: (i, 0))
    vec = pl.BlockSpec((1, D), lambda i: (0, 0))
    return _pcall(
        body, name=name, grid=(S // tr,), in_specs=[row, vec, row, row], out_specs=[row, vec],
        out_shape=[jax.ShapeDtypeStruct((S, D), F32), jax.ShapeDtypeStruct((1, D), F32)],
        compiler_params=_params("arbitrary"),
    )(x, g, dh, dres)


def ffn_up(h, w_gu, name):
    S, D = h.shape
    tm, tn = _pick(S, 512), D_FF // 2
    nc = D_FF // tn

    def body(h_ref, wg_ref, wu_ref, g_ref, u_ref, a_ref):
        hv = h_ref[...]
        g = _dot(hv, wg_ref[...], NN)
        u = _dot(hv, wu_ref[...], NN)
        g_ref[...] = g.astype(BF16)
        u_ref[...] = u.astype(BF16)
        a_ref[...] = (g * _sigmoid(g) * u).astype(BF16)

    blk = pl.BlockSpec((tm, tn), lambda j, i: (i, j))
    out = jax.ShapeDtypeStruct((S, D_FF), BF16)
    return _pcall(
        body, name=name, grid=(nc, S // tm),
        in_specs=[pl.BlockSpec((tm, D), lambda j, i: (i, 0)), pl.BlockSpec((D, tn), lambda j, i: (0, j)),
                  pl.BlockSpec((D, tn), lambda j, i: (0, j + nc))],
        out_specs=[blk] * 3, out_shape=[out] * 3, compiler_params=_params("parallel", "parallel"),
    )(h, w_gu, w_gu)


def ffn_dact(dy, w_d, g, u, name):
    S, D = dy.shape
    tm, tn = _pick(S, 512), D_FF // 2
    nc = D_FF // tn

    def body(dy_ref, wd_ref, g_ref, u_ref, dg_ref, du_ref):
        da = _dot(dy_ref[...].astype(BF16), wd_ref[...], NT) * 0.5
        gv = g_ref[...].astype(F32)
        sg = _sigmoid(gv)
        dg_ref[...] = (da * u_ref[...].astype(F32) * sg * (1.0 + gv * (1.0 - sg))).astype(BF16)
        du_ref[...] = (da * gv * sg).astype(BF16)

    blk = pl.BlockSpec((tm, tn), lambda j, i: (i, j))
    out = jax.ShapeDtypeStruct((S, D_FF), BF16)
    return _pcall(
        body, name=name, grid=(nc, S // tm),
        in_specs=[pl.BlockSpec((tm, D), lambda j, i: (i, 0)), pl.BlockSpec((tn, D), lambda j, i: (j, 0)), blk, blk],
        out_specs=[blk] * 2, out_shape=[out] * 2, compiler_params=_params("parallel", "parallel"),
    )(dy, w_d, g, u)


def _conv_taps(xpad, T, cw, cb):
    u = cb + cw[3:4, :] * xpad[pl.ds(8, T), :]
    for tap in range(3):
        u = u + cw[tap:tap + 1, :] * xpad[pl.ds(5 + tap, T), :]
    return u


def _rg_gates(u, wa_ref, wx_ref, ba, bx, lam):
    ub = u.astype(BF16)
    r = _sigmoid(_dot(ub, wa_ref[...], NN) + ba)
    ig = _sigmoid(_dot(ub, wx_ref[...], NN) + bx)
    nlam = -lam
    clam = -RG_C * (jnp.maximum(nlam, 0.0) + jnp.log(1.0 + jnp.exp(-jnp.abs(nlam))))
    la = clam * r
    return r, ig, clam, la


def rglru_fwd(z, cw, cb, wa, wx, ba, bx, lam, name):
    S, D = z.shape[0], D_MODEL
    T = _pick(S, 256)

    def body(x_ref, g_ref, cw_ref, cb_ref, wa_ref, wx_ref, ba_ref, bx_ref, lam_ref, y_ref, h_ref, xpad, a_s, b_s, hst):
        @pl.when(pl.program_id(0) == 0)
        def _():
            xpad[pl.ds(0, 8), :] = jnp.zeros((8, D), F32)
            hst[...] = jnp.zeros_like(hst)

        xpad[pl.ds(8, T), :] = x_ref[...].astype(F32)
        u = _conv_taps(xpad, T, cw_ref[...], cb_ref[...])
        xpad[pl.ds(0, 8), :] = xpad[pl.ds(T, 8), :]
        r, ig, clam, la = _rg_gates(u, wa_ref, wx_ref, ba_ref[...], bx_ref[...], lam_ref[...])
        a_s[...] = jnp.exp(la)
        b_s[...] = jnp.sqrt(_neg_expm1(2.0 * la)) * (ig * u)

        def tile(j, h):
            r0 = pl.multiple_of(j * 8, 8)
            av = a_s[pl.ds(r0, 8), :]
            bv = b_s[pl.ds(r0, 8), :]
            rows = []
            for k in range(8):
                h = av[k:k + 1, :] * h + bv[k:k + 1, :]
                rows.append(h)
            h_ref[pl.ds(r0, 8), :] = jnp.concatenate(rows, axis=0)
            return h

        hst[...] = lax.fori_loop(0, T // 8, tile, hst[...])
        gel, _ = _gelu_and_grad(g_ref[...].astype(F32))
        y_ref[...] = (h_ref[...] * gel).astype(BF16)

    blk = lambda c: pl.BlockSpec((T, D), lambda i: (i, c))
    vec = pl.BlockSpec((1, D), lambda i: (0, 0))
    full = lambda r: pl.BlockSpec((r, D), lambda i: (0, 0))
    return _pcall(
        body, name=name, grid=(S // T,),
        in_specs=[blk(0), blk(1), full(4), vec, full(D), full(D), vec, vec, vec],
        out_specs=[blk(0), blk(0)],
        out_shape=[jax.ShapeDtypeStruct((S, D), BF16), jax.ShapeDtypeStruct((S, D), F32)],
        scratch_shapes=[pltpu.VMEM((T + 8, D), F32), pltpu.VMEM((T, D), F32), pltpu.VMEM((T, D), F32), pltpu.VMEM((1, D), F32)],
        compiler_params=_params("arbitrary"),
    )(z, z, cw, cb, wa, wx, ba, bx, lam)


def rglru_bwd(z, hs, dy, cw, cb, wa, wx, ba, bx, lam, name):
    S, D = z.shape[0], D_MODEL
    T = _pick(S, 256)
    nb = S // T
    t8 = T // 8

    def body(x_ref, xp_ref, g_ref, h_ref, hp_ref, dy_ref, cw_ref, cb_ref, wa_ref, wx_ref, ba_ref, bx_ref, lam_ref,
             dx_ref, dg_ref, dwa_ref, dwx_ref, dvec_ref, xpad, hpad, dupad, a_s, d_s, carry):
        i = pl.program_id(0)
        first_block = i == nb - 1

        @pl.when(i == 0)
        def _():
            dwa_ref[...] = jnp.zeros_like(dwa_ref)
            dwx_ref[...] = jnp.zeros_like(dwx_ref)
            dvec_ref[...] = jnp.zeros_like(dvec_ref)
            carry[...] = jnp.zeros_like(carry)
            dupad[pl.ds(T, 8), :] = jnp.zeros((8, D), F32)

        keep = jnp.where(first_block, 0.0, 1.0)
        xpad[pl.ds(0, 8), :] = xp_ref[...].astype(F32) * keep
        xpad[pl.ds(8, T), :] = x_ref[...].astype(F32)
        hpad[pl.ds(0, 8), :] = hp_ref[...] * keep
        hpad[pl.ds(8, T), :] = h_ref[...]
        cwv = cw_ref[...]
        u = _conv_taps(xpad, T, cwv, cb_ref[...])
        r, ig, clam, la = _rg_gates(u, wa_ref, wx_ref, ba_ref[...], bx_ref[...], lam_ref[...])
        a = jnp.exp(la)
        a_s[...] = a
        gv = g_ref[...].astype(F32)
        gel, dgel = _gelu_and_grad(gv)
        dyv = dy_ref[...].astype(F32)
        d_s[...] = dyv * gel
        dg_ref[...] = (dyv * h_ref[...] * dgel).astype(BF16)

        def tile(j, c):
            r0 = pl.multiple_of((t8 - 1 - j) * 8, 8)
            av = a_s[pl.ds(r0, 8), :]
            dv = d_s[pl.ds(r0, 8), :]
            rows = [None] * 8
            for k in range(7, -1, -1):
                d = dv[k:k + 1, :] + c
                rows[k] = d
                c = av[k:k + 1, :] * d
            d_s[pl.ds(r0, 8), :] = jnp.concatenate(rows, axis=0)
            return c

        carry[...] = lax.fori_loop(0, t8, tile, carry[...])
        dht = d_s[...]
        hprev = hpad[pl.ds(7, T), :]
        w = _neg_expm1(2.0 * la)
        s = jnp.sqrt(w)
        e2 = 1.0 - w
        d_iu = dht * s
        dla = dht * hprev * a - dht * (ig * u) * e2 / s
        dpr = (dla * clam * r * (1.0 - r))
        dpi = (d_iu * u * ig * (1.0 - ig))
        dprb, dpib, ub = dpr.astype(BF16), dpi.astype(BF16), u.astype(BF16)
        du = d_iu * ig + _dot(dprb, wa_ref[...], NT) + _dot(dpib, wx_ref[...], NT)
        dwa_ref[...] += _dot(ub, dprb, TN)
        dwx_ref[...] += _dot(ub, dpib, TN)
        dvec_ref[0:1, :] += jnp.sum(dpr, axis=0, keepdims=True)
        dvec_ref[1:2, :] += jnp.sum(dpi, axis=0, keepdims=True)
        dvec_ref[2:3, :] += jnp.sum(dla * r, axis=0, keepdims=True)
        dvec_ref[3:4, :] += jnp.sum(du, axis=0, keepdims=True)
        for tap in range(4):
            dvec_ref[4 + tap:5 + tap, :] += jnp.sum(du * xpad[pl.ds(5 + tap, T), :], axis=0, keepdims=True)
        dupad[pl.ds(0, T), :] = du
        dx = cwv[3:4, :] * du
        for tap in range(3):
            dx = dx + cwv[tap:tap + 1, :] * dupad[pl.ds(3 - tap, T), :]
        dx_ref[...] = dx.astype(BF16)
        dupad[pl.ds(T, 8), :] = dupad[pl.ds(0, 8), :]

        @pl.when(first_block)
        def _():
            dvec_ref[2:3, :] = dvec_ref[2:3, :] * (RG_C * _sigmoid(-lam_ref[...]))

    rev = lambda c: pl.BlockSpec((T, D), lambda i: (nb - 1 - i, c))
    prev = lambda c: pl.BlockSpec((8, D), lambda i: (jnp.maximum((nb - 1 - i) * t8 - 1, 0), c))
    vec = pl.BlockSpec((1, D), lambda i: (0, 0))
    full = lambda r: pl.BlockSpec((r, D), lambda i: (0, 0))
    return _pcall(
        body, name=name, grid=(nb,),
        in_specs=[rev(0), prev(0), rev(1), rev(0), prev(0), rev(0), full(4), vec, full(D), full(D), vec, vec, vec],
        out_specs=[rev(0), rev(0), full(D), full(D), full(8)],
        out_shape=[jax.ShapeDtypeStruct((S, D), BF16), jax.ShapeDtypeStruct((S, D), BF16),
                   jax.ShapeDtypeStruct((D, D), F32), jax.ShapeDtypeStruct((D, D), F32), jax.ShapeDtypeStruct((8, D), F32)],
        scratch_shapes=[pltpu.VMEM((T + 8, D), F32), pltpu.VMEM((T + 8, D), F32), pltpu.VMEM((T + 8, D), F32),
                        pltpu.VMEM((T, D), F32), pltpu.VMEM((T, D), F32), pltpu.VMEM((1, D), F32)],
        compiler_params=_params("arbitrary"),
    )(z, z, z, hs, hs, dy, cw, cb, wa, wx, ba, bx, lam)


def _tri(n, kind):
    j = lax.broadcasted_iota(jnp.int32, (n, n), 0)
    s = lax.broadcasted_iota(jnp.int32, (n, n), 1)
    m = {"gt": j > s, "le": j <= s, "lt": j < s}[kind]
    return jnp.where(m, 1.0, 0.0).astype(BF16)


def _dot2(x, tri):
    hi = x.astype(BF16)
    lo = (x - hi.astype(F32)).astype(BF16)
    return _dot(jnp.concatenate([hi, lo], axis=1), jnp.concatenate([tri, tri], axis=0), NN)


SB_TK = 128

def _sb_logits(q, kblk, q0, k0, tq, masked):
    z = _dot(q, kblk, NT) * SB_SCALE
    sp = jnp.maximum(z, 0.0) + jnp.log(1.0 + jnp.exp(-jnp.abs(z)))
    lkeep = -sp
    mask = None
    if masked:
        tpos = q0 + lax.broadcasted_iota(jnp.int32, (tq, SB_TK), 0)
        spos = k0 + lax.broadcasted_iota(jnp.int32, (tq, SB_TK), 1)
        mask = spos < tpos
        lkeep = jnp.where(mask, lkeep, 0.0)
    return mask, lkeep, z - sp


def sb_fwd(z, name):
    S = z.shape[0]
    tq, tk = _pick(S, 512), SB_TK
    nd = tq // tk
    U = min(4, nd)
    qc, kc, vc = Z_Q // HEAD, Z_K // HEAD, Z_V // HEAD

    def body(q_ref, k_ref, v_ref, o_ref, lt_ref, acc, run):
        qi = pl.program_id(1)
        q0 = qi * tq
        q = q_ref[...]
        tri = _tri(tk, "gt")
        acc[...] = jnp.zeros_like(acc)
        run[...] = jnp.zeros_like(run)

        def group(k0s, masked):
            parts = [(k0,) + _sb_logits(q, k_ref[pl.ds(k0, tk), :], q0, k0, tq, masked) for k0 in k0s]
            cums = [_dot2(p[2], tri) for p in parts]
            r, a = run[...], acc[...]
            for (k0, mask, lkeep, lbeta), cum in zip(parts, cums):
                w = jnp.exp(lbeta + cum + r)
                if masked:
                    w = jnp.where(mask, w, 0.0)
                a = a + _dot(w.astype(BF16), v_ref[pl.ds(k0, tk), :], NN)
                r = r + jnp.sum(lkeep, axis=1, keepdims=True)
            acc[...] = a
            run[...] = r

        for g in range(nd // U):
            group([pl.multiple_of(q0 + (nd - 1 - g * U - u) * tk, tk) for u in range(U)], True)

        def step(i, c):
            base = qi * nd - 1 - i * U
            group([pl.multiple_of((base - u) * tk, tk) for u in range(U)], False)
            return c

        lax.fori_loop(0, qi * nd // U, step, 0)
        o_ref[...] = acc[...].astype(BF16)
        lt_ref[0] = run[...]

    return _pcall(
        body, name=name, grid=(SB_HEADS, S // tq),
        in_specs=[pl.BlockSpec((tq, HEAD), lambda h, i: (i, qc + h)), pl.BlockSpec((S, HEAD), lambda h, i: (0, kc + h)),
                  pl.BlockSpec((S, HEAD), lambda h, i: (0, vc + h))],
        out_specs=[pl.BlockSpec((tq, HEAD), lambda h, i: (i, h)), pl.BlockSpec((1, tq, 1), lambda h, i: (h, i, 0))],
        out_shape=[jax.ShapeDtypeStruct((S, SB_HEADS * HEAD), BF16), jax.ShapeDtypeStruct((SB_HEADS, S, 1), F32)],
        scratch_shapes=[pltpu.VMEM((tq, HEAD), F32), pltpu.VMEM((tq, 1), F32)],
        compiler_params=_params("parallel", "parallel"),
    )(z, z, z)


def sb_bwd(z, ltot, dy, name):
    S = z.shape[0]
    tq, tk = _pick(S, 512), SB_TK
    nd = tq // tk
    U = min(4, nd)
    nkb = S // tk
    qc, kc, vc = Z_Q // HEAD, Z_K // HEAD, Z_V // HEAD

    def body(q_ref, k_ref, v_ref, lt_ref, do_ref, dq_ref, dk_ref, dv_ref, dq_s, run_l, run_g, dkT, dvT):
        qi = pl.program_id(1)
        q0 = qi * tq

        @pl.when(qi == 0)
        def _():
            dkT[...] = jnp.zeros_like(dkT)
            dvT[...] = jnp.zeros_like(dvT)

        q = q_ref[...]
        do = do_ref[...].astype(BF16)
        qT, doT = q.T, do.T
        ltv = lt_ref[0]
        tri_le, tri_lt = _tri(tk, "le"), _tri(tk, "lt")
        dq_s[...] = jnp.zeros_like(dq_s)
        run_l[...] = jnp.zeros_like(run_l)
        run_g[...] = jnp.zeros_like(run_g)

        def group(k0s, masked):
            parts = []
            for k0 in k0s:
                kblk = k_ref[pl.ds(k0, tk), :]
                mask, lkeep, lbeta = _sb_logits(q, kblk, q0, k0, tq, masked)
                parts.append((k0, kblk, mask, lkeep, lbeta, _dot(do, v_ref[pl.ds(k0, tk), :], NT)))
            pres = [_dot2(p[3], tri_le) for p in parts]
            rl = run_l[...]
            ws = []
            for (k0, kblk, mask, lkeep, lbeta, dw), pre in zip(parts, pres):
                w = jnp.exp(lbeta + (ltv - (pre + rl)))
                if masked:
                    w = jnp.where(mask, w, 0.0)
                ws.append((w, w * dw))
                rl = rl + jnp.sum(lkeep, axis=1, keepdims=True)
            run_l[...] = rl
            gpres = [_dot2(g, tri_lt) for _, g in ws]
            rg, dq = run_g[...], dq_s[...]
            for (k0, kblk, mask, lkeep, lbeta, dw), (w, g), gpre in zip(parts, ws, gpres):
                dz = (g * jnp.exp(lkeep) - jnp.exp(lbeta) * (gpre + rg)) * SB_SCALE
                if masked:
                    dz = jnp.where(mask, dz, 0.0)
                dz = dz.astype(BF16)
                dq = dq + _dot(dz, kblk, NN)
                kb = k0 // tk
                dkT[kb] += _dot(qT, dz, NN)
                dvT[kb] += _dot(doT, w.astype(BF16), NN)
                rg = rg + jnp.sum(g, axis=1, keepdims=True)
            run_g[...] = rg
            dq_s[...] = dq

        def step(i, c):
            group([pl.multiple_of((i * U + u) * tk, tk) for u in range(U)], False)
            return c

        lax.fori_loop(0, qi * nd // U, step, 0)
        for g in range(nd // U):
            group([pl.multiple_of(q0 + (g * U + u) * tk, tk) for u in range(U)], True)
        dq_ref[...] = dq_s[...].astype(BF16)

        @pl.when(qi == pl.num_programs(1) - 1)
        def _():
            def flush(kb, c):
                r0 = pl.multiple_of(kb * tk, tk)
                dk_ref[pl.ds(r0, tk), :] = dkT[kb].T.astype(BF16)
                dv_ref[pl.ds(r0, tk), :] = dvT[kb].T.astype(BF16)
                return c

            lax.fori_loop(0, nkb, flush, 0)

    qblk = lambda c: pl.BlockSpec((tq, HEAD), lambda h, i: (i, c + h))
    kfull = lambda c: pl.BlockSpec((S, HEAD), lambda h, i: (0, c + h))
    out = jax.ShapeDtypeStruct((S, SB_HEADS * HEAD), BF16)
    return _pcall(
        body, name=name, grid=(SB_HEADS, S // tq),
        in_specs=[qblk(qc), kfull(kc), kfull(vc), pl.BlockSpec((1, tq, 1), lambda h, i: (h, i, 0)), qblk(0)],
        out_specs=[qblk(0), kfull(0), kfull(0)], out_shape=[out, out, out],
        scratch_shapes=[pltpu.VMEM((tq, HEAD), F32), pltpu.VMEM((tq, 1), F32), pltpu.VMEM((tq, 1), F32),
                        pltpu.VMEM((nkb, HEAD, tk), F32), pltpu.VMEM((nkb, HEAD, tk), F32)],
        compiler_params=_params("arbitrary", "arbitrary"),
    )(z, z, z, ltot, dy)


def _rope(x, cs, sn, sign):
    lane = lax.broadcasted_iota(jnp.int32, x.shape, 1)
    swapped = jnp.where(lane < MLA_ROPE // 2, -pltpu.roll(x, LANE - MLA_ROPE // 2, 1), pltpu.roll(x, MLA_ROPE // 2, 1))
    return x * cs + sign * swapped * sn


def mla_prep_fwd(z, qn, kvn, cs, sn, name):
    S = z.shape[0]
    tr = _pick(S, 512)

    def body(cq_ref, ckv_ref, kr_ref, qn_ref, kvn_ref, cs_ref, sn_ref, oq_ref, okv_ref, okr_ref):
        for src, g, dst in ((cq_ref, qn_ref, oq_ref), (ckv_ref, kvn_ref, okv_ref)):
            xv = src[...].astype(F32)
            r = lax.rsqrt(jnp.mean(xv * xv, axis=-1, keepdims=True) + NORM_EPS)
            dst[...] = (xv * r * g[...]).astype(BF16)
        okr_ref[...] = _rope(kr_ref[...].astype(F32), cs_ref[...], sn_ref[...], 1.0).astype(BF16)

    lora = lambda c: pl.BlockSpec((tr, MLA_LORA), lambda i: (i, c))
    tile = lambda c: pl.BlockSpec((tr, LANE), lambda i: (i, c))
    vec = pl.BlockSpec((1, MLA_LORA), lambda i: (0, 0))
    return _pcall(
        body, name=name, grid=(S // tr,),
        in_specs=[lora(Z_CQ // MLA_LORA), lora(Z_CKV // MLA_LORA), tile(Z_KR // LANE), vec, vec, tile(0), tile(0)],
        out_specs=[lora(0), lora(0), tile(0)],
        out_shape=[jax.ShapeDtypeStruct((S, MLA_LORA), BF16), jax.ShapeDtypeStruct((S, MLA_LORA), BF16),
                   jax.ShapeDtypeStruct((S, LANE), BF16)],
        compiler_params=_params("parallel"),
    )(z, z, z, qn, kvn, cs, sn)


def mla_prep_bwd(z, qn, kvn, cs, sn, dcqn, dckvn, dkrope, name):
    S = z.shape[0]
    tr = _pick(S, 512)

    def body(cq_ref, ckv_ref, qn_ref, kvn_ref, cs_ref, sn_ref, dq_ref, dkv_ref, dkr_ref, oq_ref, okv_ref, okr_ref, gq_ref, gkv_ref):
        @pl.when(pl.program_id(0) == 0)
        def _():
            gq_ref[...] = jnp.zeros_like(gq_ref)
            gkv_ref[...] = jnp.zeros_like(gkv_ref)

        for src, g, dh, dst, gacc in ((cq_ref, qn_ref, dq_ref, oq_ref, gq_ref), (ckv_ref, kvn_ref, dkv_ref, okv_ref, gkv_ref)):
            dx, dg = _rms_bwd_math(src[...].astype(F32), g[...], dh[...])
            dst[...] = dx.astype(BF16)
            gacc[...] += dg
        okr_ref[...] = _rope(dkr_ref[...], cs_ref[...], sn_ref[...], -1.0).astype(BF16)

    lora = lambda c: pl.BlockSpec((tr, MLA_LORA), lambda i: (i, c))
    tile = lambda c: pl.BlockSpec((tr, LANE), lambda i: (i, c))
    vec = pl.BlockSpec((1, MLA_LORA), lambda i: (0, 0))
    return _pcall(
        body, name=name, grid=(S // tr,),
        in_specs=[lora(Z_CQ // MLA_LORA), lora(Z_CKV // MLA_LORA), vec, vec, tile(0), tile(0), lora(0), lora(0), tile(0)],
        out_specs=[lora(0), lora(0), tile(0), vec, vec],
        out_shape=[jax.ShapeDtypeStruct((S, MLA_LORA), BF16), jax.ShapeDtypeStruct((S, MLA_LORA), BF16),
                   jax.ShapeDtypeStruct((S, LANE), BF16), jax.ShapeDtypeStruct((1, MLA_LORA), F32), jax.ShapeDtypeStruct((1, MLA_LORA), F32)],
        compiler_params=_params("arbitrary"),
    )(z, z, qn, kvn, cs, sn, dcqn, dckvn, dkrope)


def q_rope(q, cs, sn, sign, name):
    S = q.shape[0]
    tr = _pick(S, 512)

    def body(q_ref, cs_ref, sn_ref, o_ref):
        o_ref[:, 0:LANE] = q_ref[:, 0:LANE].astype(BF16)
        o_ref[:, LANE:2 * LANE] = _rope(q_ref[:, LANE:2 * LANE], cs_ref[...], sn_ref[...], sign).astype(BF16)

    blk = pl.BlockSpec((tr, 2 * LANE), lambda i, h: (i, h))
    tile = pl.BlockSpec((tr, LANE), lambda i, h: (i, 0))
    return _pcall(
        body, name=name, grid=(S // tr, MLA_HEADS), in_specs=[blk, tile, tile], out_specs=blk,
        out_shape=jax.ShapeDtypeStruct(q.shape, BF16), compiler_params=_params("parallel", "parallel"),
    )(q, cs, sn)


def uq_rope(a, b, cs, sn, name):
    S, kd = a.shape
    N = b.shape[1]
    tm, tn = _pick(S, 1024), _pick(N, 1024)

    def body(a_ref, b_ref, cs_ref, sn_ref, o_ref):
        r = _dot(a_ref[...], b_ref[...], NN)
        c, s = cs_ref[...], sn_ref[...]
        for h in range(tn // (2 * LANE)):
            lo = h * 2 * LANE
            o_ref[:, lo:lo + LANE] = r[:, lo:lo + LANE].astype(BF16)
            o_ref[:, lo + LANE:lo + 2 * LANE] = _rope(r[:, lo + LANE:lo + 2 * LANE], c, s, 1.0).astype(BF16)

    tile = pl.BlockSpec((tm, LANE), lambda i, j: (i, 0))
    return _pcall(
        body, name=name, grid=(S // tm, N // tn),
        in_specs=[pl.BlockSpec((tm, kd), lambda i, j: (i, 0)), pl.BlockSpec((kd, tn), lambda i, j: (0, j)), tile, tile],
        out_specs=pl.BlockSpec((tm, tn), lambda i, j: (i, j)), out_shape=jax.ShapeDtypeStruct((S, N), BF16),
        compiler_params=_params("parallel", "parallel"),
    )(a, b, cs, sn)


def _chunk_mask(rows, cols):
    tch = lax.broadcasted_iota(jnp.int32, (rows, cols), 0) // CHUNK
    sch = lax.broadcasted_iota(jnp.int32, (rows, cols), 1) // CHUNK
    return sch <= tch


def _fill_kcat(kcat, kv_ref, kr_ref):
    kcat[:, 0:HEAD] = kv_ref[:, 0:HEAD]
    kcat[:, HEAD:2 * HEAD] = kr_ref[...]


def mla_fwd(q, kv, kr, name):
    S = q.shape[0]
    t = _pick(S, 512)

    def body(q_ref, kv_ref, kr_ref, o_ref, lse_ref, kcat, m_s, l_s, acc):
        qi = pl.program_id(1)

        @pl.when(qi == 0)
        def _():
            _fill_kcat(kcat, kv_ref, kr_ref)

        q = q_ref[...]
        m_s[...] = jnp.full_like(m_s, NEG)
        l_s[...] = jnp.zeros_like(l_s)
        acc[...] = jnp.zeros_like(acc)

        def block(k0, width, masked):
            s = _dot(q, kcat[pl.ds(k0, width), :], NT) * MLA_SCALE
            if masked:
                s = jnp.where(_chunk_mask(t, width), s, NEG)
            m = m_s[...]
            m2 = jnp.maximum(m, jnp.max(s, axis=1, keepdims=True))
            p = jnp.exp(s - m2)
            alpha = jnp.exp(m - m2)
            l_s[...] = alpha * l_s[...] + jnp.sum(p, axis=1, keepdims=True)
            acc[...] = alpha * acc[...] + _dot(p.astype(BF16), kv_ref[pl.ds(k0, width), HEAD:2 * HEAD], NN)
            m_s[...] = m2

        if S >= 2 * t:
            def step(i, c):
                block(pl.multiple_of(i * 2 * t, 2 * t), 2 * t, False)
                return c

            lax.fori_loop(0, qi // 2, step, 0)

            @pl.when(qi % 2 == 1)
            def _():
                block(pl.multiple_of((qi - 1) * t, t), t, False)

        block(pl.multiple_of(qi * t, t), t, True)
        o_ref[...] = (acc[...] / l_s[...]).astype(BF16)
        lse_ref[0] = m_s[...] + jnp.log(l_s[...])

    return _pcall(
        body, name=name, grid=(MLA_HEADS, S // t),
        in_specs=[pl.BlockSpec((t, 2 * HEAD), lambda h, i: (i, h)), pl.BlockSpec((S, 2 * HEAD), lambda h, i: (0, h)),
                  pl.BlockSpec((S, LANE), lambda h, i: (0, 0))],
        out_specs=[pl.BlockSpec((t, HEAD), lambda h, i: (i, h)), pl.BlockSpec((1, t, 1), lambda h, i: (h, i, 0))],
        out_shape=[jax.ShapeDtypeStruct((S, MLA_HEADS * HEAD), BF16), jax.ShapeDtypeStruct((MLA_HEADS, S, 1), F32)],
        scratch_shapes=[pltpu.VMEM((S, 2 * HEAD), BF16), pltpu.VMEM((t, 1), F32), pltpu.VMEM((t, 1), F32), pltpu.VMEM((t, HEAD), F32)],
        compiler_params=_params("arbitrary", "arbitrary"),
    )(q, kv, kr)


def mla_bwd(q, kv, kr, o, lse, do, cs, sn, name):
    S = q.shape[0]
    t = _pick(S, 512)
    nkb = S // t

    def body(q_ref, kv_ref, kr_ref, o_ref, lse_ref, do_ref, cs_ref, sn_ref, dq_ref, dkv_ref, dkr_ref, kcat, dq_s, dkT, dvT, dkrT):
        h, qi = pl.program_id(0), pl.program_id(1)

        @pl.when(qi == 0)
        def _():
            _fill_kcat(kcat, kv_ref, kr_ref)
            dkT[...] = jnp.zeros_like(dkT)
            dvT[...] = jnp.zeros_like(dvT)

        @pl.when((qi == 0) & (h == 0))
        def _():
            dkrT[...] = jnp.zeros_like(dkrT)

        q = q_ref[...]
        dov = do_ref[...].astype(F32)
        dob = dov.astype(BF16)
        qT, doT = q.T, dob.T
        delta = jnp.sum(dov * o_ref[...].astype(F32), axis=1, keepdims=True)
        lsev = lse_ref[0]
        dq_s[...] = jnp.zeros_like(dq_s)

        def block(kb, masked):
            k0 = pl.multiple_of(kb * t, t)
            kc = kcat[pl.ds(k0, t), :]
            p = jnp.exp(_dot(q, kc, NT) * MLA_SCALE - lsev)
            if masked:
                p = jnp.where(_chunk_mask(t, t), p, 0.0)
            ds = (p * (_dot(dob, kv_ref[pl.ds(k0, t), HEAD:2 * HEAD], NT) - delta) * MLA_SCALE).astype(BF16)
            dkT[kb] += _dot(qT, ds, NN)
            dvT[kb] += _dot(doT, p.astype(BF16), NN)
            dq_s[...] += _dot(ds, kc, NN)

        def step(kb, c):
            block(kb, False)
            return c

        lax.fori_loop(0, qi, step, 0)
        block(qi, True)
        dq_ref[:, 0:HEAD] = dq_s[:, 0:HEAD].astype(BF16)
        dq_ref[:, HEAD:2 * HEAD] = _rope(dq_s[:, HEAD:2 * HEAD], cs_ref[...], sn_ref[...], -1.0).astype(BF16)
        last_q = qi == pl.num_programs(1) - 1

        @pl.when(last_q)
        def _():
            def flush(kb, c):
                r0 = pl.multiple_of(kb * t, t)
                dkv_ref[pl.ds(r0, t), 0:HEAD] = dkT[kb, 0:HEAD, :].T
                dkv_ref[pl.ds(r0, t), HEAD:2 * HEAD] = dvT[kb].T
                dkrT[kb] += dkT[kb, HEAD:2 * HEAD, :]
                return c

            lax.fori_loop(0, nkb, flush, 0)

        @pl.when(last_q & (h == pl.num_programs(0) - 1))
        def _():
            def flush(kb, c):
                r0 = pl.multiple_of(kb * t, t)
                dkr_ref[pl.ds(r0, t), :] = dkrT[kb].T
                return c

            lax.fori_loop(0, nkb, flush, 0)

    qblk = pl.BlockSpec((t, 2 * HEAD), lambda h, i: (i, h))
    kvfull = pl.BlockSpec((S, 2 * HEAD), lambda h, i: (0, h))
    krfull = pl.BlockSpec((S, LANE), lambda h, i: (0, 0))
    oblk = pl.BlockSpec((t, HEAD), lambda h, i: (i, h))
    return _pcall(
        body, name=name, grid=(MLA_HEADS, S // t),
        in_specs=[qblk, kvfull, krfull, oblk, pl.BlockSpec((1, t, 1), lambda h, i: (h, i, 0)), oblk,
                  pl.BlockSpec((t, LANE), lambda h, i: (i, 0)), pl.BlockSpec((t, LANE), lambda h, i: (i, 0))],
        out_specs=[qblk, kvfull, krfull],
        out_shape=[jax.ShapeDtypeStruct(q.shape, BF16), jax.ShapeDtypeStruct(kv.shape, F32), jax.ShapeDtypeStruct((S, LANE), F32)],
        scratch_shapes=[pltpu.VMEM((S, 2 * HEAD), BF16), pltpu.VMEM((t, 2 * HEAD), F32), pltpu.VMEM((nkb, 2 * HEAD, t), F32),
                        pltpu.VMEM((nkb, HEAD, t), F32), pltpu.VMEM((nkb, LANE, t), F32)],
        compiler_params=_params("arbitrary", "arbitrary"),
    )(q, kv, kr, o, lse, do, cs, sn)


GATE_TC = 512


def merge_fwd(z, ya, yb, yc, name):
    S = z.shape[0]
    tr, tc = _pick(S, 512), GATE_TC
    g0 = Z_GATE // tc
    nc = D_MODEL // tc

    def body(ga_ref, gb_ref, gc_ref, ya_ref, yb_ref, yc_ref, o_ref):
        acc = None
        for g, y in ((ga_ref, ya_ref), (gb_ref, yb_ref), (gc_ref, yc_ref)):
            term = _sigmoid(g[...].astype(F32)) * y[...].astype(F32)
            acc = term if acc is None else acc + term
        o_ref[...] = acc.astype(BF16)

    gate = lambda b: pl.BlockSpec((tr, tc), lambda i, j: (i, g0 + b * nc + j))
    blk = pl.BlockSpec((tr, tc), lambda i, j: (i, j))
    return _pcall(
        body, name=name, grid=(S // tr, nc), in_specs=[gate(0), gate(1), gate(2), blk, blk, blk], out_specs=blk,
        out_shape=jax.ShapeDtypeStruct((S, D_MODEL), BF16), compiler_params=_params("parallel", "parallel"),
    )(z, z, z, ya, yb, yc)


def merge_bwd(z, ya, yb, yc, dm, name):
    S = z.shape[0]
    tr, tc = _pick(S, 512), GATE_TC
    g0 = Z_GATE // tc
    nc = D_MODEL // tc

    def body(ga_ref, gb_ref, gc_ref, ya_ref, yb_ref, yc_ref, dm_ref, da_ref, db_ref, dc_ref, dga_ref, dgb_ref, dgc_ref):
        dmv = dm_ref[...].astype(F32)
        for g, y, dy, dg in ((ga_ref, ya_ref, da_ref, dga_ref), (gb_ref, yb_ref, db_ref, dgb_ref), (gc_ref, yc_ref, dc_ref, dgc_ref)):
            sg = _sigmoid(g[...].astype(F32))
            dy[...] = (dmv * sg).astype(BF16)
            dg[...] = (dmv * y[...].astype(F32) * sg * (1.0 - sg)).astype(BF16)

    gate = lambda b: pl.BlockSpec((tr, tc), lambda i, j: (i, g0 + b * nc + j))
    blk = pl.BlockSpec((tr, tc), lambda i, j: (i, j))
    out = jax.ShapeDtypeStruct((S, D_MODEL), BF16)
    return _pcall(
        body, name=name, grid=(S // tr, nc), in_specs=[gate(0), gate(1), gate(2), blk, blk, blk, blk],
        out_specs=[blk] * 6, out_shape=[out] * 6, compiler_params=_params("parallel", "parallel"),
    )(z, z, z, ya, yb, yc, dm)


def loss_head(x, g, target, name):
    S, D = x.shape
    tr = _pick(S, 512)

    def body(x_ref, g_ref, t_ref, l_ref, dx_ref, dg_ref):
        @pl.when(pl.program_id(0) == 0)
        def _():
            l_ref[...] = jnp.zeros_like(l_ref)
            dg_ref[...] = jnp.zeros_like(dg_ref)

        xv, gv = x_ref[...], g_ref[...]
        r = lax.rsqrt(jnp.mean(xv * xv, axis=-1, keepdims=True) + NORM_EPS)
        diff = xv * r * gv - t_ref[...]
        l_ref[...] += 0.5 * jnp.sum(jnp.mean(diff * diff, axis=-1, keepdims=True), axis=0, keepdims=True)
        dx, dg = _rms_bwd_math(xv, gv, diff * (1.0 / D))
        dx_ref[...] = dx
        dg_ref[...] += dg

    row = pl.BlockSpec((tr, D), lambda i: (i, 0))
    vec = pl.BlockSpec((1, D), lambda i: (0, 0))
    return _pcall(
        body, name=name, grid=(S // tr,), in_specs=[row, vec, row],
        out_specs=[pl.BlockSpec((1, LANE), lambda i: (0, 0)), row, vec],
        out_shape=[jax.ShapeDtypeStruct((1, LANE), F32), jax.ShapeDtypeStruct((S, D), F32), jax.ShapeDtypeStruct((1, D), F32)],
        compiler_params=_params("arbitrary"),
    )(x, g, target)


def _peer(k, x, y, c):
    px = 1 - x if k & 4 else x
    py = 1 - y if k & 2 else y
    pc = 1 - c if k & 1 else c
    return (px, py, pc), 4 * px + 2 * py + pc


def exchange(arrs, gather, name):
    n = len(arrs)
    shapes = [((N_DEV,) + a.shape) if gather else a.shape for a in arrs]

    def body(*refs):
        ins, outs = refs[:n], refs[n:2 * n]
        send_sems, recv_sems, loc_sems = refs[2 * n:]
        x, y, c = lax.axis_index("x"), lax.axis_index("y"), lax.axis_index("c")
        me = 4 * x + 2 * y + c
        sends, recvs, locs = [], [], []
        for a in range(n):
            loc = pltpu.make_async_copy(ins[a] if gather else ins[a].at[me], outs[a].at[me], loc_sems.at[a])
            loc.start()
            locs.append(loc)
            for k in range(1, N_DEV):
                peer, pid = _peer(k, x, y, c)
                s = a * (N_DEV - 1) + k - 1
                src = ins[a] if gather else ins[a].at[pid]
                snd = pltpu.make_async_remote_copy(src_ref=src, dst_ref=outs[a].at[me], send_sem=send_sems.at[s],
                                                   recv_sem=recv_sems.at[s], device_id=peer, device_id_type=pl.DeviceIdType.MESH)
                snd.start()
                sends.append(snd)
                recvs.append(pltpu.make_async_remote_copy(src_ref=src, dst_ref=outs[a].at[pid], send_sem=send_sems.at[s],
                                                          recv_sem=recv_sems.at[s], device_id=peer, device_id_type=pl.DeviceIdType.MESH))
        for snd, rcv in zip(sends, recvs):
            snd.wait_send()
            rcv.wait_recv()
        for loc in locs:
            loc.wait()

    any_spec = pl.BlockSpec(memory_space=pl.ANY)
    outs = _pcall(
        body, name=name, in_specs=[any_spec] * n, out_specs=[any_spec] * n,
        out_shape=[jax.ShapeDtypeStruct(s, a.dtype) for s, a in zip(shapes, arrs)],
        scratch_shapes=[pltpu.SemaphoreType.DMA((n * (N_DEV - 1),)), pltpu.SemaphoreType.DMA((n * (N_DEV - 1),)),
                        pltpu.SemaphoreType.DMA((n,))],
        compiler_params=pltpu.CompilerParams(has_side_effects=True),
    )(*arrs)
    return list(outs)


_HBM = pl.BlockSpec(memory_space=pltpu.HBM)
_SEM = pl.BlockSpec(memory_space=pltpu.SEMAPHORE)
_EFFECT = pltpu.SideEffectType.DATAFLOW_SIDE_EFFECTING


def _peer_copies(srcs, lands, send_sems, recv_sems, gather):
    x, y, c = lax.axis_index("x"), lax.axis_index("y"), lax.axis_index("c")
    me = 4 * x + 2 * y + c
    out = []
    for a, (src, land) in enumerate(zip(srcs, lands)):
        for k in range(1, N_DEV):
            peer, pid = _peer(k, x, y, c)
            s = a * (N_DEV - 1) + k - 1
            mk = lambda dst: pltpu.make_async_remote_copy(
                src_ref=src if gather else src.at[pid], dst_ref=dst, send_sem=send_sems.at[s], recv_sem=recv_sems.at[s],
                device_id=peer, device_id_type=pl.DeviceIdType.MESH)
            out.append((mk(land.at[me]), mk(land.at[pid])))
    return out


def exchange_start(arrs, gather, after, name):
    n = len(arrs)
    nsem = n * (N_DEV - 1)
    lands = [lax.empty(((N_DEV,) + a.shape) if gather else a.shape, a.dtype) for a in arrs]

    def body(*refs):
        srcs, land_refs = refs[:n], refs[n:2 * n]
        send_sems, recv_sems = refs[2 * n + 1], refs[2 * n + 2]
        token = refs[-1]
        for snd, _ in _peer_copies(srcs, land_refs, send_sems, recv_sems, gather):
            snd.start()
        token[...] = jnp.zeros_like(token)

    hbm = lambda a: pltpu.HBM(a.shape, a.dtype)
    outs = _pcall(
        body, name=name, in_specs=[_HBM] * (2 * n) + [pl.BlockSpec(memory_space=pl.ANY)],
        out_specs=[_SEM, _SEM] + [_HBM] * (2 * n) + [pl.BlockSpec(memory_space=pltpu.VMEM)],
        out_shape=[pltpu.SemaphoreType.DMA((nsem,)), pltpu.SemaphoreType.DMA((nsem,))] + [hbm(a) for a in arrs]
        + [hbm(a) for a in lands] + [jax.ShapeDtypeStruct((SUBLANE, LANE), F32)],
        input_output_aliases={i: i + 2 for i in range(2 * n)},
        compiler_params=pltpu.CompilerParams(has_side_effects=_EFFECT),
    )(*[pltpu.with_memory_space_constraint(a, pltpu.HBM) for a in list(arrs) + lands], after)
    return (outs[0], outs[1], list(outs[2:2 + n]), list(outs[2 + n:2 + 2 * n])), outs[-1]


def exchange_wait(handle, gather, after, name):
    send_sems, recv_sems, srcs, lands = handle
    n = len(srcs)

    def body(*refs):
        src_refs, land_refs = refs[:n], refs[n:2 * n]
        for snd, rcv in _peer_copies(src_refs, land_refs, refs[2 * n], refs[2 * n + 1], gather):
            snd.wait_send()
            rcv.wait_recv()

    hbm = lambda a: pltpu.HBM(a.shape, a.dtype)
    outs = _pcall(
        body, name=name, in_specs=[_HBM] * (2 * n) + [_SEM, _SEM, pl.BlockSpec(memory_space=pl.ANY)],
        out_specs=[_HBM] * (2 * n), out_shape=[hbm(a) for a in srcs] + [hbm(a) for a in lands],
        input_output_aliases={i: i for i in range(2 * n)},
        compiler_params=pltpu.CompilerParams(has_side_effects=_EFFECT),
    )(*srcs, *lands, send_sems, recv_sems, after)
    return list(outs[n:])


def _my_index():
    return 4 * lax.axis_index("x") + 2 * lax.axis_index("y") + lax.axis_index("c")


def adamw_sum(parts, w, m, v, name):
    L, R, C = w.shape
    tr = R
    for cand in (512, 352, 256, 128, 64, 48, 32, 16, 8):
        if R % cand == 0 and cand * C * 4 <= 2 * 1024 * 1024:
            tr = cand
            break
    c1 = 1.0 - ADAM_B1 ** ADAM_STEP
    c2 = 1.0 - ADAM_B2 ** ADAM_STEP

    def body(p_ref, w_ref, m_ref, v_ref, g_ref, d_ref, nm_ref, nv_ref):
        g = p_ref[0, 0].astype(F32)
        for k in range(1, N_DEV):
            g = g + p_ref[k, 0].astype(F32)
        m2 = ADAM_B1 * m_ref[0] + (1.0 - ADAM_B1) * g
        v2 = ADAM_B2 * v_ref[0] + (1.0 - ADAM_B2) * (g * g)
        g_ref[0] = g
        nm_ref[0] = m2
        nv_ref[0] = v2
        d_ref[0] = -ADAM_LR * ((m2 / c1) / (jnp.sqrt(v2 / c2) + ADAM_EPS) + ADAM_WD * w_ref[0])

    blk = pl.BlockSpec((1, tr, C), lambda l, i: (l, i, 0))
    out = jax.ShapeDtypeStruct((L, R, C), F32)
    return _pcall(
        body, name=name, grid=(L, R // tr),
        in_specs=[pl.BlockSpec((N_DEV, 1, tr, C), lambda l, i: (0, l, i, 0)), blk, blk, blk],
        out_specs=[blk] * 4, out_shape=[out] * 4, compiler_params=_params("parallel", "parallel"),
    )(parts, w, m, v)


def adamw_layer(parts, w, m, v, layer, prev, name):
    L, R, C = w.shape
    tr = R
    for cand in (512, 352, 256, 128, 64, 48, 32, 16, 8):
        if R % cand == 0 and cand * C * 4 <= 2 * 1024 * 1024:
            tr = cand
            break
    c1 = 1.0 - ADAM_B1 ** ADAM_STEP
    c2 = 1.0 - ADAM_B2 ** ADAM_STEP
    n_prev = 0 if prev is None else 4

    def body(*refs):
        p_ref, w_ref, m_ref, v_ref = refs[:4]
        g_ref, d_ref, nm_ref, nv_ref = refs[4 + n_prev:]
        g = p_ref[0].astype(F32)
        for k in range(1, N_DEV):
            g = g + p_ref[k].astype(F32)
        m2 = ADAM_B1 * m_ref[0] + (1.0 - ADAM_B1) * g
        v2 = ADAM_B2 * v_ref[0] + (1.0 - ADAM_B2) * (g * g)
        g_ref[0] = g
        nm_ref[0] = m2
        nv_ref[0] = v2
        d_ref[0] = -ADAM_LR * ((m2 / c1) / (jnp.sqrt(v2 / c2) + ADAM_EPS) + ADAM_WD * w_ref[0])

    blk = pl.BlockSpec((1, tr, C), lambda i: (layer, i, 0))
    out = jax.ShapeDtypeStruct((L, R, C), F32)
    return _pcall(
        body, name=name, grid=(R // tr,),
        in_specs=[pl.BlockSpec((N_DEV, tr, C), lambda i: (0, i, 0)), blk, blk, blk] + [pl.BlockSpec(memory_space=pl.ANY)] * n_prev,
        out_specs=[blk] * 4, out_shape=[out] * 4, input_output_aliases={4 + j: j for j in range(n_prev)},
        compiler_params=_params("parallel"),
    )(parts, w, m, v, *(prev or ()))


def _cols_full(g):
    return jnp.transpose(g, (1, 0, 2)).reshape(g.shape[1], N_DEV * g.shape[2])


def _cols_shards(w):
    R = w.shape[0]
    return jnp.transpose(w.reshape(R, N_DEV, w.shape[1] // N_DEV), (1, 0, 2))


def _w_in_to_z(w):
    kr0 = Z_GATE
    gate0 = Z_GATE + MLA_ROPE
    pad = jnp.zeros((w.shape[0], Z_W - Z_KR - MLA_ROPE), w.dtype)
    return jnp.concatenate([w[:, :kr0], w[:, gate0:], w[:, kr0:gate0], pad], axis=1)


def _z_to_w_in(dw):
    return jnp.concatenate([dw[:, :Z_GATE], dw[:, Z_KR:Z_KR + MLA_ROPE], dw[:, Z_GATE:Z_KR]], axis=1)


def _block_diag(w):
    eye = jnp.eye(RG_BLOCKS, dtype=w.dtype)
    return (w[:, :, None, :] * eye[:, None, :, None]).reshape(D_MODEL, D_MODEL).astype(BF16)


def _diag_blocks(d):
    d4 = d.reshape(RG_BLOCKS, RG_BLOCK_W, RG_BLOCKS, RG_BLOCK_W)
    return jnp.stack([d4[n, :, n, :] for n in range(RG_BLOCKS)], axis=0)


def _uq_full(g):
    p = jnp.pad(g, ((0, 0), (0, 0), (0, 2 * HEAD - HEAD - MLA_ROPE)))
    return jnp.transpose(p, (1, 0, 2)).reshape(MLA_LORA, MLA_HEADS * 2 * HEAD)


def _uq_shards(dw):
    return jnp.transpose(dw.reshape(MLA_LORA, MLA_HEADS, 2 * HEAD), (1, 0, 2))[:, :, :HEAD + MLA_ROPE]


SMALL = ("ffn1_norm", "mix_norm", "conv_b", "rg_w_a", "rg_b_a", "rg_w_x", "rg_b_x", "rg_lambda", "mla_q_norm",
         "mla_kv_norm", "ffn2_norm", "final_norm")
BIG = ("ffn1_w_gate_up", "ffn1_w_down", "w_in", "conv_w", "mla_w_uq", "mla_w_ukv", "w_branch_a", "w_branch_b",
       "w_branch_c", "w_out", "ffn2_w_gate_up", "ffn2_w_down")
ROW_SHARDED = ("ffn1_w_down", "w_branch_a", "w_branch_b", "w_branch_c", "w_out", "ffn2_w_down")


FIRST = ("ffn1_w_gate_up", "ffn1_w_down")
REST = tuple(k for k in BIG if k not in FIRST)


def _full_weights(g):
    fw = {}
    for k, s in g.items():
        if k in ROW_SHARDED:
            fw[k] = s.reshape(-1, s.shape[-1])
        elif k == "w_in":
            fw[k] = _w_in_to_z(_cols_full(s))
        elif k == "mla_w_uq":
            fw[k] = _uq_full(s)
        else:
            fw[k] = _cols_full(s)
    return fw


def _grad_shards(dw):
    out = {}
    for k, g in dw.items():
        if k in ROW_SHARDED:
            out[k] = g.reshape(N_DEV, g.shape[0] // N_DEV, g.shape[1])
        elif k == "w_in":
            out[k] = _cols_shards(_z_to_w_in(g))
        elif k == "mla_w_uq":
            out[k] = _uq_shards(g)
        else:
            out[k] = _cols_shards(g)
    return out


def ffn_fwd(x, norm, w_gu, w_d, tag):
    h = rms_fwd(x, norm, f"{tag}_rms")
    g, u, a = ffn_up(h, w_gu, f"{tag}_up")
    y = matmul(a, w_d, "nn", F32, f"{tag}_down", scale=0.5, res=x, tk=1408)
    return y, (x, h, g, u, a)


def ffn_bwd(dy, saved, norm, w_gu, w_d, tag):
    x, h, g, u, a = saved
    dg, du = ffn_dact(dy, w_d, g, u, f"{tag}_dact")
    dw_d = matmul(a, dy, "tn", BF16, f"{tag}_dwd", scale=0.5, tm=1408)
    dw_gu = jnp.concatenate([matmul(h, dg, "tn", BF16, f"{tag}_dwg", tn=1408), matmul(h, du, "tn", BF16, f"{tag}_dwu", tn=1408)], axis=1)
    dh = matmul(dg, w_gu, "nt", F32, f"{tag}_dhg", tk=1408)
    dh = matmul(du, w_gu, "nt", F32, f"{tag}_dhu", tk=1408, res=dh, b_koff=D_FF // 1408)
    dx, dnorm = rms_bwd(x, norm, dh, dy, f"{tag}_drms")
    return dx, dw_gu, dw_d, dnorm


def mixer_fwd(x, sp, fw, cs, sn, tag):
    h = rms_fwd(x, sp["mix_norm"], f"{tag}_rms")
    z = matmul(h, fw["w_in"], "nn", BF16, f"{tag}_in", tn=1280)
    wa, wx = _block_diag(sp["rg_w_a"]), _block_diag(sp["rg_w_x"])
    ya, hs = rglru_fwd(z, fw["conv_w"], sp["conv_b"], wa, wx, sp["rg_b_a"], sp["rg_b_x"], sp["rg_lambda"], f"{tag}_rg")
    yb, ltot = sb_fwd(z, f"{tag}_sb")
    cqn, ckvn, krope = mla_prep_fwd(z, sp["mla_q_norm"], sp["mla_kv_norm"], cs, sn, f"{tag}_mprep")
    q = uq_rope(cqn, fw["mla_w_uq"], cs, sn, f"{tag}_uq")
    kv = matmul(ckvn, fw["mla_w_ukv"], "nn", BF16, f"{tag}_ukv")
    yc, lse = mla_fwd(q, kv, krope, f"{tag}_mla")
    pa = matmul(ya, fw["w_branch_a"], "nn", BF16, f"{tag}_pa")
    pb = matmul(yb, fw["w_branch_b"], "nn", BF16, f"{tag}_pb")
    pc = matmul(yc, fw["w_branch_c"], "nn", BF16, f"{tag}_pc")
    merged = merge_fwd(z, pa, pb, pc, f"{tag}_merge")
    y = matmul(merged, fw["w_out"], "nn", F32, f"{tag}_out", res=x)
    return y, (x, h, z, wa, wx, ya, hs, yb, ltot, cqn, ckvn, krope, q, kv, yc, lse, pa, pb, pc, merged)


def mixer_bwd(dy, saved, sp, fw, cs, sn, tag):
    x, h, z, wa, wx, ya, hs, yb, ltot, cqn, ckvn, krope, q, kv, yc, lse, pa, pb, pc, merged = saved
    S = x.shape[0]
    dw, ds = {}, {}
    dmerged = matmul(dy, fw["w_out"], "nt", BF16, f"{tag}_dmerged")
    dw["w_out"] = matmul(merged, dy, "tn", BF16, f"{tag}_dwout")
    dpa, dpb, dpc, dga, dgb, dgc = merge_bwd(z, pa, pb, pc, dmerged, f"{tag}_dmerge")
    dya = matmul(dpa, fw["w_branch_a"], "nt", BF16, f"{tag}_dya")
    dyb = matmul(dpb, fw["w_branch_b"], "nt", BF16, f"{tag}_dyb")
    dyc = matmul(dpc, fw["w_branch_c"], "nt", BF16, f"{tag}_dyc")
    dw["w_branch_a"] = matmul(ya, dpa, "tn", BF16, f"{tag}_dwa")
    dw["w_branch_b"] = matmul(yb, dpb, "tn", BF16, f"{tag}_dwb")
    dw["w_branch_c"] = matmul(yc, dpc, "tn", BF16, f"{tag}_dwc")
    drgx, drgg, dwa, dwx, dvec = rglru_bwd(z, hs, dya, fw["conv_w"], sp["conv_b"], wa, wx, sp["rg_b_a"], sp["rg_b_x"],
                                           sp["rg_lambda"], f"{tag}_drg")
    ds["rg_w_a"], ds["rg_w_x"] = _diag_blocks(dwa), _diag_blocks(dwx)
    ds["rg_b_a"], ds["rg_b_x"], ds["rg_lambda"], ds["conv_b"] = dvec[0], dvec[1], dvec[2], dvec[3]
    dw["conv_w"] = dvec[4:8]
    dsq, dsk, dsv = sb_bwd(z, ltot, dyb, f"{tag}_dsb")
    dqp, dkv, dkr = mla_bwd(q, kv, krope, yc, lse, dyc, cs, sn, f"{tag}_dmla")
    dw["mla_w_uq"] = matmul(cqn, dqp, "tn", BF16, f"{tag}_dwuq")
    dw["mla_w_ukv"] = matmul(ckvn, dkv, "tn", BF16, f"{tag}_dwukv")
    dcqn = matmul(dqp, fw["mla_w_uq"], "nt", F32, f"{tag}_dcqn")
    dckvn = matmul(dkv, fw["mla_w_ukv"], "nt", F32, f"{tag}_dckvn")
    dcq, dckv, dkrr, dqn, dkvn = mla_prep_bwd(z, sp["mla_q_norm"], sp["mla_kv_norm"], cs, sn, dcqn, dckvn, dkr, f"{tag}_dmprep")
    ds["mla_q_norm"], ds["mla_kv_norm"] = dqn[0], dkvn[0]
    dz = jnp.concatenate([drgx, drgg, dsq.astype(BF16), dsk.astype(BF16), dsv.astype(BF16), dcq, dckv, dga, dgb, dgc, dkrr,
                          jnp.zeros((S, Z_W - Z_KR - LANE), BF16)], axis=1)
    dw["w_in"] = matmul(h, dz, "tn", BF16, f"{tag}_dwin", tn=1280)
    dh = matmul(dz, fw["w_in"], "nt", F32, f"{tag}_dh", tk=1280)
    dx, dnorm = rms_bwd(x, sp["mix_norm"], dh, dy, f"{tag}_drms")
    ds["mix_norm"] = dnorm[0]
    return dx, dw, ds


def _rope_tables(positions):
    inv = ROPE_THETA ** (-jnp.arange(0, MLA_ROPE, 2, dtype=F32) / MLA_ROPE)
    ang = positions.astype(F32)[:, None] * inv
    zeros = jnp.zeros((positions.shape[0], LANE - MLA_ROPE), F32)
    cs = jnp.concatenate([jnp.cos(ang), jnp.cos(ang), zeros], axis=1)
    sn = jnp.concatenate([jnp.sin(ang), jnp.sin(ang), zeros], axis=1)
    return cs, sn


def local_step(x, positions, target, small, fetch, emit):
    L = small["ffn1_norm"].shape[0]
    cs, sn = _rope_tables(positions)
    row = lambda v: v.reshape(1, -1)
    saved = []
    for l in range(L):
        sp = {k: small[k][l] for k in SMALL if k != "final_norm"}
        sp = {k: (v if v.ndim == 3 else row(v)) for k, v in sp.items()}
        g, token = fetch(l, "first", x)
        fw = _full_weights(g)
        if token is not None:
            sp["ffn1_norm"] = sp["ffn1_norm"] + token[0:1, 0:1]
        x, s1 = ffn_fwd(x, sp["ffn1_norm"], fw["ffn1_w_gate_up"], fw["ffn1_w_down"], f"l{l}_f1")
        g, token = fetch(l, "rest", x)
        fw.update(_full_weights(g))
        if token is not None:
            sp["mix_norm"] = sp["mix_norm"] + token[0:1, 0:1]
        x, s2 = mixer_fwd(x, sp, fw, cs, sn, f"l{l}_mx")
        x, s3 = ffn_fwd(x, sp["ffn2_norm"], fw["ffn2_w_gate_up"], fw["ffn2_w_down"], f"l{l}_f2")
        saved.append((fw, sp, s1, s2, s3))
    loss, dx, dfinal = loss_head(x, row(small["final_norm"]), target, "loss_head")
    for l in reversed(range(L)):
        fw, sp, s1, s2, s3 = saved[l]
        dx, dgu2, dd2, dn2 = ffn_bwd(dx, s3, sp["ffn2_norm"], fw["ffn2_w_gate_up"], fw["ffn2_w_down"], f"l{l}_f2")
        dx, dw, ds = mixer_bwd(dx, s2, sp, fw, cs, sn, f"l{l}_mx")
        dw.update(ffn2_w_gate_up=dgu2, ffn2_w_down=dd2)
        dx = emit(l, "rest", _grad_shards(dw), dx)
        dx, dgu1, dd1, dn1 = ffn_bwd(dx, s1, sp["ffn1_norm"], fw["ffn1_w_gate_up"], fw["ffn1_w_down"], f"l{l}_f1")
        ds.update(ffn1_norm=dn1[0], ffn2_norm=dn2[0])
        if l == L - 1:
            ds["final_norm"] = dfinal[0]
        dx = emit(l, "first", _grad_shards(dict(ffn1_w_gate_up=dgu1, ffn1_w_down=dd1)), dx, ds)
    return loss[0, 0], dx


def _pack_small(tree):
    flat = jnp.concatenate([tree[k].reshape(-1).astype(F32) for k in SMALL])
    rows = -(-flat.shape[0] // (SUBLANE * D_MODEL)) * SUBLANE
    return jnp.pad(flat, (0, rows * D_MODEL - flat.shape[0])).reshape(rows, D_MODEL)


SMALL_LAYER = tuple(k for k in SMALL if k != "final_norm")


def _pack_rows(flat):
    rows = -(-flat.shape[-1] // (SUBLANE * D_MODEL)) * SUBLANE
    pad = [(0, 0)] * (flat.ndim - 1) + [(0, rows * D_MODEL - flat.shape[-1])]
    return jnp.pad(flat, pad).reshape(flat.shape[:-1] + (rows, D_MODEL))


def _pack_layer_small(ds):
    tail = ds.get("final_norm", jnp.zeros((D_MODEL,), F32))
    return _pack_rows(jnp.concatenate([ds[k].reshape(-1).astype(F32) for k in SMALL_LAYER] + [tail]))


def _small_parts(landed, like):
    L = len(landed)
    flats = [p.reshape(N_DEV, -1) for p in landed]
    pieces, off = {}, 0
    for k in SMALL_LAYER:
        n = like[k][0].size
        pieces[k] = jnp.concatenate([f[:, off:off + n] for f in flats], axis=1)
        off += n
    pieces["final_norm"] = flats[L - 1][:, off:off + D_MODEL]
    return _pack_rows(jnp.concatenate([pieces[k] for k in SMALL], axis=1))


def _unpack_small(buf, like):
    flat = buf.reshape(-1)
    out, off = {}, 0
    for k in SMALL:
        n = like[k].size
        out[k] = flat[off:off + n].reshape(like[k].shape)
        off += n
    return out


NAMES = ("ffn1_norm", "ffn1_w_gate_up", "ffn1_w_down", "mix_norm", "w_in", "conv_w", "conv_b", "rg_w_a", "rg_b_a", "rg_w_x",
         "rg_b_x", "rg_lambda", "mla_q_norm", "mla_w_uq", "mla_kv_norm", "mla_w_ukv", "w_branch_a", "w_branch_b",
         "w_branch_c", "w_out", "ffn2_norm", "ffn2_w_gate_up", "ffn2_w_down", "final_norm")


def kernel(x, positions, *rest):
    n = len(NAMES)
    w = dict(zip(NAMES, rest[:n]))
    target = rest[n]
    m = dict(zip(NAMES, rest[n + 1:2 * n + 1]))
    v = dict(zip(NAMES, rest[2 * n + 1:3 * n + 1]))
    L = w["ffn1_norm"].shape[0]
    me = _my_index()

    stages = [(0, FIRST), (0, REST)] + [(l, BIG) for l in range(1, L)]
    shard = lambda l, k: w[k][l] if k == "conv_w" else w[k][l].astype(BF16)

    pending, got = {}, {}

    def gather_start(s, after):
        l, names = stages[s]
        mine = [shard(l, k) for k in names]
        handle, token = exchange_start(mine, True, after, f"gather_start_{s}")
        pending[(l, names[0])] = (s, handle, mine)
        return token

    def fetch(l, part, after):
        token = None
        key = (l, FIRST[0] if part == "first" else REST[0])
        if key in pending:
            s, handle, mine = pending.pop(key)
            landed = exchange_wait(handle, True, after, f"gather_wait_{s}")
            if s + 1 < len(stages):
                token = gather_start(s + 1, landed[0])
            filled = [lax.dynamic_update_slice_in_dim(g, a[None], me, 0) for g, a in zip(landed, mine)]
            got.update({(l, k): a for k, a in zip(stages[s][1], filled)})
        return {k: got.pop((l, k)) for k in (FIRST if part == "first" else REST)}, token

    gather_start(0, x)

    flying, stash, res, small_landed = [], {}, {}, [None] * L

    def land(after):
        s, handle, own = flying.pop()
        l, names = stages[s]
        landed = exchange_wait(handle, False, after, f"scatter_wait_{s}")
        for k, g, o in zip(names, landed, own):
            parts = lax.dynamic_update_slice_in_dim(g, o, me, 0)
            res[k] = adamw_layer(parts, w[k], m[k], v[k], l, res.get(k), f"adamw_{k}_{l}")
        if len(landed) > len(names):
            small_landed[l] = lax.dynamic_update_slice_in_dim(landed[-1], own[-1], me, 0)
        return landed[0]

    def emit(l, part, gshards, dx, small_grads=None):
        stash.update(gshards)
        s = next(i for i, (sl, names) in enumerate(stages) if sl == l and (names[0] == FIRST[0]) == (part == "first" or l > 0))
        if l > 0 and part == "rest":
            return dx
        send = [stash.pop(k) for k in stages[s][1]]
        own = [lax.dynamic_slice_in_dim(a, me, 1, 0) for a in send]
        if small_grads is not None:
            pack = _pack_layer_small(small_grads)
            send.append(jnp.broadcast_to(pack[None], (N_DEV,) + pack.shape))
            own.append(pack[None])
        after = land(dx) if flying else dx
        handle, token = exchange_start(send, False, after, f"scatter_start_{s}")
        flying.append((s, handle, own))
        return dx + token[0:1, 0:1]

    small = {k: w[k] for k in SMALL}
    loss, dx = local_step(x[0], positions[0], target[0], small, fetch, emit)
    loss = lax.psum(loss, ("x", "y", "c"))
    land(dx)
    small_parts = _small_parts(small_landed, small)
    packed = adamw_sum(small_parts[:, None], _pack_small(small)[None], _pack_small({k: m[k] for k in SMALL})[None],
                       _pack_small({k: v[k] for k in SMALL})[None], "adamw_small")
    unpacked = [_unpack_small(p[0], small) for p in packed]
    for k in SMALL:
        res[k] = tuple(u[k] for u in unpacked)

    outs = [loss, dx[None]]
    for i in range(4):
        outs += [res[k][i] for k in NAMES]
    return tuple(outs)
```
